```python
import jax, jax.numpy as jnp
from jax import lax
import numpy as np

D_MODEL = 1024
BATCH = 2
SEQ = 8192
DEPTH = 2

GRID_W = 64
CTX_LEN = 256
HEAD_DIM = 64
GROUP_W = D_MODEL // 4
A_HEADS = GROUP_W // HEAD_DIM
A_KV = A_HEADS // 2
B_HEADS = GROUP_W // HEAD_DIM
B_KV = B_HEADS // 2
F_CH = 64
F_GROUPS = GROUP_W // F_CH
R_HEADS = GROUP_W // HEAD_DIM
MIX_W = 4 * GROUP_W
Q_BLOCK = 128
WINDOW = 128
RET_CHUNK = 128
ROPE_THETA = 10000.0
FF_DIM = -(-8 * D_MODEL // (3 * 256)) * 256
ALPHA = (2.0 * DEPTH) ** 0.25
BETA = (8.0 * DEPTH) ** -0.25
NORM_EPS = 1e-6
NEG_INF = -1e30
F32 = jnp.float32
SPLITS = (A_HEADS * HEAD_DIM, A_KV * HEAD_DIM, A_KV * HEAD_DIM,
          B_HEADS * HEAD_DIM, B_KV * HEAD_DIM, B_KV * HEAD_DIM,
          F_GROUPS * F_CH,
          GROUP_W, GROUP_W, GROUP_W, GROUP_W)
SPLIT_IDX = tuple(int(i) for i in np.cumsum(SPLITS)[:-1])
IN_W = sum(SPLITS)

kernel_name = 'hybrid_parallel_group_dit_block'


def layer_norm(x, w=None, b=None):
    xf = x.astype(F32)
    mu = jnp.mean(xf, -1, keepdims=True)
    var = jnp.mean(jnp.square(xf - mu), -1, keepdims=True)
    y = (xf - mu) * lax.rsqrt(var + NORM_EPS)
    if w is not None:
        y = y * w.astype(F32) + b.astype(F32)
    return y.astype(x.dtype)


def rms_norm(x, w):
    xf = x.astype(F32)
    y = xf * lax.rsqrt(jnp.mean(jnp.square(xf), -1, keepdims=True) + NORM_EPS) * w.astype(F32)
    return y.astype(x.dtype)


def modulate(x, shift, scale):
    return layer_norm(x) * (1.0 + scale) + shift


def post_norm(x, y, w, b):
    return layer_norm(ALPHA * x + y, w, b)


def heads(t):
    return t.reshape(t.shape[:-1] + (t.shape[-1] // HEAD_DIM, HEAD_DIM))


def group_q(q, n_kv):
    return q.reshape(q.shape[:2] + (n_kv, q.shape[2] // n_kv, q.shape[3]))


def rope_tables(pos, n_freq):
    inv = ROPE_THETA ** (-jnp.arange(n_freq, dtype=F32) / n_freq)
    ang = pos[:, None] * inv[None, :]
    return jnp.cos(ang), jnp.sin(ang)


def rotate(x, cos, sin):
    f = cos.shape[-1]
    c = cos[None, :, None, :].astype(x.dtype)
    s = sin[None, :, None, :].astype(x.dtype)
    x1, x2 = x[..., :f], x[..., f:]
    return jnp.concatenate([x1 * c - x2 * s, x2 * c + x1 * s], -1)


def rope_2d(x, tab):
    cos_r, sin_r, cos_c, sin_c = tab
    h = x.shape[-1] // 2
    return jnp.concatenate([rotate(x[..., :h], cos_r, sin_r), rotate(x[..., h:], cos_c, sin_c)], -1)


def gqa_attend(q, k, v, sink):
    s = jnp.einsum('bqkgd,bskd->bkgqs', q, k).astype(F32) * (q.shape[-1] ** -0.5)
    if sink is not None:
        snk = jnp.broadcast_to(sink.astype(F32)[None, :, :, None, None], s.shape[:-1] + (1,))
        p = jax.nn.softmax(jnp.concatenate([s, snk], -1), axis=-1)[..., :-1]
    else:
        p = jax.nn.softmax(s, axis=-1)
    o = jnp.einsum('bkgqs,bskd->bqkgd', p.astype(v.dtype), v)
    return o.reshape(o.shape[:2] + (-1,))


def global_attention(q, k, v):
    B, S, KV, G, d = q.shape
    nb = S // Q_BLOCK
    qb = jnp.moveaxis(q.reshape(B, nb, Q_BLOCK, KV, G, d), 1, 0)
    o = lax.map(lambda blk: gqa_attend(blk, k, v, None), qb)
    return jnp.moveaxis(o, 0, 1).reshape(B, S, -1)


def window_attention(q, k, v, k_c, v_c, sink):
    B, S, KV, G, d = q.shape
    nb = S // Q_BLOCK
    w3 = 3 * Q_BLOCK
    lc = k_c.shape[1]
    qb = q.reshape(B, nb, Q_BLOCK, KV, G, d)

    def band(t):
        tp = jnp.pad(t, ((0, 0), (Q_BLOCK, Q_BLOCK), (0, 0), (0, 0))).reshape(B, nb + 2, Q_BLOCK, KV, d)
        return jnp.concatenate([tp[:, :-2], tp[:, 1:-1], tp[:, 2:]], axis=2)

    kw, vw = band(k), band(v)
    scale = d ** -0.5
    s_loc = jnp.einsum('bnqkgd,bnskd->bnkgqs', qb, kw).astype(F32) * scale
    blk = jnp.arange(nb, dtype=jnp.int32)[:, None]
    qpos = blk * Q_BLOCK + jnp.arange(Q_BLOCK, dtype=jnp.int32)[None, :]
    kpos = (blk - 1) * Q_BLOCK + jnp.arange(w3, dtype=jnp.int32)[None, :]
    kp = kpos[:, None, :]
    valid = (jnp.abs(kp - qpos[:, :, None]) <= WINDOW) & (kp >= 0) & (kp < S)
    s_loc = jnp.where(valid[None, :, None, None], s_loc, NEG_INF)
    s_ctx = jnp.einsum('bnqkgd,bskd->bnkgqs', qb, k_c).astype(F32) * scale
    s_snk = jnp.broadcast_to(sink.astype(F32)[None, None, :, :, None, None], s_loc.shape[:-1] + (1,))
    p = jax.nn.softmax(jnp.concatenate([s_loc, s_ctx, s_snk], -1), axis=-1)
    p_loc = p[..., :w3].astype(v.dtype)
    p_ctx = p[..., w3:w3 + lc].astype(v.dtype)
    o = (jnp.einsum('bnkgqs,bnskd->bnqkgd', p_loc, vw)
         + jnp.einsum('bnkgqs,bskd->bnqkgd', p_ctx, v_c))
    return o.reshape(B, S, -1)


def fourier_mix(u, f_mix):
    B, L, _ = u.shape
    ug = u.astype(F32).reshape(B, L, F_GROUPS, F_CH)
    z = jnp.fft.fft2(ug, axes=(1, 3), norm='ortho').real.astype(u.dtype)
    return jnp.einsum('blgc,gce->blge', z, f_mix).reshape(B, L, -1)


def retention_scan(q, k, v, log_g, state0):
    Bn, L, H, _ = q.shape
    n = L // RET_CHUNK

    def chunks(t):
        return t.reshape(Bn, n, RET_CHUNK, H, t.shape[-1]).transpose(1, 0, 3, 2, 4)

    idx = jnp.arange(RET_CHUNK, dtype=F32)
    diff = idx[:, None] - idx[None, :]
    decay_in = jnp.where(diff >= 0, jnp.exp(log_g[:, None, None] * jnp.maximum(diff, 0.0)), 0.0)
    xi = jnp.exp(log_g[:, None] * (idx + 1.0))[None, :, :, None]
    zeta = jnp.exp(log_g[:, None] * (RET_CHUNK - 1.0 - idx))[None, :, :, None]
    g_chunk = jnp.exp(log_g * RET_CHUNK)[None, :, None, None]

    def step(state, blk):
        qi, ki, vi = blk
        inner = jnp.einsum('bhqd,bhsd->bhqs', qi, ki) * decay_in
        o = jnp.einsum('bhqs,bhsv->bhqv', inner, vi) + jnp.einsum('bhqd,bhdv->bhqv', qi, state) * xi
        state = state * g_chunk + jnp.einsum('bhsd,bhsv->bhdv', ki * zeta, vi)
        return state, o

    state, o = lax.scan(step, state0, (chunks(q), chunks(k), chunks(v)))
    return o.transpose(1, 0, 3, 2, 4).reshape(Bn, L, H, v.shape[-1]), state


def retention_state(k, v, log_g):
    L = k.shape[1]
    w = jnp.exp(log_g[:, None] * (L - 1.0 - jnp.arange(L, dtype=F32))[None, :])
    return jnp.einsum('blhd,blhv,hl->bhdv', k, v, w)


def bidir_retention(ql, kl, vl, qc, kc, vc, log_g, need_ctx):
    Bn, _, H, dk = ql.shape
    zeros = jnp.zeros((Bn, H, dk, vl.shape[-1]), F32)
    o_l = 0.0
    o_c = 0.0
    for direction in range(2):
        f = (lambda t: t[:, ::-1]) if direction == 1 else (lambda t: t)
        lg = log_g[direction]
        if need_ctx:
            oc, s_ctx = retention_scan(f(qc), f(kc), f(vc), lg, zeros)
            o_c = o_c + f(oc)
        else:
            s_ctx = retention_state(f(kc), f(vc), lg)
        ol, _ = retention_scan(f(ql), f(kl), f(vl), lg, s_ctx)
        o_l = o_l + f(ol)
    return o_l, (o_c if need_ctx else None)


def retention_out(o, g, gn_w):
    mu = jnp.mean(o, -1, keepdims=True)
    var = jnp.mean(jnp.square(o - mu), -1, keepdims=True)
    on = ((o - mu) * lax.rsqrt(var + NORM_EPS)).reshape(o.shape[:2] + (-1,)) * gn_w.astype(F32)
    return (jax.nn.silu(g.astype(F32)) * on).astype(g.dtype)


def swiglu(h, w_gu, w_dn):
    g, u = jnp.split(h @ w_gu, 2, axis=-1)
    return (jax.nn.silu(g) * u) @ w_dn


def mixer(hl, hc, w_in, a_qn, a_kn, b_sink, f_mix, r_decay, r_gn, w_out, grid_tab, seq_tab, need_ctx):
    pl = jnp.split(hl @ w_in, SPLIT_IDX, axis=-1)
    pc = jnp.split(hc @ w_in, SPLIT_IDX, axis=-1)
    qa = rope_2d(rms_norm(heads(pl[0]), a_qn), grid_tab)
    ka = rope_2d(rms_norm(heads(pl[1]), a_kn), grid_tab)
    ka_c = rms_norm(heads(pc[1]), a_kn)
    va, va_c = heads(pl[2]), heads(pc[2])
    a_l = global_attention(group_q(qa, A_KV), jnp.concatenate([ka, ka_c], 1), jnp.concatenate([va, va_c], 1))
    sink = b_sink.reshape(B_KV, B_HEADS // B_KV)
    qb = rope_2d(heads(pl[3]), grid_tab)
    kb = rope_2d(heads(pl[4]), grid_tab)
    kb_c, vb, vb_c = heads(pc[4]), heads(pl[5]), heads(pc[5])
    b_l = window_attention(group_q(qb, B_KV), kb, vb, kb_c, vb_c, sink)
    f_l = fourier_mix(pl[6], f_mix)
    log_g = jax.nn.log_sigmoid(r_decay.astype(F32))
    cos_t, sin_t = seq_tab
    kscale = HEAD_DIM ** -0.5
    qr = rotate(heads(pl[7]), cos_t, sin_t).astype(F32)
    kr = rotate(heads(pl[8]), cos_t, sin_t).astype(F32) * kscale
    o_l, o_c = bidir_retention(qr, kr, heads(pl[9]).astype(F32),
                               heads(pc[7]).astype(F32), heads(pc[8]).astype(F32) * kscale,
                               heads(pc[9]).astype(F32), log_g, need_ctx)
    r_l = retention_out(o_l, pl[10], r_gn)
    y_l = jnp.concatenate([a_l, b_l, f_l, r_l], -1) @ w_out
    if not need_ctx:
        return y_l, None
    a_c = gqa_attend(group_q(rms_norm(heads(pc[0]), a_qn), A_KV), ka_c, va_c, None)
    b_c = gqa_attend(group_q(heads(pc[3]), B_KV), kb_c, vb_c, sink)
    f_c = fourier_mix(pc[6], f_mix)
    r_c = retention_out(o_c, pc[10], r_gn)
    y_c = jnp.concatenate([a_c, b_c, f_c, r_c], -1) @ w_out
    return y_l, y_c


def setup_inputs(seed: int = 0) -> dict:
    key = jax.random.key(seed)
    ks = jax.random.split(key, 20)
    D = D_MODEL

    def nrm(k, shape, s):
        return jax.random.normal(k, shape, F32) * s

    gamma0 = 1.0 - 2.0 ** (-5.0 - jnp.arange(R_HEADS, dtype=F32))
    decay_logit = jnp.log(gamma0 / (1.0 - gamma0))
    return {
        'x': nrm(ks[0], (BATCH, SEQ, D), 1.0),
        'c': nrm(ks[1], (BATCH, D), 1.0),
        'ctx': nrm(ks[2], (BATCH, CTX_LEN, D), 1.0),
        'c_ctx': nrm(ks[3], (D,), 1.0),
        'w_mod': nrm(ks[4], (DEPTH, D, 6 * D), 0.5 * D ** -0.5),
        'b_mod': nrm(ks[5], (DEPTH, 6 * D), 0.02),
        'w_in': nrm(ks[6], (DEPTH, D, IN_W), D ** -0.5),
        'a_q_norm': 1.0 + nrm(ks[7], (DEPTH, HEAD_DIM), 0.02),
        'a_k_norm': 1.0 + nrm(ks[8], (DEPTH, HEAD_DIM), 0.02),
        'b_sink': nrm(ks[9], (DEPTH, B_HEADS), 0.5),
        'f_mix': nrm(ks[10], (DEPTH, F_GROUPS, F_CH, F_CH), F_CH ** -0.5),
        'r_decay': decay_logit[None, None, :] + nrm(ks[11], (DEPTH, 2, R_HEADS), 0.05),
        'r_gn_w': 1.0 + nrm(ks[12], (DEPTH, GROUP_W), 0.02),
        'w_out': nrm(ks[13], (DEPTH, MIX_W, D), BETA * MIX_W ** -0.5),
        'ln1_w': 1.0 + nrm(ks[14], (DEPTH, D), 0.02),
        'ln1_b': nrm(ks[15], (DEPTH, D), 0.02),
        'w_gate_up': nrm(ks[16], (DEPTH, D, 2 * FF_DIM), D ** -0.5),
        'w_down': nrm(ks[17], (DEPTH, FF_DIM, D), BETA * FF_DIM ** -0.5),
        'ln2_w': 1.0 + nrm(ks[18], (DEPTH, D), 0.02),
        'ln2_b': nrm(ks[19], (DEPTH, D), 0.02),
    }


def reference(x, c, ctx, c_ctx, w_mod, b_mod, w_in, a_q_norm, a_k_norm, b_sink, f_mix, r_decay,
              r_gn_w, w_out, ln1_w, ln1_b, w_gate_up, w_down, ln2_w, ln2_b):
    S = x.shape[1]
    rows = S // GRID_W
    t = jnp.arange(rows * GRID_W, dtype=jnp.int32)
    row = (t // GRID_W).astype(F32)
    col = (t % GRID_W).astype(F32)
    axis_freq = HEAD_DIM // 4
    grid_tab = rope_tables(row, axis_freq) + rope_tables(col, axis_freq)
    seq_tab = rope_tables(t.astype(F32), HEAD_DIM // 2)
    for layer in range(DEPTH):
        need_ctx = layer < DEPTH - 1
        mod_l = jax.nn.silu(c) @ w_mod[layer] + b_mod[layer]
        sh1, sc1, g1, sh2, sc2, g2 = [m[:, None, :] for m in jnp.split(mod_l, 6, axis=-1)]
        mod_c = jax.nn.silu(c_ctx) @ w_mod[layer] + b_mod[layer]
        csh1, csc1, cg1, csh2, csc2, cg2 = jnp.split(mod_c, 6, axis=-1)
        y_l, y_c = mixer(modulate(x, sh1, sc1), modulate(ctx, csh1, csc1), w_in[layer],
                         a_q_norm[layer], a_k_norm[layer], b_sink[layer], f_mix[layer], r_decay[layer],
                         r_gn_w[layer], w_out[layer], grid_tab, seq_tab, need_ctx)
        x = post_norm(x, g1 * y_l, ln1_w[layer], ln1_b[layer])
        x = post_norm(x, g2 * swiglu(modulate(x, sh2, sc2), w_gate_up[layer], w_down[layer]),
                      ln2_w[layer], ln2_b[layer])
        if need_ctx:
            ctx = post_norm(ctx, cg1 * y_c, ln1_w[layer], ln1_b[layer])
            ctx = post_norm(ctx, cg2 * swiglu(modulate(ctx, csh2, csc2), w_gate_up[layer], w_down[layer]),
                            ln2_w[layer], ln2_b[layer])
    return x
```

```python
import functools

import numpy as np
import jax
import jax.numpy as jnp
from jax import lax
from jax.experimental import pallas as pl
from jax.experimental.pallas import tpu as pltpu

F32 = jnp.float32
BF16 = jnp.bfloat16
HIGHEST = lax.Precision.HIGHEST

HEAD_DIM = 64
GRID_W = 64
Q_BLOCK = 128
ROPE_THETA = 10000.0
NORM_EPS = 1e-6
NEG_INF = -1e30

LANES = 128
VMEM_LIMIT_BYTES = 56 * 1024 * 1024

RET_CHUNK = 256
FFT_L2 = 128


def _cparams(sem):
    return pltpu.CompilerParams(dimension_semantics=sem, vmem_limit_bytes=VMEM_LIMIT_BYTES)


def _ln(x):
    mu = jnp.mean(x, axis=-1, keepdims=True)
    xc = x - mu
    var = jnp.mean(xc * xc, axis=-1, keepdims=True)
    return xc * lax.rsqrt(var + NORM_EPS)


def _silu(x):
    return x * jax.nn.sigmoid(x)


def _group_mean(t, g):
    hi = t.astype(BF16)
    lo = (t - hi.astype(F32)).astype(BF16)
    return (jnp.dot(hi, g, preferred_element_type=F32) + jnp.dot(lo, g, preferred_element_type=F32))


def _dot_nt(a, b):
    return lax.dot_general(a, b, (((1,), (1,)), ((), ())), preferred_element_type=F32)


def _mod_kernel(c_ref, w_ref, b_ref, o_ref):
    h = _silu(c_ref[...])
    o_ref[...] = jnp.dot(h, w_ref[...], precision=HIGHEST, preferred_element_type=F32) + b_ref[...]


def _modulation(cc, w_mod, b_mod):
    depth, d, n = w_mod.shape
    tn = 2048
    return pl.pallas_call(
        _mod_kernel,
        out_shape=jax.ShapeDtypeStruct((depth, 8, n), F32),
        grid=(depth, n // tn),
        in_specs=[pl.BlockSpec((8, d), lambda l, j: (0, 0)),
                  pl.BlockSpec((None, d, tn), lambda l, j: (l, 0, j)),
                  pl.BlockSpec((None, 1, tn), lambda l, j: (l, 0, j))],
        out_specs=pl.BlockSpec((None, 8, tn), lambda l, j: (l, 0, j)),
        compiler_params=_cparams(("arbitrary", "arbitrary")),
        name="modulation",
    )(cc, w_mod, b_mod.reshape(depth, 1, n))


P_COLS = 14 * LANES


def _rope_lanes(t, c, sa, sb, half):
    outs = []
    for j in range(t.shape[1] // LANES):
        tj = t[:, j * LANES:(j + 1) * LANES]
        outs.append(tj * c + pltpu.roll(tj, LANES - half, 1) * sa + pltpu.roll(tj, half, 1) * sb)
    return outs[0] if len(outs) == 1 else jnp.concatenate(outs, axis=1)


def _in_kernel(*refs, rope):
    x_ref, sh_ref, sc_ref, w_ref, qn_ref, kn_ref, gavg_ref = refs[:7]
    if rope:
        c2_ref, sa2_ref, sb2_ref, c1_ref, sa1_ref, sb1_ref, p_ref, u_ref, g_ref = refs[7:]
    else:
        p_ref, u_ref, g_ref = refs[7:]
    h = _ln(x_ref[...]) * (1.0 + sc_ref[...]) + sh_ref[...]
    y = jnp.dot(h.astype(BF16), w_ref[...], preferred_element_type=F32)

    def rms(t, w, g):
        return t * lax.rsqrt(_group_mean(t * t, g) + NORM_EPS) * w

    def rope2(t):
        if not rope:
            return t
        return _rope_lanes(t, c2_ref[...], sa2_ref[...], sb2_ref[...], HEAD_DIM // 4)

    def rope1(t):
        if not rope:
            return t
        return _rope_lanes(t, c1_ref[...], sa1_ref[...], sb1_ref[...], HEAD_DIM // 2)

    scale = HEAD_DIM ** -0.5
    qa = rope2(rms(y[:, 0:256], qn_ref[...], gavg_ref[...])) * scale
    ka = rope2(rms(y[:, 256:384], kn_ref[...], gavg_ref[0:LANES, 0:LANES]))
    p_ref[:, 0:256] = qa.astype(BF16)
    p_ref[:, 256:384] = ka.astype(BF16)
    p_ref[:, 384:512] = y[:, 384:512].astype(BF16)
    p_ref[:, 512:768] = (rope2(y[:, 512:768]) * scale).astype(BF16)
    p_ref[:, 768:896] = rope2(y[:, 768:896]).astype(BF16)
    p_ref[:, 896:1024] = y[:, 896:1024].astype(BF16)
    p_ref[:, 1024:1280] = rope1(y[:, 1024:1280]).astype(BF16)
    p_ref[:, 1280:1536] = (rope1(y[:, 1280:1536]) * scale).astype(BF16)
    p_ref[:, 1536:1792] = y[:, 1536:1792].astype(BF16)
    u_ref[...] = y[:, 1792:2048]
    g_ref[...] = y[:, 2048:2304]


def _in_proj(x, mod, mod_row, w, qn, kn, gavg, tabs, tm):
    bsz, length, d = x.shape
    nw = w.shape[1]
    nt = length // tm
    rope = tabs is not None
    row = lambda b, i: (b, i, 0)
    const2 = lambda b, i: (0, 0)
    in_specs = [pl.BlockSpec((None, tm, d), row),
                pl.BlockSpec((None, None, 1, d), lambda b, i: (mod_row(b), 0, 0, 0)),
                pl.BlockSpec((None, None, 1, d), lambda b, i: (mod_row(b), 1, 0, 0)),
                pl.BlockSpec((d, nw), const2),
                pl.BlockSpec((1, 256), const2),
                pl.BlockSpec((1, LANES), const2),
                pl.BlockSpec((256, 256), const2)]
    args = [x, mod, mod, w, qn, kn, gavg]
    if rope:
        in_specs += [pl.BlockSpec((tm, LANES), lambda b, i: (i, 0))] * 6
        args += list(tabs)
    return pl.pallas_call(
        functools.partial(_in_kernel, rope=rope),
        out_shape=(jax.ShapeDtypeStruct((bsz, length, P_COLS), BF16),
                   jax.ShapeDtypeStruct((bsz, length, 256), F32),
                   jax.ShapeDtypeStruct((bsz, length, 256), F32)),
        grid=(bsz, nt),
        in_specs=in_specs,
        out_specs=(pl.BlockSpec((None, tm, P_COLS), row),
                   pl.BlockSpec((None, tm, 256), row),
                   pl.BlockSpec((None, tm, 256), row)),
        compiler_params=_cparams(("arbitrary", "arbitrary")),
        name="in_proj_rope" if rope else "in_proj_ctx",
    )(*args)


def _stack_heads(q):
    qf = q.astype(F32)
    lo = lax.broadcasted_iota(jnp.int32, (q.shape[0], LANES), 1) < HEAD_DIM
    q0, q1 = qf[:, 0:LANES], qf[:, LANES:2 * LANES]
    z = jnp.zeros_like(q0)
    return jnp.concatenate([jnp.where(lo, q0, z), jnp.where(lo, q1, z),
                            jnp.where(lo, z, q0), jnp.where(lo, z, q1)], axis=0).astype(BF16)


def _aug_values(v):
    vf = v.astype(F32)
    lo = lax.broadcasted_iota(jnp.int32, vf.shape, 1) < HEAD_DIM
    one = jnp.ones_like(vf)
    return jnp.where(lo, vf, one).astype(BF16), jnp.where(lo, one, vf).astype(BF16)


def _finish_heads(acc0, acc1, e, tq):
    l0 = pltpu.roll(acc0, HEAD_DIM, 1)
    l1 = pltpu.roll(acc1, HEAD_DIM, 1)
    if e is not None:
        l0 = l0 + e[:2 * tq]
        l1 = l1 + e[2 * tq:]
    n0 = acc0 / l0
    n1 = acc1 / l1
    lo = lax.broadcasted_iota(jnp.int32, (tq, LANES), 1) < HEAD_DIM
    return jnp.concatenate([jnp.where(lo, n0[:tq], n1[:tq]), jnp.where(lo, n0[tq:], n1[tq:])], axis=1)


def _sink_column(sink_ref, tq):
    return jnp.concatenate([jnp.full((tq, 1), sink_ref[h], F32) for h in range(4)], axis=0)


def _attn_kernel(*refs, tq, tk, n_lat, has_sink):
    i = 0
    sink_ref = None
    if has_sink:
        sink_ref = refs[0]
        i = 1
    q_ref = refs[i]
    i += 1
    if n_lat:
        kl_ref, vl_ref = refs[i:i + 2]
        i += 2
    kc_ref, vc_ref, o_ref = refs[i:i + 3]
    i += 3
    if n_lat:
        v0l_ref, v1l_ref = refs[i:i + 2]
        i += 2
    v0c_ref, v1c_ref = refs[i:i + 2]

    @pl.when(pl.program_id(1) == 0)
    def _():
        if n_lat:
            a0, a1 = _aug_values(vl_ref[...])
            v0l_ref[...] = a0
            v1l_ref[...] = a1
        a0, a1 = _aug_values(vc_ref[...])
        v0c_ref[...] = a0
        v1c_ref[...] = a1

    qs = _stack_heads(q_ref[...])
    half = 2 * tq

    def step(k, v0, v1, carry):
        m, acc0, acc1 = carry
        s = _dot_nt(qs, k)
        m_new = jnp.maximum(m, jnp.max(s, axis=1, keepdims=True))
        alpha = jnp.exp(m - m_new)
        p = jnp.exp(s - m_new).astype(BF16)
        acc0 = acc0 * alpha[:half] + jnp.dot(p[:half], v0, preferred_element_type=F32)
        acc1 = acc1 * alpha[half:] + jnp.dot(p[half:], v1, preferred_element_type=F32)
        return m_new, acc0, acc1

    m0 = _sink_column(sink_ref, tq) if has_sink else jnp.full((4 * tq, 1), NEG_INF, F32)
    carry = (m0, jnp.zeros((half, LANES), F32), jnp.zeros((half, LANES), F32))
    if n_lat:
        def body(c, carry):
            r = pl.multiple_of(c * tk, tk)
            return step(kl_ref[pl.ds(r, tk), :], v0l_ref[pl.ds(r, tk), :], v1l_ref[pl.ds(r, tk), :], carry)
        carry = lax.fori_loop(0, n_lat, body, carry)
    m, acc0, acc1 = step(kc_ref[...], v0c_ref[...], v1c_ref[...], carry)
    e = jnp.exp(_sink_column(sink_ref, tq) - m) if has_sink else None
    o_ref[...] = _finish_heads(acc0, acc1, e, tq).astype(BF16)


def _attention(pq, q_blk, p_lat, p_ctx, k_blk, v_blk, sink, tq, tk):
    bsz, lq, _ = pq.shape
    lc = p_ctx.shape[1]
    assert p_lat is None or p_lat.shape[1] % tk == 0
    n_lat = 0 if p_lat is None else p_lat.shape[1] // tk
    has_sink = sink is not None
    in_specs = [pl.BlockSpec((None, tq, 256), lambda b, i, *_: (b, i, q_blk))]
    args = [pq]
    scratch = []
    if n_lat:
        ll = p_lat.shape[1]
        in_specs += [pl.BlockSpec((None, ll, LANES), lambda b, i, *_: (b, 0, k_blk)),
                     pl.BlockSpec((None, ll, LANES), lambda b, i, *_: (b, 0, v_blk))]
        args += [p_lat, p_lat]
        scratch += [pltpu.VMEM((ll, LANES), BF16), pltpu.VMEM((ll, LANES), BF16)]
    in_specs += [pl.BlockSpec((None, lc, LANES), lambda b, i, *_: (b, 0, k_blk)),
                 pl.BlockSpec((None, lc, LANES), lambda b, i, *_: (b, 0, v_blk))]
    args += [p_ctx, p_ctx]
    scratch += [pltpu.VMEM((lc, LANES), BF16), pltpu.VMEM((lc, LANES), BF16)]
    kern = functools.partial(_attn_kernel, tq=tq, tk=tk, n_lat=n_lat, has_sink=has_sink)
    grid_spec = pltpu.PrefetchScalarGridSpec(
        num_scalar_prefetch=1 if has_sink else 0,
        grid=(bsz, lq // tq),
        in_specs=in_specs,
        out_specs=pl.BlockSpec((None, tq, 256), lambda b, i, *_: (b, i, 0)),
        scratch_shapes=scratch)
    call = pl.pallas_call(
        kern, out_shape=jax.ShapeDtypeStruct((bsz, lq, 256), BF16), grid_spec=grid_spec,
        compiler_params=_cparams(("arbitrary", "arbitrary")),
        name="attn_sink" if has_sink else ("attn_global" if n_lat else "attn_ctx"))
    return call(sink, *args) if has_sink else call(*args)


def _win_kernel(sink_ref, q_ref, kl_ref, vl_ref, kc_ref, vc_ref, o_ref, *, nb):
    tq = Q_BLOCK
    i = pl.program_id(1)
    prev = jnp.maximum(i - 1, 0)
    nxt = jnp.minimum(i + 1, nb - 1)

    def rows(ref, blk):
        return ref[pl.ds(pl.multiple_of(blk * tq, tq), tq), :]

    k = jnp.concatenate([rows(kl_ref, prev), rows(kl_ref, i), rows(kl_ref, nxt), kc_ref[...]], axis=0)
    v = jnp.concatenate([rows(vl_ref, prev), rows(vl_ref, i), rows(vl_ref, nxt), vc_ref[...]], axis=0)
    qs = _stack_heads(q_ref[...])
    s = _dot_nt(qs, k)
    nk = s.shape[1]
    r = lax.broadcasted_iota(jnp.int32, (4 * tq, nk), 0) & (tq - 1)
    j = lax.broadcasted_iota(jnp.int32, (4 * tq, nk), 1)
    has_prev = (i > 0).astype(jnp.int32)
    has_next = (i < nb - 1).astype(jnp.int32)
    ok_prev = jnp.where(j >= r, has_prev, 0)
    ok_next = jnp.where(j - 2 * tq <= r, has_next, 0)
    valid = jnp.where(j < tq, ok_prev, jnp.where(j < 2 * tq, 1, jnp.where(j < 3 * tq, ok_next, 1)))
    s = jnp.where(valid > 0, s, NEG_INF)
    snk = _sink_column(sink_ref, tq)
    m = jnp.maximum(jnp.max(s, axis=1, keepdims=True), snk)
    p = jnp.exp(s - m).astype(BF16)
    v0, v1 = _aug_values(v)
    acc0 = jnp.dot(p[:2 * tq], v0, preferred_element_type=F32)
    acc1 = jnp.dot(p[2 * tq:], v1, preferred_element_type=F32)
    o_ref[...] = _finish_heads(acc0, acc1, jnp.exp(snk - m), tq).astype(BF16)


def _window_attention(p_lat, p_ctx, sink):
    bsz, ll, _ = p_lat.shape
    lc = p_ctx.shape[1]
    nb = ll // Q_BLOCK
    grid_spec = pltpu.PrefetchScalarGridSpec(
        num_scalar_prefetch=1,
        grid=(bsz, nb),
        in_specs=[pl.BlockSpec((None, Q_BLOCK, 256), lambda b, i, *_: (b, i, 2)),
                  pl.BlockSpec((None, ll, LANES), lambda b, i, *_: (b, 0, 6)),
                  pl.BlockSpec((None, ll, LANES), lambda b, i, *_: (b, 0, 7)),
                  pl.BlockSpec((None, lc, LANES), lambda b, i, *_: (b, 0, 6)),
                  pl.BlockSpec((None, lc, LANES), lambda b, i, *_: (b, 0, 7))],
        out_specs=pl.BlockSpec((None, Q_BLOCK, 256), lambda b, i, *_: (b, i, 0)))
    return pl.pallas_call(
        functools.partial(_win_kernel, nb=nb),
        out_shape=jax.ShapeDtypeStruct((bsz, ll, 256), BF16), grid_spec=grid_spec,
        compiler_params=_cparams(("arbitrary", "arbitrary")),
        name="attn_window",
    )(sink, p_lat, p_lat, p_lat, p_ctx, p_ctx)


def _fft1_kernel(u_ref, w_ref, y_ref, *, l1):
    y = jnp.dot(w_ref[...], u_ref[...], precision=HIGHEST, preferred_element_type=F32)
    y_ref[0] = y[:l1]
    y_ref[1] = y[l1:]


def _fft2_kernel(*refs, has_imag):
    if has_imag:
        yr_ref, yi_ref, c_ref, s_ref, bdc_ref, bds_ref, fm_ref, o_ref = refs
    else:
        yr_ref, c_ref, s_ref, bdc_ref, bds_ref, fm_ref, o_ref = refs
    dot = functools.partial(jnp.dot, precision=HIGHEST, preferred_element_type=F32)
    c, s, yr = c_ref[...], s_ref[...], yr_ref[...]
    a = dot(c, yr)
    bm = -dot(s, yr)
    if has_imag:
        yi = yi_ref[...]
        a = a + dot(s, yi)
        bm = bm + dot(c, yi)
    z = dot(a, bdc_ref[...]) + dot(bm, bds_ref[...])
    o_ref[...] = jnp.dot(z.astype(BF16), fm_ref[...], preferred_element_type=F32).astype(BF16)


def _dft_tables(n_rows, n_cols, length, row_stride=1, row_offset=0):
    k = row_offset + row_stride * np.arange(n_rows, dtype=np.int64)
    n = np.arange(n_cols, dtype=np.int64)
    ang = 2.0 * np.pi * ((k[:, None] * n[None, :]) % length).astype(np.float64) / length
    return np.cos(ang).astype(np.float32), np.sin(ang).astype(np.float32)


def _channel_dft_blockdiag(width, length):
    c, s = _dft_tables(HEAD_DIM, HEAD_DIM, HEAD_DIM)
    eye = np.eye(width // HEAD_DIM, dtype=np.float64) / np.sqrt(float(length) * HEAD_DIM)
    return np.kron(eye, c).astype(np.float32), np.kron(eye, s).astype(np.float32)


def _fourier_latent(u, fm_bd):
    bsz, length, w = u.shape
    l2 = FFT_L2
    l1 = length // l2
    c1, s1 = _dft_tables(l1, l1, l1)
    w1 = jnp.asarray(np.concatenate([c1, -s1], axis=0))
    tn = 8 * w
    y = pl.pallas_call(
        functools.partial(_fft1_kernel, l1=l1),
        out_shape=jax.ShapeDtypeStruct((bsz, 2, l1, l2 * w), F32),
        grid=(bsz, l2 * w // tn),
        in_specs=[pl.BlockSpec((None, l1, tn), lambda b, j: (b, 0, j)),
                  pl.BlockSpec((2 * l1, l1), lambda b, j: (0, 0))],
        out_specs=pl.BlockSpec((None, 2, l1, tn), lambda b, j: (b, 0, 0, j)),
        compiler_params=_cparams(("arbitrary", "arbitrary")),
        name="fourier_stage1",
    )(u.reshape(bsz, l1, l2 * w), w1)
    y = y.reshape(bsz, 2, l1, l2, w)
    ck = np.stack([_dft_tables(l2, l2, length, row_stride=l1, row_offset=k1)[0] for k1 in range(l1)])
    sk = np.stack([_dft_tables(l2, l2, length, row_stride=l1, row_offset=k1)[1] for k1 in range(l1)])
    bdc, bds = _channel_dft_blockdiag(w, length)
    const2 = lambda b, k: (0, 0)
    out = pl.pallas_call(
        functools.partial(_fft2_kernel, has_imag=True),
        out_shape=jax.ShapeDtypeStruct((bsz, l2, l1 * w), BF16),
        grid=(bsz, l1),
        in_specs=[pl.BlockSpec((None, None, None, l2, w), lambda b, k: (b, 0, k, 0, 0)),
                  pl.BlockSpec((None, None, None, l2, w), lambda b, k: (b, 1, k, 0, 0)),
                  pl.BlockSpec((None, l2, l2), lambda b, k: (k, 0, 0)),
                  pl.BlockSpec((None, l2, l2), lambda b, k: (k, 0, 0)),
                  pl.BlockSpec((w, w), const2), pl.BlockSpec((w, w), const2), pl.BlockSpec((w, w), const2)],
        out_specs=pl.BlockSpec((None, l2, w), lambda b, k: (b, 0, k)),
        compiler_params=_cparams(("arbitrary", "arbitrary")),
        name="fourier_stage2",
    )(y, y, jnp.asarray(ck), jnp.asarray(sk), jnp.asarray(bdc), jnp.asarray(bds), fm_bd)
    return out.reshape(bsz, length, w)


def _fourier_direct(u, fm_bd):
    bsz, length, w = u.shape
    c, s = _dft_tables(length, length, length)
    bdc, bds = _channel_dft_blockdiag(w, length)
    const2 = lambda b: (0, 0)
    return pl.pallas_call(
        functools.partial(_fft2_kernel, has_imag=False),
        out_shape=jax.ShapeDtypeStruct((bsz, length, w), BF16),
        grid=(bsz,),
        in_specs=[pl.BlockSpec((None, length, w), lambda b: (b, 0, 0)),
                  pl.BlockSpec((length, length), const2), pl.BlockSpec((length, length), const2),
                  pl.BlockSpec((w, w), const2), pl.BlockSpec((w, w), const2), pl.BlockSpec((w, w), const2)],
        out_specs=pl.BlockSpec((None, length, w), lambda b: (b, 0, 0)),
        compiler_params=_cparams(("arbitrary",)),
        name="fourier_ctx",
    )(u, jnp.asarray(c), jnp.asarray(s), jnp.asarray(bdc), jnp.asarray(bds), fm_bd)


def _log_sigmoid(x):
    return jnp.minimum(x, 0.0) - jnp.log1p(jnp.exp(-jnp.abs(x)))


def _ret_kernel(*refs, need_ctx):
    (rdl_ref, rdh_ref, qf_ref, kf_ref, vf_ref, qb_ref, kb_ref, vb_ref, qc_ref, kc_ref, vc_ref) = refs[:11]
    if need_ctx:
        of_ref, ob_ref, oc_ref = refs[11:14]
        scr = refs[14:]
    else:
        of_ref, ob_ref = refs[11:13]
        oc_ref = None
        scr = refs[13:]
    sf_ref, sb_ref, din_ref, tab_ref = scr
    c = RET_CHUNK
    w = 4 * HEAD_DIM
    j = pl.program_id(1)
    head_shift = HEAD_DIM.bit_length() - 1
    lane_head = lax.broadcasted_iota(jnp.int32, (c, w), 1) >> head_shift
    blockdiag = ((lax.broadcasted_iota(jnp.int32, (w, w), 0) >> head_shift)
                 == (lax.broadcasted_iota(jnp.int32, (w, w), 1) >> head_shift))

    def chunk(q, k, v, d):
        qf = q.astype(F32)
        o = jnp.zeros((c, w), F32)
        for h in range(4):
            hm = lane_head == h
            qh = jnp.where(hm, qf, 0.0).astype(BF16)
            inner = (_dot_nt(qh, k) * din_ref[d, h]).astype(BF16)
            o = o + jnp.where(hm, jnp.dot(inner, v, preferred_element_type=F32), 0.0)
        kz = (k.astype(F32) * tab_ref[d, 1]).T.astype(BF16)
        kv = jnp.where(blockdiag, jnp.dot(kz, v, preferred_element_type=F32), 0.0)
        return o, kv

    @pl.when(j == 0)
    def _():
        t = lax.broadcasted_iota(jnp.int32, (c, w), 0).astype(F32)
        rr = lax.broadcasted_iota(jnp.int32, (c, c), 0)
        cc = lax.broadcasted_iota(jnp.int32, (c, c), 1)
        for d in range(2):
            lg = _log_sigmoid(rdl_ref[d])
            tab_ref[d, 0] = jnp.exp(lg * ((t + 1.0) if d == 0 else (c - t)))
            tab_ref[d, 1] = jnp.exp(lg * ((c - 1.0 - t) if d == 0 else t))
            tab_ref[d, 2] = jnp.exp(jnp.broadcast_to(lg, (c, w)) * float(c))
            diff = (rr - cc) if d == 0 else (cc - rr)
            dpos = jnp.maximum(diff, 0).astype(F32)
            for h in range(4):
                lgh = _log_sigmoid(rdh_ref[d, h])
                din_ref[d, h] = jnp.where(diff >= 0, jnp.exp(lgh * dpos), 0.0)
        q, k, v = qc_ref[...], kc_ref[...], vc_ref[...]
        o_f, kv_f = chunk(q, k, v, 0)
        o_b, kv_b = chunk(q, k, v, 1)
        sf_ref[...] = kv_f
        sb_ref[...] = kv_b
        if need_ctx:
            oc_ref[...] = o_f + o_b

    @pl.when(j > 0)
    def _():
        for d, (q_ref, k_ref, v_ref, s_ref, o_ref) in enumerate(
                ((qf_ref, kf_ref, vf_ref, sf_ref, of_ref), (qb_ref, kb_ref, vb_ref, sb_ref, ob_ref))):
            q, k, v = q_ref[...], k_ref[...], v_ref[...]
            o, kv = chunk(q, k, v, d)
            state = s_ref[...]
            o_ref[...] = o + jnp.dot(q, state.astype(BF16), preferred_element_type=F32) * tab_ref[d, 0]
            s_ref[...] = state * tab_ref[d, 2, 0:1, :] + kv


def _retention(p_lat, p_ctx, rdl, rdh, need_ctx):
    bsz, ll, _ = p_lat.shape
    lc = p_ctx.shape[1]
    c = RET_CHUNK
    w = 4 * HEAD_DIM
    assert lc == c and ll % c == 0
    n = ll // c
    fwd = lambda blk: (lambda b, j: (b, jnp.maximum(j - 1, 0), blk))
    bwd = lambda blk: (lambda b, j: (b, n - 1 - jnp.maximum(j - 1, 0), blk))
    ctx = lambda blk: (lambda b, j: (b, 0, blk))
    in_specs = [pl.BlockSpec((2, 1, w), lambda b, j: (0, 0, 0)),
                pl.BlockSpec((2, 4, 1, c), lambda b, j: (0, 0, 0, 0))]
    in_specs += [pl.BlockSpec((None, c, w), fwd(blk)) for blk in (4, 5, 6)]
    in_specs += [pl.BlockSpec((None, c, w), bwd(blk)) for blk in (4, 5, 6)]
    in_specs += [pl.BlockSpec((None, c, w), ctx(blk)) for blk in (4, 5, 6)]
    out_shape = [jax.ShapeDtypeStruct((bsz, ll, w), F32), jax.ShapeDtypeStruct((bsz, ll, w), F32)]
    out_specs = [pl.BlockSpec((None, c, w), fwd(0)), pl.BlockSpec((None, c, w), bwd(0))]
    if need_ctx:
        out_shape.append(jax.ShapeDtypeStruct((bsz, lc, w), F32))
        out_specs.append(pl.BlockSpec((None, c, w), ctx(0)))
    return pl.pallas_call(
        functools.partial(_ret_kernel, need_ctx=need_ctx),
        out_shape=tuple(out_shape),
        grid=(bsz, n + 1),
        in_specs=in_specs,
        out_specs=tuple(out_specs),
        scratch_shapes=[pltpu.VMEM((w, w), F32), pltpu.VMEM((w, w), F32),
                        pltpu.VMEM((2, 4, c, c), F32), pltpu.VMEM((2, 3, c, w), F32)],
        compiler_params=_cparams(("arbitrary", "arbitrary")),
        name="retention_ctx_out" if need_ctx else "retention",
    )(rdl, rdh, *([p_lat] * 6), *([p_ctx] * 3))


def _out_kernel(*refs, n_o, alpha):
    a_ref, b_ref, f_ref, g_ref = refs[:4]
    o_refs = refs[4:4 + n_o]
    x_ref, g1_ref, lnw_ref, lnb_ref, w_ref, gnw_ref, gavg_ref, out_ref = refs[4 + n_o:]
    o = o_refs[0][...]
    for r in o_refs[1:]:
        o = o + r[...]
    gavg = gavg_ref[...]
    dlt = o - _group_mean(o, gavg)
    on = dlt * lax.rsqrt(_group_mean(dlt * dlt, gavg) + NORM_EPS) * gnw_ref[...]
    r = (_silu(g_ref[...]) * on).astype(BF16)
    cat = jnp.concatenate([a_ref[...], b_ref[...], f_ref[...], r], axis=1)
    y = jnp.dot(cat, w_ref[...], preferred_element_type=F32)
    z = alpha * x_ref[...] + g1_ref[...] * y
    out_ref[...] = _ln(z) * lnw_ref[...] + lnb_ref[...]


def _out_proj(a, b, f, g, o_parts, x, mod, mod_row, lnw, lnb, w, gnw, gavg, alpha, tm):
    bsz, length, d = x.shape
    row = lambda bb, i: (bb, i, 0)
    const2 = lambda bb, i: (0, 0)
    blk256 = pl.BlockSpec((None, tm, 256), row)
    in_specs = [blk256] * (4 + len(o_parts)) + [
        pl.BlockSpec((None, tm, d), row),
        pl.BlockSpec((None, None, 1, d), lambda bb, i: (mod_row(bb), 2, 0, 0)),
        pl.BlockSpec((1, d), const2), pl.BlockSpec((1, d), const2),
        pl.BlockSpec((d, d), const2),
        pl.BlockSpec((1, 256), const2), pl.BlockSpec((256, 256), const2)]
    return pl.pallas_call(
        functools.partial(_out_kernel, n_o=len(o_parts), alpha=alpha),
        out_shape=jax.ShapeDtypeStruct((bsz, length, d), F32),
        grid=(bsz, length // tm),
        in_specs=in_specs,
        out_specs=pl.BlockSpec((None, tm, d), row),
        compiler_params=_cparams(("arbitrary", "arbitrary")),
        name="out_proj",
    )(a, b, f, g, *o_parts, x, mod, lnw, lnb, w, gnw, gavg)


def _ffn_kernel(x_ref, sh_ref, sc_ref, g2_ref, lnw_ref, lnb_ref, wg_ref, wu_ref, wd_ref, out_ref, *, fc, alpha):
    x = x_ref[...]
    h = (_ln(x) * (1.0 + sc_ref[...]) + sh_ref[...]).astype(BF16)
    acc = jnp.zeros(x.shape, F32)
    for c in range(wg_ref.shape[1] // fc):
        cols = slice(c * fc, (c + 1) * fc)
        gate = jnp.dot(h, wg_ref[:, cols], preferred_element_type=F32)
        up = jnp.dot(h, wu_ref[:, cols], preferred_element_type=F32)
        act = (_silu(gate) * up).astype(BF16)
        acc = acc + jnp.dot(act, wd_ref[cols, :], preferred_element_type=F32)
    z = alpha * x + g2_ref[...] * acc
    out_ref[...] = _ln(z) * lnw_ref[...] + lnb_ref[...]


def _ffn(x, mod, mod_row, lnw, lnb, wg, wu, wd, alpha, tm):
    bsz, length, d = x.shape
    ff = wg.shape[1]
    fc = ff // 2 if (ff // 2) % LANES == 0 else ff
    row = lambda bb, i: (bb, i, 0)
    const2 = lambda bb, i: (0, 0)
    modspec = lambda which: pl.BlockSpec((None, None, 1, d), lambda bb, i: (mod_row(bb), which, 0, 0))
    resident = functools.partial(pl.BlockSpec, index_map=const2, pipeline_mode=pl.Buffered(1))
    return pl.pallas_call(
        functools.partial(_ffn_kernel, fc=fc, alpha=alpha),
        out_shape=jax.ShapeDtypeStruct((bsz, length, d), F32),
        grid=(bsz, length // tm),
        in_specs=[pl.BlockSpec((None, tm, d), row), modspec(3), modspec(4), modspec(5),
                  pl.BlockSpec((1, d), const2), pl.BlockSpec((1, d), const2),
                  resident((d, ff)), resident((d, ff)), resident((ff, d))],
        out_specs=pl.BlockSpec((None, tm, d), row),
        compiler_params=_cparams(("arbitrary", "arbitrary")),
        name="ffn",
    )(x, mod, mod, mod, lnw, lnb, wg, wu, wd)


def _head_perm(off):
    return [(off + h * HEAD_DIM, off + (h + 1) * HEAD_DIM) for h in (0, 2, 1, 3)]


def _take_ranges(w, ranges, axis):
    return jnp.concatenate([lax.slice_in_dim(w, lo, hi, axis=axis) for lo, hi in ranges], axis=axis)


def _rope_tables(seq):
    t = jnp.arange(seq, dtype=jnp.int32)
    row = (t // GRID_W).astype(F32)
    col = (t % GRID_W).astype(F32)

    def tab(pos, n_freq):
        inv = ROPE_THETA ** (-jnp.arange(n_freq, dtype=F32) / n_freq)
        ang = pos[:, None] * inv[None, :]
        return jnp.cos(ang), jnp.sin(ang)

    nf = HEAD_DIM // 4
    cr, sr = tab(row, nf)
    cc, sc = tab(col, nf)
    z = jnp.zeros_like(sr)
    c2 = jnp.tile(jnp.concatenate([cr, cr, cc, cc], -1), (1, 2))
    sa2 = jnp.tile(jnp.concatenate([-sr, z, -sc, z], -1), (1, 2))
    sb2 = jnp.tile(jnp.concatenate([z, sr, z, sc], -1), (1, 2))
    ct, st = tab(t.astype(F32), HEAD_DIM // 2)
    z = jnp.zeros_like(st)
    c1 = jnp.tile(jnp.concatenate([ct, ct], -1), (1, 2))
    sa1 = jnp.tile(jnp.concatenate([-st, z], -1), (1, 2))
    sb1 = jnp.tile(jnp.concatenate([z, st], -1), (1, 2))
    return c2, sa2, sb2, c1, sa1, sb1


def kernel(x, c, ctx, c_ctx, w_mod, b_mod, w_in, a_q_norm, a_k_norm, b_sink, f_mix, r_decay, r_gn_w, w_out,
           ln1_w, ln1_b, w_gate_up, w_down, ln2_w, ln2_b):
    bsz, seq, d = x.shape
    depth = w_in.shape[0]
    ff = w_down.shape[1]
    gw = d // 4
    assert gw == 4 * HEAD_DIM and a_q_norm.shape[-1] == HEAD_DIM and seq % (FFT_L2 * 8) == 0
    alpha = (2.0 * depth) ** 0.25

    tabs = _rope_tables(seq)
    gavg = jnp.asarray(np.kron(np.eye(gw // HEAD_DIM), np.full((HEAD_DIM, HEAD_DIM), 1.0 / HEAD_DIM)), BF16)
    cc = jnp.zeros((8, d), F32).at[:bsz].set(c).at[bsz].set(c_ctx)
    mod_all = _modulation(cc, w_mod, b_mod).reshape(depth, 8, 6, 1, d)
    lat_row = lambda b: b
    ctx_row = lambda b: bsz

    perm_in = (_head_perm(0) + [(256, 512)] + _head_perm(512) + [(768, 1024), (1280, 2048), (1024, 1280),
                                                                  (2048, 2304)])
    perm_out = _head_perm(0) + _head_perm(gw) + [(2 * gw, 4 * gw)]
    eye_g = jnp.eye(gw // HEAD_DIM, dtype=F32)

    tm = 512
    for layer in range(depth):
        need_ctx = layer < depth - 1
        mod = mod_all[layer]
        w_in_l = _take_ranges(w_in[layer], perm_in, 1).astype(BF16)
        w_out_l = _take_ranges(w_out[layer], perm_out, 0).astype(BF16)
        wg = w_gate_up[layer][:, :ff].astype(BF16)
        wu = w_gate_up[layer][:, ff:].astype(BF16)
        wd = w_down[layer].astype(BF16)
        qn = jnp.tile(a_q_norm[layer], 4)[None, :]
        kn = jnp.tile(a_k_norm[layer], 2)[None, :]
        gnw = r_gn_w[layer][None, :]
        lnw1, lnb1 = ln1_w[layer][None, :], ln1_b[layer][None, :]
        lnw2, lnb2 = ln2_w[layer][None, :], ln2_b[layer][None, :]
        fm_bd = jnp.einsum('gh,gce->gche', eye_g, f_mix[layer]).reshape(gw, gw).astype(BF16)
        rd = r_decay[layer]
        rdl = jnp.repeat(rd, HEAD_DIM, axis=1)[:, None, :]
        rdh = jnp.broadcast_to(rd[:, :, None, None], (2, 4, 1, RET_CHUNK))
        sink = b_sink[layer]

        p_l, u_l, g_l = _in_proj(x, mod, lat_row, w_in_l, qn, kn, gavg, tabs, tm)
        p_c, u_c, g_c = _in_proj(ctx, mod, ctx_row, w_in_l, qn, kn, gavg, None, ctx.shape[1])

        a_l = _attention(p_l, 0, p_l, p_c, 2, 3, None, tq=Q_BLOCK, tk=min(2048, seq))
        b_l = _window_attention(p_l, p_c, sink)
        f_l = _fourier_latent(u_l, fm_bd)
        r_out = _retention(p_l, p_c, rdl, rdh, need_ctx)
        x = _out_proj(a_l, b_l, f_l, g_l, r_out[:2], x, mod, lat_row, lnw1, lnb1, w_out_l, gnw, gavg, alpha, tm)
        x = _ffn(x, mod, lat_row, lnw2, lnb2, wg, wu, wd, alpha, tm)
        if need_ctx:
            lc = ctx.shape[1]
            a_c = _attention(p_c, 0, None, p_c, 2, 3, None, tq=Q_BLOCK, tk=2048)
            b_c = _attention(p_c, 2, None, p_c, 6, 7, sink, tq=Q_BLOCK, tk=2048)
            f_c = _fourier_direct(u_c, fm_bd)
            ctx = _out_proj(a_c, b_c, f_c, g_c, r_out[2:], ctx, mod, ctx_row, lnw1, lnb1, w_out_l, gnw, gavg,
                            alpha, lc)
            ctx = _ffn(ctx, mod, ctx_row, lnw2, lnb2, wg, wu, wd, alpha, lc)
    return x
```

```python
import functools

import numpy as np
import jax
import jax.numpy as jnp
from jax import lax
from jax.experimental import pallas as pl
from jax.experimental.pallas import tpu as pltpu

F32 = jnp.float32
BF16 = jnp.bfloat16
HIGHEST = lax.Precision.HIGHEST

HEAD_DIM = 64
GRID_W = 64
Q_BLOCK = 128
ROPE_THETA = 10000.0
NORM_EPS = 1e-6
NEG_INF = -1e30

LANES = 128
VMEM_LIMIT_BYTES = 56 * 1024 * 1024

RET_CHUNK = 256
FFT_L2 = 128


def _cparams(sem):
    return pltpu.CompilerParams(dimension_semantics=sem, vmem_limit_bytes=VMEM_LIMIT_BYTES)


def _ln(x):
    mu = jnp.mean(x, axis=-1, keepdims=True)
    xc = x - mu
    var = jnp.mean(xc * xc, axis=-1, keepdims=True)
    return xc * lax.rsqrt(var + NORM_EPS)


def _silu(x):
    return x * jax.nn.sigmoid(x)


def _group_mean(t, g):
    hi = t.astype(BF16)
    lo = (t - hi.astype(F32)).astype(BF16)
    return (jnp.dot(hi, g, preferred_element_type=F32) + jnp.dot(lo, g, preferred_element_type=F32))


def _dot_nt(a, b):
    return lax.dot_general(a, b, (((1,), (1,)), ((), ())), preferred_element_type=F32)


def _mod_kernel(c_ref, w_ref, b_ref, o_ref):
    h = _silu(c_ref[...])
    o_ref[...] = jnp.dot(h, w_ref[...], precision=HIGHEST, preferred_element_type=F32) + b_ref[...]


def _modulation(cc, w_mod, b_mod):
    depth, d, n = w_mod.shape
    tn = 2048
    return pl.pallas_call(
        _mod_kernel,
        out_shape=jax.ShapeDtypeStruct((depth, 8, n), F32),
        grid=(depth, n // tn),
        in_specs=[pl.BlockSpec((8, d), lambda l, j: (0, 0)),
                  pl.BlockSpec((None, d, tn), lambda l, j: (l, 0, j)),
                  pl.BlockSpec((None, 1, tn), lambda l, j: (l, 0, j))],
        out_specs=pl.BlockSpec((None, 8, tn), lambda l, j: (l, 0, j)),
        compiler_params=_cparams(("arbitrary", "arbitrary")),
        name="modulation",
    )(cc, w_mod, b_mod.reshape(depth, 1, n))


P_COLS = 14 * LANES


def _rope_lanes(t, c, sa, sb, half):
    outs = []
    for j in range(t.shape[1] // LANES):
        tj = t[:, j * LANES:(j + 1) * LANES]
        outs.append(tj * c + pltpu.roll(tj, LANES - half, 1) * sa + pltpu.roll(tj, half, 1) * sb)
    return outs[0] if len(outs) == 1 else jnp.concatenate(outs, axis=1)


def _in_kernel(*refs, rope):
    x_ref, sh_ref, sc_ref, w_ref, qn_ref, kn_ref, gavg_ref = refs[:7]
    if rope:
        c2_ref, sa2_ref, sb2_ref, c1_ref, sa1_ref, sb1_ref, p_ref, u_ref, g_ref = refs[7:]
    else:
        p_ref, u_ref, g_ref = refs[7:]
    h = _ln(x_ref[...]) * (1.0 + sc_ref[...]) + sh_ref[...]
    y = jnp.dot(h.astype(BF16), w_ref[...], preferred_element_type=F32)

    def rms(t, w, g):
        return t * lax.rsqrt(_group_mean(t * t, g) + NORM_EPS) * w

    def rope2(t):
        if not rope:
            return t
        return _rope_lanes(t, c2_ref[...], sa2_ref[...], sb2_ref[...], HEAD_DIM // 4)

    def rope1(t):
        if not rope:
            return t
        return _rope_lanes(t, c1_ref[...], sa1_ref[...], sb1_ref[...], HEAD_DIM // 2)

    scale = HEAD_DIM ** -0.5
    qa = rope2(rms(y[:, 0:256], qn_ref[...], gavg_ref[...])) * scale
    ka = rope2(rms(y[:, 256:384], kn_ref[...], gavg_ref[0:LANES, 0:LANES]))
    p_ref[:, 0:256] = qa.astype(BF16)
    p_ref[:, 256:384] = ka.astype(BF16)
    p_ref[:, 384:512] = y[:, 384:512].astype(BF16)
    p_ref[:, 512:768] = (rope2(y[:, 512:768]) * scale).astype(BF16)
    p_ref[:, 768:896] = rope2(y[:, 768:896]).astype(BF16)
    p_ref[:, 896:1024] = y[:, 896:1024].astype(BF16)
    p_ref[:, 1024:1280] = rope1(y[:, 1024:1280]).astype(BF16)
    p_ref[:, 1280:1536] = (rope1(y[:, 1280:1536]) * scale).astype(BF16)
    p_ref[:, 1536:1792] = y[:, 1536:1792].astype(BF16)
    u_ref[...] = y[:, 1792:2048]
    g_ref[...] = y[:, 2048:2304]


def _in_proj(x, mod, mod_row, w, qn, kn, gavg, tabs, tm):
    bsz, length, d = x.shape
    nw = w.shape[1]
    nt = length // tm
    rope = tabs is not None
    row = lambda b, i: (b, i, 0)
    const2 = lambda b, i: (0, 0)
    in_specs = [pl.BlockSpec((None, tm, d), row),
                pl.BlockSpec((None, None, 1, d), lambda b, i: (mod_row(b), 0, 0, 0)),
                pl.BlockSpec((None, None, 1, d), lambda b, i: (mod_row(b), 1, 0, 0)),
                pl.BlockSpec((d, nw), const2),
                pl.BlockSpec((1, 256), const2),
                pl.BlockSpec((1, LANES), const2),
                pl.BlockSpec((256, 256), const2)]
    args = [x, mod, mod, w, qn, kn, gavg]
    if rope:
        in_specs += [pl.BlockSpec((tm, LANES), lambda b, i: (i, 0))] * 6
        args += list(tabs)
    return pl.pallas_call(
        functools.partial(_in_kernel, rope=rope),
        out_shape=(jax.ShapeDtypeStruct((bsz, length, P_COLS), BF16),
                   jax.ShapeDtypeStruct((bsz, length, 256), F32),
                   jax.ShapeDtypeStruct((bsz, length, 256), F32)),
        grid=(bsz, nt),
        in_specs=in_specs,
        out_specs=(pl.BlockSpec((None, tm, P_COLS), row),
                   pl.BlockSpec((None, tm, 256), row),
                   pl.BlockSpec((None, tm, 256), row)),
        compiler_params=_cparams(("arbitrary", "arbitrary")),
        name="in_proj_rope" if rope else "in_proj_ctx",
    )(*args)


def _stack_heads(q):
    qf = q.astype(F32)
    lo = lax.broadcasted_iota(jnp.int32, (q.shape[0], LANES), 1) < HEAD_DIM
    q0, q1 = qf[:, 0:LANES], qf[:, LANES:2 * LANES]
    z = jnp.zeros_like(q0)
    return jnp.concatenate([jnp.where(lo, q0, z), jnp.where(lo, q1, z),
                            jnp.where(lo, z, q0), jnp.where(lo, z, q1)], axis=0).astype(BF16)


def _aug_values(v):
    vf = v.astype(F32)
    lo = lax.broadcasted_iota(jnp.int32, vf.shape, 1) < HEAD_DIM
    one = jnp.ones_like(vf)
    return jnp.where(lo, vf, one).astype(BF16), jnp.where(lo, one, vf).astype(BF16)


def _finish_heads(acc0, acc1, e, tq):
    l0 = pltpu.roll(acc0, HEAD_DIM, 1)
    l1 = pltpu.roll(acc1, HEAD_DIM, 1)
    if e is not None:
        l0 = l0 + e[:2 * tq]
        l1 = l1 + e[2 * tq:]
    n0 = acc0 / l0
    n1 = acc1 / l1
    lo = lax.broadcasted_iota(jnp.int32, (tq, LANES), 1) < HEAD_DIM
    return jnp.concatenate([jnp.where(lo, n0[:tq], n1[:tq]), jnp.where(lo, n0[tq:], n1[tq:])], axis=1)


def _sink_column(sink_ref, tq):
    return jnp.concatenate([jnp.full((tq, 1), sink_ref[h], F32) for h in range(4)], axis=0)


def _attn_kernel(*refs, tq, tk, n_lat, has_sink):
    i = 0
    sink_ref = None
    if has_sink:
        sink_ref = refs[0]
        i = 1
    q_ref = refs[i]
    i += 1
    if n_lat:
        kl_ref, vl_ref = refs[i:i + 2]
        i += 2
    kc_ref, vc_ref, o_ref = refs[i:i + 3]
    i += 3
    if n_lat:
        v0l_ref, v1l_ref = refs[i:i + 2]
        i += 2
    v0c_ref, v1c_ref = refs[i:i + 2]

    @pl.when(pl.program_id(1) == 0)
    def _():
        if n_lat:
            a0, a1 = _aug_values(vl_ref[...])
            v0l_ref[...] = a0
            v1l_ref[...] = a1
        a0, a1 = _aug_values(vc_ref[...])
        v0c_ref[...] = a0
        v1c_ref[...] = a1

    qs = _stack_heads(q_ref[...])
    half = 2 * tq

    chunks = [(kl_ref, v0l_ref, v1l_ref, slice(c * tk, (c + 1) * tk)) for c in range(n_lat)]
    chunks.append((kc_ref, v0c_ref, v1c_ref, slice(None)))

    def scores(chunk):
        k_ref, _, _, rows = chunk
        return _dot_nt(qs, k_ref[rows, :])

    m = _sink_column(sink_ref, tq) if has_sink else jnp.full((4 * tq, 1), NEG_INF, F32)
    acc0 = jnp.zeros((half, LANES), F32)
    acc1 = jnp.zeros((half, LANES), F32)
    s_next = scores(chunks[0])
    for idx, (_, v0_ref, v1_ref, rows) in enumerate(chunks):
        s = s_next
        if idx + 1 < len(chunks):
            s_next = scores(chunks[idx + 1])
        m_new = jnp.maximum(m, jnp.max(s, axis=1, keepdims=True))
        alpha = jnp.exp(m - m_new)
        p = jnp.exp(s - m_new).astype(BF16)
        acc0 = acc0 * alpha[:half] + jnp.dot(p[:half], v0_ref[rows, :], preferred_element_type=F32)
        acc1 = acc1 * alpha[half:] + jnp.dot(p[half:], v1_ref[rows, :], preferred_element_type=F32)
        m = m_new
    e = jnp.exp(_sink_column(sink_ref, tq) - m) if has_sink else None
    o_ref[...] = _finish_heads(acc0, acc1, e, tq).astype(BF16)


def _attention(pq, q_blk, p_lat, p_ctx, k_blk, v_blk, sink, tq, tk):
    bsz, lq, _ = pq.shape
    lc = p_ctx.shape[1]
    assert p_lat is None or p_lat.shape[1] % tk == 0
    n_lat = 0 if p_lat is None else p_lat.shape[1] // tk
    has_sink = sink is not None
    in_specs = [pl.BlockSpec((None, tq, 256), lambda b, i, *_: (b, i, q_blk))]
    args = [pq]
    scratch = []
    if n_lat:
        ll = p_lat.shape[1]
        in_specs += [pl.BlockSpec((None, ll, LANES), lambda b, i, *_: (b, 0, k_blk)),
                     pl.BlockSpec((None, ll, LANES), lambda b, i, *_: (b, 0, v_blk))]
        args += [p_lat, p_lat]
        scratch += [pltpu.VMEM((ll, LANES), BF16), pltpu.VMEM((ll, LANES), BF16)]
    in_specs += [pl.BlockSpec((None, lc, LANES), lambda b, i, *_: (b, 0, k_blk)),
                 pl.BlockSpec((None, lc, LANES), lambda b, i, *_: (b, 0, v_blk))]
    args += [p_ctx, p_ctx]
    scratch += [pltpu.VMEM((lc, LANES), BF16), pltpu.VMEM((lc, LANES), BF16)]
    kern = functools.partial(_attn_kernel, tq=tq, tk=tk, n_lat=n_lat, has_sink=has_sink)
    grid_spec = pltpu.PrefetchScalarGridSpec(
        num_scalar_prefetch=1 if has_sink else 0,
        grid=(bsz, lq // tq),
        in_specs=in_specs,
        out_specs=pl.BlockSpec((None, tq, 256), lambda b, i, *_: (b, i, 0)),
        scratch_shapes=scratch)
    call = pl.pallas_call(
        kern, out_shape=jax.ShapeDtypeStruct((bsz, lq, 256), BF16), grid_spec=grid_spec,
        compiler_params=_cparams(("arbitrary", "arbitrary")),
        name="attn_sink" if has_sink else ("attn_global" if n_lat else "attn_ctx"))
    return call(sink, *args) if has_sink else call(*args)


WIN_BLOCKS_PER_STEP = 4


def _win_kernel(sink_ref, q_ref, kl_ref, vl_ref, kc_ref, vc_ref, o_ref, *, nb):
    tq = Q_BLOCK
    r = lax.broadcasted_iota(jnp.int32, (4 * tq, tq), 0) & (tq - 1)
    j = lax.broadcasted_iota(jnp.int32, (4 * tq, tq), 1)
    in_prev = j >= r
    in_next = j <= r
    snk = _sink_column(sink_ref, tq)
    kc, vc = kc_ref[...], vc_ref[...]

    def rows(ref, blk):
        return ref[pl.ds(pl.multiple_of(blk * tq, tq), tq), :]

    for t in range(WIN_BLOCKS_PER_STEP):
        i = pl.program_id(1) * WIN_BLOCKS_PER_STEP + t
        prev = jnp.maximum(i - 1, 0)
        nxt = jnp.minimum(i + 1, nb - 1)
        k = jnp.concatenate([rows(kl_ref, prev), rows(kl_ref, i), rows(kl_ref, nxt), kc], axis=0)
        v = jnp.concatenate([rows(vl_ref, prev), rows(vl_ref, i), rows(vl_ref, nxt), vc], axis=0)
        qs = _stack_heads(q_ref[t * tq:(t + 1) * tq, :])
        s = _dot_nt(qs, k)
        off_prev = jnp.where(i > 0, 0.0, NEG_INF)
        off_next = jnp.where(i < nb - 1, 0.0, NEG_INF)
        s = jnp.concatenate([jnp.where(in_prev, s[:, 0:tq] + off_prev, NEG_INF), s[:, tq:2 * tq],
                             jnp.where(in_next, s[:, 2 * tq:3 * tq] + off_next, NEG_INF), s[:, 3 * tq:]], axis=1)
        m = jnp.maximum(jnp.max(s, axis=1, keepdims=True), snk)
        p = jnp.exp(s - m).astype(BF16)
        v0, v1 = _aug_values(v)
        acc0 = jnp.dot(p[:2 * tq], v0, preferred_element_type=F32)
        acc1 = jnp.dot(p[2 * tq:], v1, preferred_element_type=F32)
        o_ref[t * tq:(t + 1) * tq, :] = _finish_heads(acc0, acc1, jnp.exp(snk - m), tq).astype(BF16)


def _window_attention(p_lat, p_ctx, sink):
    bsz, ll, _ = p_lat.shape
    lc = p_ctx.shape[1]
    nb = ll // Q_BLOCK
    tqs = WIN_BLOCKS_PER_STEP * Q_BLOCK
    assert ll % tqs == 0
    grid_spec = pltpu.PrefetchScalarGridSpec(
        num_scalar_prefetch=1,
        grid=(bsz, ll // tqs),
        in_specs=[pl.BlockSpec((None, tqs, 256), lambda b, i, *_: (b, i, 2)),
                  pl.BlockSpec((None, ll, LANES), lambda b, i, *_: (b, 0, 6)),
                  pl.BlockSpec((None, ll, LANES), lambda b, i, *_: (b, 0, 7)),
                  pl.BlockSpec((None, lc, LANES), lambda b, i, *_: (b, 0, 6)),
                  pl.BlockSpec((None, lc, LANES), lambda b, i, *_: (b, 0, 7))],
        out_specs=pl.BlockSpec((None, tqs, 256), lambda b, i, *_: (b, i, 0)))
    return pl.pallas_call(
        functools.partial(_win_kernel, nb=nb),
        out_shape=jax.ShapeDtypeStruct((bsz, ll, 256), BF16), grid_spec=grid_spec,
        compiler_params=_cparams(("arbitrary", "arbitrary")),
        name="attn_window",
    )(sink, p_lat, p_lat, p_lat, p_ctx, p_ctx)


FFT_ROWS = 8


def _fft1_kernel(u_ref, w_ref, y_ref):
    l1, rows, w = u_ref.shape
    u = u_ref[...].reshape(l1 * rows, w).astype(BF16)
    y = jnp.dot(w_ref[...], u, preferred_element_type=F32)
    y_ref[...] = y.reshape(2, l1, rows, w)


def _channel_mix(ab, g_ref, fm_ref):
    z = jnp.dot(ab.astype(BF16), g_ref[...], preferred_element_type=F32)
    return jnp.dot(z.astype(BF16), fm_ref[...], preferred_element_type=F32)


def _fft2_kernel(y_ref, c_ref, s_ref, g_ref, fm_ref, o_ref):
    l2 = y_ref.shape[2]
    ab = []
    for r in range(FFT_ROWS):
        yr, yi = y_ref[0, r].astype(BF16), y_ref[1, r].astype(BF16)
        cs = jnp.concatenate([c_ref[r], s_ref[r]], axis=1)
        rhs = jnp.concatenate([jnp.concatenate([yr, yi], axis=1),
                               jnp.concatenate([yi, -yr], axis=1)], axis=0)
        ab.append(jnp.dot(cs, rhs, preferred_element_type=F32))
    o = _channel_mix(jnp.concatenate(ab, axis=0), g_ref, fm_ref)
    for r in range(FFT_ROWS):
        o_ref[:, r, :] = o[r * l2:(r + 1) * l2]


def _fft_direct_kernel(u_ref, cs_ref, g_ref, fm_ref, o_ref):
    n = u_ref.shape[0]
    y = jnp.dot(cs_ref[...], u_ref[...].astype(BF16), preferred_element_type=F32)
    o_ref[...] = _channel_mix(jnp.concatenate([y[:n], y[n:]], axis=1), g_ref, fm_ref)


def _mxu_const(a):
    return jnp.asarray(a, F32).astype(BF16)


def _dft_tables(n_rows, n_cols, length, row_stride=1, row_offset=0):
    k = row_offset + row_stride * np.arange(n_rows, dtype=np.int64)
    n = np.arange(n_cols, dtype=np.int64)
    ang = 2.0 * np.pi * ((k[:, None] * n[None, :]) % length).astype(np.float64) / length
    return np.cos(ang), np.sin(ang)


def _channel_dft(width, length):
    c, s = _dft_tables(HEAD_DIM, HEAD_DIM, HEAD_DIM)
    eye = np.eye(width // HEAD_DIM) / np.sqrt(float(length) * HEAD_DIM)
    return _mxu_const(np.concatenate([np.kron(eye, c), np.kron(eye, s)], axis=0))


def _fourier_latent(u, fm_bd):
    bsz, length, w = u.shape
    l2 = FFT_L2
    l1 = length // l2
    rows = FFT_ROWS
    c1, s1 = _dft_tables(l1, l1, l1)
    w1 = _mxu_const(np.kron(np.concatenate([c1, -s1], axis=0), np.eye(rows)))
    y = pl.pallas_call(
        _fft1_kernel,
        out_shape=jax.ShapeDtypeStruct((bsz, 2, l1, l2, w), F32),
        grid=(bsz, l2 // rows),
        in_specs=[pl.BlockSpec((None, l1, rows, w), lambda b, j: (b, 0, j, 0)),
                  pl.BlockSpec(w1.shape, lambda b, j: (0, 0))],
        out_specs=pl.BlockSpec((None, 2, l1, rows, w), lambda b, j: (b, 0, 0, j, 0)),
        compiler_params=_cparams(("arbitrary", "arbitrary")),
        name="fourier_stage1",
    )(u.reshape(bsz, l1, l2, w), w1)
    tabs = [_dft_tables(l2, l2, length, row_stride=l1, row_offset=k1) for k1 in range(l1)]
    ck = _mxu_const(np.stack([t[0] for t in tabs]))
    sk = _mxu_const(np.stack([t[1] for t in tabs]))
    const2 = lambda b, k: (0, 0)
    out = pl.pallas_call(
        _fft2_kernel,
        out_shape=jax.ShapeDtypeStruct((bsz, l2, l1, w), F32),
        grid=(bsz, l1 // rows),
        in_specs=[pl.BlockSpec((None, 2, rows, l2, w), lambda b, k: (b, 0, k, 0, 0)),
                  pl.BlockSpec((rows, l2, l2), lambda b, k: (k, 0, 0)),
                  pl.BlockSpec((rows, l2, l2), lambda b, k: (k, 0, 0)),
                  pl.BlockSpec((2 * w, w), const2), pl.BlockSpec((w, w), const2)],
        out_specs=pl.BlockSpec((None, l2, rows, w), lambda b, k: (b, 0, k, 0)),
        compiler_params=_cparams(("arbitrary", "arbitrary")),
        name="fourier_stage2",
    )(y, ck, sk, _channel_dft(w, length), fm_bd)
    return out.reshape(bsz, length, w)


def _fourier_direct(u, fm_bd):
    bsz, length, w = u.shape
    c, s = _dft_tables(length, length, length)
    cs = _mxu_const(np.concatenate([c, -s], axis=0))
    const2 = lambda b: (0, 0)
    return pl.pallas_call(
        _fft_direct_kernel,
        out_shape=jax.ShapeDtypeStruct((bsz, length, w), F32),
        grid=(bsz,),
        in_specs=[pl.BlockSpec((None, length, w), lambda b: (b, 0, 0)),
                  pl.BlockSpec((2 * length, length), const2),
                  pl.BlockSpec((2 * w, w), const2), pl.BlockSpec((w, w), const2)],
        out_specs=pl.BlockSpec((None, length, w), lambda b: (b, 0, 0)),
        compiler_params=_cparams(("arbitrary",)),
        name="fourier_ctx",
    )(u, cs, _channel_dft(w, length), fm_bd)


def _log_sigmoid(x):
    return jnp.minimum(x, 0.0) - jnp.log1p(jnp.exp(-jnp.abs(x)))


def _ret_kernel(*refs, need_ctx):
    (rdl_ref, rdh_ref, qf_ref, kf_ref, vf_ref, qb_ref, kb_ref, vb_ref, qc_ref, kc_ref, vc_ref) = refs[:11]
    if need_ctx:
        of_ref, ob_ref, oc_ref = refs[11:14]
        scr = refs[14:]
    else:
        of_ref, ob_ref = refs[11:13]
        oc_ref = None
        scr = refs[13:]
    sf_ref, sb_ref, din_ref, tab_ref = scr
    c = RET_CHUNK
    w = 4 * HEAD_DIM
    j = pl.program_id(1)
    head_shift = HEAD_DIM.bit_length() - 1
    lane_head = lax.broadcasted_iota(jnp.int32, (c, w), 1) >> head_shift
    blockdiag = ((lax.broadcasted_iota(jnp.int32, (w, w), 0) >> head_shift)
                 == (lax.broadcasted_iota(jnp.int32, (w, w), 1) >> head_shift))

    def chunk(q, k, v, d):
        qf = q.astype(F32)
        o = jnp.zeros((c, w), F32)
        for h in range(4):
            hm = lane_head == h
            qh = jnp.where(hm, qf, 0.0).astype(BF16)
            inner = (_dot_nt(qh, k) * din_ref[d, h]).astype(BF16)
            o = o + jnp.where(hm, jnp.dot(inner, v, preferred_element_type=F32), 0.0)
        kz = (k.astype(F32) * tab_ref[d, 1]).T.astype(BF16)
        kv = jnp.where(blockdiag, jnp.dot(kz, v, preferred_element_type=F32), 0.0)
        return o, kv

    @pl.when(j == 0)
    def _():
        t = lax.broadcasted_iota(jnp.int32, (c, w), 0).astype(F32)
        rr = lax.broadcasted_iota(jnp.int32, (c, c), 0)
        cc = lax.broadcasted_iota(jnp.int32, (c, c), 1)
        for d in range(2):
            lg = _log_sigmoid(rdl_ref[d])
            tab_ref[d, 0] = jnp.exp(lg * ((t + 1.0) if d == 0 else (c - t)))
            tab_ref[d, 1] = jnp.exp(lg * ((c - 1.0 - t) if d == 0 else t))
            tab_ref[d, 2] = jnp.exp(jnp.broadcast_to(lg, (c, w)) * float(c))
            diff = (rr - cc) if d == 0 else (cc - rr)
            dpos = jnp.maximum(diff, 0).astype(F32)
            for h in range(4):
                lgh = _log_sigmoid(rdh_ref[d, h])
                din_ref[d, h] = jnp.where(diff >= 0, jnp.exp(lgh * dpos), 0.0)
        q, k, v = qc_ref[...], kc_ref[...], vc_ref[...]
        o_f, kv_f = chunk(q, k, v, 0)
        o_b, kv_b = chunk(q, k, v, 1)
        sf_ref[...] = kv_f
        sb_ref[...] = kv_b
        if need_ctx:
            oc_ref[...] = o_f + o_b

    @pl.when(j > 0)
    def _():
        for d, (q_ref, k_ref, v_ref, s_ref, o_ref) in enumerate(
                ((qf_ref, kf_ref, vf_ref, sf_ref, of_ref), (qb_ref, kb_ref, vb_ref, sb_ref, ob_ref))):
            q, k, v = q_ref[...], k_ref[...], v_ref[...]
            o, kv = chunk(q, k, v, d)
            state = s_ref[...]
            o_ref[...] = o + jnp.dot(q, state.astype(BF16), preferred_element_type=F32) * tab_ref[d, 0]
            s_ref[...] = state * tab_ref[d, 2, 0:1, :] + kv


def _retention(p_lat, p_ctx, rdl, rdh, need_ctx):
    bsz, ll, _ = p_lat.shape
    lc = p_ctx.shape[1]
    c = RET_CHUNK
    w = 4 * HEAD_DIM
    assert lc == c and ll % c == 0
    n = ll // c
    fwd = lambda blk: (lambda b, j: (b, jnp.maximum(j - 1, 0), blk))
    bwd = lambda blk: (lambda b, j: (b, n - 1 - jnp.maximum(j - 1, 0), blk))
    ctx = lambda blk: (lambda b, j: (b, 0, blk))
    in_specs = [pl.BlockSpec((2, 1, w), lambda b, j: (0, 0, 0)),
                pl.BlockSpec((2, 4, 1, c), lambda b, j: (0, 0, 0, 0))]
    in_specs += [pl.BlockSpec((None, c, w), fwd(blk)) for blk in (4, 5, 6)]
    in_specs += [pl.BlockSpec((None, c, w), bwd(blk)) for blk in (4, 5, 6)]
    in_specs += [pl.BlockSpec((None, c, w), ctx(blk)) for blk in (4, 5, 6)]
    out_shape = [jax.ShapeDtypeStruct((bsz, ll, w), F32), jax.ShapeDtypeStruct((bsz, ll, w), F32)]
    out_specs = [pl.BlockSpec((None, c, w), fwd(0)), pl.BlockSpec((None, c, w), bwd(0))]
    if need_ctx:
        out_shape.append(jax.ShapeDtypeStruct((bsz, lc, w), F32))
        out_specs.append(pl.BlockSpec((None, c, w), ctx(0)))
    return pl.pallas_call(
        functools.partial(_ret_kernel, need_ctx=need_ctx),
        out_shape=tuple(out_shape),
        grid=(bsz, n + 1),
        in_specs=in_specs,
        out_specs=tuple(out_specs),
        scratch_shapes=[pltpu.VMEM((w, w), F32), pltpu.VMEM((w, w), F32),
                        pltpu.VMEM((2, 4, c, c), F32), pltpu.VMEM((2, 3, c, w), F32)],
        compiler_params=_cparams(("arbitrary", "arbitrary")),
        name="retention_ctx_out" if need_ctx else "retention",
    )(rdl, rdh, *([p_lat] * 6), *([p_ctx] * 3))


def _out_kernel(*refs, n_o, alpha):
    a_ref, b_ref, f_ref, g_ref = refs[:4]
    o_refs = refs[4:4 + n_o]
    x_ref, g1_ref, lnw_ref, lnb_ref, w_ref, gnw_ref, gavg_ref, out_ref = refs[4 + n_o:]
    o = o_refs[0][...]
    for r in o_refs[1:]:
        o = o + r[...]
    gavg = gavg_ref[...]
    dlt = o - _group_mean(o, gavg)
    on = dlt * lax.rsqrt(_group_mean(dlt * dlt, gavg) + NORM_EPS) * gnw_ref[...]
    r = (_silu(g_ref[...]) * on).astype(BF16)
    cat = jnp.concatenate([a_ref[...], b_ref[...], f_ref[...].astype(BF16), r], axis=1)
    y = jnp.dot(cat, w_ref[...], preferred_element_type=F32)
    z = alpha * x_ref[...] + g1_ref[...] * y
    out_ref[...] = _ln(z) * lnw_ref[...] + lnb_ref[...]


def _out_proj(a, b, f, g, o_parts, x, mod, mod_row, lnw, lnb, w, gnw, gavg, alpha, tm):
    bsz, length, d = x.shape
    row = lambda bb, i: (bb, i, 0)
    const2 = lambda bb, i: (0, 0)
    blk256 = pl.BlockSpec((None, tm, 256), row)
    in_specs = [blk256] * (4 + len(o_parts)) + [
        pl.BlockSpec((None, tm, d), row),
        pl.BlockSpec((None, None, 1, d), lambda bb, i: (mod_row(bb), 2, 0, 0)),
        pl.BlockSpec((1, d), const2), pl.BlockSpec((1, d), const2),
        pl.BlockSpec((d, d), const2),
        pl.BlockSpec((1, 256), const2), pl.BlockSpec((256, 256), const2)]
    return pl.pallas_call(
        functools.partial(_out_kernel, n_o=len(o_parts), alpha=alpha),
        out_shape=jax.ShapeDtypeStruct((bsz, length, d), F32),
        grid=(bsz, length // tm),
        in_specs=in_specs,
        out_specs=pl.BlockSpec((None, tm, d), row),
        compiler_params=_cparams(("arbitrary", "arbitrary")),
        name="out_proj",
    )(a, b, f, g, *o_parts, x, mod, lnw, lnb, w, gnw, gavg)


def _ffn_kernel(x_ref, sh_ref, sc_ref, g2_ref, lnw_ref, lnb_ref, wg_ref, wu_ref, wd_ref, out_ref, *, fc, alpha):
    x = x_ref[...]
    h = (_ln(x) * (1.0 + sc_ref[...]) + sh_ref[...]).astype(BF16)
    acc = jnp.zeros(x.shape, F32)
    for c in range(wg_ref.shape[1] // fc):
        cols = slice(c * fc, (c + 1) * fc)
        gate = jnp.dot(h, wg_ref[:, cols], preferred_element_type=F32)
        up = jnp.dot(h, wu_ref[:, cols], preferred_element_type=F32)
        act = (_silu(gate) * up).astype(BF16)
        acc = acc + jnp.dot(act, wd_ref[cols, :], preferred_element_type=F32)
    z = alpha * x + g2_ref[...] * acc
    out_ref[...] = _ln(z) * lnw_ref[...] + lnb_ref[...]


def _ffn(x, mod, mod_row, lnw, lnb, wg, wu, wd, alpha, tm):
    bsz, length, d = x.shape
    ff = wg.shape[1]
    fc = ff // 2 if (ff // 2) % LANES == 0 else ff
    row = lambda bb, i: (bb, i, 0)
    const2 = lambda bb, i: (0, 0)
    modspec = lambda which: pl.BlockSpec((None, None, 1, d), lambda bb, i: (mod_row(bb), which, 0, 0))
    resident = functools.partial(pl.BlockSpec, index_map=const2, pipeline_mode=pl.Buffered(1))
    return pl.pallas_call(
        functools.partial(_ffn_kernel, fc=fc, alpha=alpha),
        out_shape=jax.ShapeDtypeStruct((bsz, length, d), F32),
        grid=(bsz, length // tm),
        in_specs=[pl.BlockSpec((None, tm, d), row), modspec(3), modspec(4), modspec(5),
                  pl.BlockSpec((1, d), const2), pl.BlockSpec((1, d), const2),
                  resident((d, ff)), resident((d, ff)), resident((ff, d))],
        out_specs=pl.BlockSpec((None, tm, d), row),
        compiler_params=_cparams(("arbitrary", "arbitrary")),
        name="ffn",
    )(x, mod, mod, mod, lnw, lnb, wg, wu, wd)


def _head_perm(off):
    return [(off + h * HEAD_DIM, off + (h + 1) * HEAD_DIM) for h in (0, 2, 1, 3)]


def _take_ranges(w, ranges, axis):
    return jnp.concatenate([lax.slice_in_dim(w, lo, hi, axis=axis) for lo, hi in ranges], axis=axis)


def _rope_tables(seq):
    t = jnp.arange(seq, dtype=jnp.int32)
    row = (t // GRID_W).astype(F32)
    col = (t % GRID_W).astype(F32)

    def tab(pos, n_freq):
        inv = ROPE_THETA ** (-jnp.arange(n_freq, dtype=F32) / n_freq)
        ang = pos[:, None] * inv[None, :]
        return jnp.cos(ang), jnp.sin(ang)

    nf = HEAD_DIM // 4
    cr, sr = tab(row, nf)
    cc, sc = tab(col, nf)
    z = jnp.zeros_like(sr)
    c2 = jnp.tile(jnp.concatenate([cr, cr, cc, cc], -1), (1, 2))
    sa2 = jnp.tile(jnp.concatenate([-sr, z, -sc, z], -1), (1, 2))
    sb2 = jnp.tile(jnp.concatenate([z, sr, z, sc], -1), (1, 2))
    ct, st = tab(t.astype(F32), HEAD_DIM // 2)
    z = jnp.zeros_like(st)
    c1 = jnp.tile(jnp.concatenate([ct, ct], -1), (1, 2))
    sa1 = jnp.tile(jnp.concatenate([-st, z], -1), (1, 2))
    sb1 = jnp.tile(jnp.concatenate([z, st], -1), (1, 2))
    return c2, sa2, sb2, c1, sa1, sb1


def kernel(x, c, ctx, c_ctx, w_mod, b_mod, w_in, a_q_norm, a_k_norm, b_sink, f_mix, r_decay, r_gn_w, w_out,
           ln1_w, ln1_b, w_gate_up, w_down, ln2_w, ln2_b):
    bsz, seq, d = x.shape
    depth = w_in.shape[0]
    ff = w_down.shape[1]
    gw = d // 4
    assert gw == 4 * HEAD_DIM and a_q_norm.shape[-1] == HEAD_DIM and seq % (FFT_L2 * 8) == 0
    alpha = (2.0 * depth) ** 0.25

    tabs = _rope_tables(seq)
    gavg = jnp.asarray(np.kron(np.eye(gw // HEAD_DIM), np.full((HEAD_DIM, HEAD_DIM), 1.0 / HEAD_DIM)), BF16)
    cc = jnp.zeros((8, d), F32).at[:bsz].set(c).at[bsz].set(c_ctx)
    mod_all = _modulation(cc, w_mod, b_mod).reshape(depth, 8, 6, 1, d)
    lat_row = lambda b: b
    ctx_row = lambda b: bsz

    perm_in = (_head_perm(0) + [(256, 512)] + _head_perm(512) + [(768, 1024), (1280, 2048), (1024, 1280),
                                                                  (2048, 2304)])
    perm_out = _head_perm(0) + _head_perm(gw) + [(2 * gw, 4 * gw)]
    eye_g = jnp.eye(gw // HEAD_DIM, dtype=F32)

    tm = 512
    for layer in range(depth):
        need_ctx = layer < depth - 1
        mod = mod_all[layer]
        w_in_l = _take_ranges(w_in[layer], perm_in, 1).astype(BF16)
        w_out_l = _take_ranges(w_out[layer], perm_out, 0).astype(BF16)
        wg = w_gate_up[layer][:, :ff].astype(BF16)
        wu = w_gate_up[layer][:, ff:].astype(BF16)
        wd = w_down[layer].astype(BF16)
        qn = jnp.tile(a_q_norm[layer], 4)[None, :]
        kn = jnp.tile(a_k_norm[layer], 2)[None, :]
        gnw = r_gn_w[layer][None, :]
        lnw1, lnb1 = ln1_w[layer][None, :], ln1_b[layer][None, :]
        lnw2, lnb2 = ln2_w[layer][None, :], ln2_b[layer][None, :]
        fm_bd = jnp.einsum('gh,gce->gche', eye_g, f_mix[layer]).reshape(gw, gw).astype(BF16)
        rd = r_decay[layer]
        rdl = jnp.repeat(rd, HEAD_DIM, axis=1)[:, None, :]
        rdh = jnp.broadcast_to(rd[:, :, None, None], (2, 4, 1, RET_CHUNK))
        sink = b_sink[layer]

        p_l, u_l, g_l = _in_proj(x, mod, lat_row, w_in_l, qn, kn, gavg, tabs, tm)
        p_c, u_c, g_c = _in_proj(ctx, mod, ctx_row, w_in_l, qn, kn, gavg, None, ctx.shape[1])

        a_l = _attention(p_l, 0, p_l, p_c, 2, 3, None, tq=2 * Q_BLOCK, tk=min(2048, seq))
        b_l = _window_attention(p_l, p_c, sink)
        f_l = _fourier_latent(u_l, fm_bd)
        r_out = _retention(p_l, p_c, rdl, rdh, need_ctx)
        x = _out_proj(a_l, b_l, f_l, g_l, r_out[:2], x, mod, lat_row, lnw1, lnb1, w_out_l, gnw, gavg, alpha, tm)
        x = _ffn(x, mod, lat_row, lnw2, lnb2, wg, wu, wd, alpha, tm)
        if need_ctx:
            lc = ctx.shape[1]
            a_c = _attention(p_c, 0, None, p_c, 2, 3, None, tq=Q_BLOCK, tk=2048)
            b_c = _attention(p_c, 2, None, p_c, 6, 7, sink, tq=Q_BLOCK, tk=2048)
            f_c = _fourier_direct(u_c, fm_bd)
            ctx = _out_proj(a_c, b_c, f_c, g_c, r_out[2:], ctx, mod, ctx_row, lnw1, lnb1, w_out_l, gnw, gavg,
                            alpha, lc)
            ctx = _ffn(ctx, mod, ctx_row, lnw2, lnb2, wg, wu, wd, alpha, lc)
    return x
```

```python
import functools

import numpy as np
import jax
import jax.numpy as jnp
from jax import lax
from jax.experimental import pallas as pl
from jax.experimental.pallas import tpu as pltpu

F32 = jnp.float32
BF16 = jnp.bfloat16
HIGHEST = lax.Precision.HIGHEST

HEAD_DIM = 64
GRID_W = 64
Q_BLOCK = 128
ROPE_THETA = 10000.0
NORM_EPS = 1e-6
NEG_INF = -1e30
LOG2E = 1.4426950408889634

LANES = 128
VMEM_LIMIT_BYTES = 56 * 1024 * 1024

RET_CHUNK = 256
RET_STEP_CHUNKS = 4
FFT_L2 = 128


def _cparams(sem, flags=None):
    return pltpu.CompilerParams(dimension_semantics=sem, vmem_limit_bytes=VMEM_LIMIT_BYTES, flags=flags)


def _ln(x):
    mu = jnp.mean(x, axis=-1, keepdims=True)
    xc = x - mu
    var = jnp.mean(xc * xc, axis=-1, keepdims=True)
    return xc * lax.rsqrt(var + NORM_EPS)


def _silu(x):
    return x * jax.nn.sigmoid(x)


def _group_mean(t, g):
    hi = t.astype(BF16)
    lo = (t - hi.astype(F32)).astype(BF16)
    return (jnp.dot(hi, g, preferred_element_type=F32) + jnp.dot(lo, g, preferred_element_type=F32))


def _dot_nt(a, b):
    return lax.dot_general(a, b, (((1,), (1,)), ((), ())), preferred_element_type=F32)


def _mod_kernel(c_ref, w_ref, b_ref, o_ref):
    h = _silu(c_ref[...])
    o_ref[...] = jnp.dot(h, w_ref[...], precision=HIGHEST, preferred_element_type=F32) + b_ref[...]


def _modulation(cc, w_mod, b_mod):
    depth, d, n = w_mod.shape
    tn = 2048
    return pl.pallas_call(
        _mod_kernel,
        out_shape=jax.ShapeDtypeStruct((depth, 8, n), F32),
        grid=(depth, n // tn),
        in_specs=[pl.BlockSpec((8, d), lambda l, j: (0, 0)),
                  pl.BlockSpec((None, d, tn), lambda l, j: (l, 0, j)),
                  pl.BlockSpec((None, 1, tn), lambda l, j: (l, 0, j))],
        out_specs=pl.BlockSpec((None, 8, tn), lambda l, j: (l, 0, j)),
        compiler_params=_cparams(("arbitrary", "arbitrary")),
        name="modulation",
    )(cc, w_mod, b_mod.reshape(depth, 1, n))


P_COLS = 14 * LANES


def _rope_lanes(t, c, ss, half):
    first = (lax.broadcasted_iota(jnp.int32, (t.shape[0], LANES), 1) & half) == 0
    outs = []
    for j in range(t.shape[1] // LANES):
        tj = t[:, j * LANES:(j + 1) * LANES]
        partner = jnp.where(first, pltpu.roll(tj, LANES - half, 1), pltpu.roll(tj, half, 1))
        outs.append(tj * c + partner * ss)
    return outs[0] if len(outs) == 1 else jnp.concatenate(outs, axis=1)


IN_ROW_SLABS = 4
IN_TILE = 1024


def _pair_heads_by_kv(q):
    a, b = q[:, 0:LANES], q[:, LANES:2 * LANES]
    lo = lax.broadcasted_iota(jnp.int32, a.shape, 1) < HEAD_DIM
    return jnp.concatenate([jnp.where(lo, a, pltpu.roll(b, HEAD_DIM, 1)),
                            jnp.where(lo, pltpu.roll(a, HEAD_DIM, 1), b)], axis=1)


def _in_kernel(*refs, rope):
    x_ref, sh_ref, sc_ref, w_ref, qn_ref, kn_ref, gavg_ref = refs[:7]
    if rope:
        c2_ref, ss2_ref, c1_ref, ss1_ref, p_ref, u_ref, g_ref = refs[7:]
    else:
        p_ref, u_ref, g_ref = refs[7:]
    def rms(t, w, g):
        return t * lax.rsqrt(_group_mean(t * t, g) + NORM_EPS) * w

    scale = HEAD_DIM ** -0.5
    qscale = scale * LOG2E
    slabs = IN_ROW_SLABS if x_ref.shape[0] % (8 * IN_ROW_SLABS) == 0 else 1
    rows = x_ref.shape[0] // slabs
    for r in range(slabs):
        rs = slice(r * rows, (r + 1) * rows)

        def rope2(t):
            if not rope:
                return t
            return _rope_lanes(t, c2_ref[rs, :], ss2_ref[rs, :], HEAD_DIM // 4)

        def rope1(t):
            if not rope:
                return t
            return _rope_lanes(t, c1_ref[rs, :], ss1_ref[rs, :], HEAD_DIM // 2)

        h = _ln(x_ref[rs, :]) * (1.0 + sc_ref[...]) + sh_ref[...]
        y = jnp.dot(h.astype(BF16), w_ref[...], preferred_element_type=F32)
        qa = rope2(rms(y[:, 0:256], qn_ref[...], gavg_ref[...])) * qscale
        ka = rope2(rms(y[:, 256:384], kn_ref[...], gavg_ref[0:LANES, 0:LANES]))
        p_ref[rs, 0:256] = _pair_heads_by_kv(qa).astype(BF16)
        p_ref[rs, 256:384] = ka.astype(BF16)
        p_ref[rs, 384:512] = y[:, 384:512].astype(BF16)
        p_ref[rs, 512:768] = _pair_heads_by_kv(rope2(y[:, 512:768]) * qscale).astype(BF16)
        p_ref[rs, 768:896] = rope2(y[:, 768:896]).astype(BF16)
        p_ref[rs, 896:1024] = y[:, 896:1024].astype(BF16)
        u_ref[rs, :] = y[:, 1024:1280]
        p_ref[rs, 1024:1280] = rope1(y[:, 1280:1536]).astype(BF16)
        p_ref[rs, 1280:1536] = (rope1(y[:, 1536:1792]) * scale).astype(BF16)
        p_ref[rs, 1536:1792] = y[:, 1792:2048].astype(BF16)
        g_ref[rs, :] = y[:, 2048:2304]


def _in_proj(x, mod, mod_row, w, layer, qn, kn, gavg, tabs, tm):
    bsz, length, d = x.shape
    nw = w.shape[2]
    nt = length // tm
    rope = tabs is not None
    row = lambda b, i: (b, i, 0)
    const2 = lambda b, i: (0, 0)
    in_specs = [pl.BlockSpec((None, tm, d), row),
                pl.BlockSpec((None, None, 1, d), lambda b, i: (mod_row(b), 0, 0, 0)),
                pl.BlockSpec((None, None, 1, d), lambda b, i: (mod_row(b), 1, 0, 0)),
                pl.BlockSpec((None, d, nw), lambda b, i: (layer, 0, 0)),
                pl.BlockSpec((1, 256), const2),
                pl.BlockSpec((1, LANES), const2),
                pl.BlockSpec((256, 256), const2)]
    args = [x, mod, mod, w, qn, kn, gavg]
    if rope:
        in_specs += [pl.BlockSpec((tm, LANES), lambda b, i: (i, 0))] * len(tabs)
        args += list(tabs)
    return pl.pallas_call(
        functools.partial(_in_kernel, rope=rope),
        out_shape=(jax.ShapeDtypeStruct((bsz, length, P_COLS), BF16),
                   jax.ShapeDtypeStruct((bsz, length, 256), F32),
                   jax.ShapeDtypeStruct((bsz, length, 256), F32)),
        grid=(bsz, nt),
        in_specs=in_specs,
        out_specs=(pl.BlockSpec((None, tm, P_COLS), row),
                   pl.BlockSpec((None, tm, 256), row),
                   pl.BlockSpec((None, tm, 256), row)),
        compiler_params=_cparams(("arbitrary", "arbitrary")),
        name="in_proj_rope" if rope else "in_proj_ctx",
    )(*args)


def _stack_heads(q):
    qf = q.astype(F32)
    lo = lax.broadcasted_iota(jnp.int32, (q.shape[0], LANES), 1) < HEAD_DIM
    q0, q1 = qf[:, 0:LANES], qf[:, LANES:2 * LANES]
    z = jnp.zeros_like(q0)
    return jnp.concatenate([jnp.where(lo, q0, z), jnp.where(lo, q1, z),
                            jnp.where(lo, z, q0), jnp.where(lo, z, q1)], axis=0).astype(BF16)


def _aug_values(v):
    vf = v.astype(F32)
    lo = lax.broadcasted_iota(jnp.int32, vf.shape, 1) < HEAD_DIM
    one = jnp.ones_like(vf)
    return jnp.where(lo, vf, one).astype(BF16), jnp.where(lo, one, vf).astype(BF16)


def _finish_heads(acc0, acc1, e, tq):
    l0 = pltpu.roll(acc0, HEAD_DIM, 1)
    l1 = pltpu.roll(acc1, HEAD_DIM, 1)
    if e is not None:
        l0 = l0 + e[:2 * tq]
        l1 = l1 + e[2 * tq:]
    n0 = acc0 / l0
    n1 = acc1 / l1
    lo = lax.broadcasted_iota(jnp.int32, (tq, LANES), 1) < HEAD_DIM
    return jnp.concatenate([jnp.where(lo, n0[:tq], pltpu.roll(n0[tq:], HEAD_DIM, 1)),
                            jnp.where(lo, pltpu.roll(n1[:tq], HEAD_DIM, 1), n1[tq:])], axis=1)


def _sink_column(sink_ref, tq):
    return jnp.concatenate([jnp.full((tq, 1), sink_ref[h] * LOG2E, F32) for h in range(4)], axis=0)


ATTN_FLAGS = None

def _attn_kernel(*refs, tq, tk, n_lat, has_sink):
    i = 0
    sink_ref = None
    if has_sink:
        sink_ref = refs[0]
        i = 1
    q_ref = refs[i]
    i += 1
    if n_lat:
        kl_ref, vl_ref = refs[i:i + 2]
        i += 2
    kc_ref, vc_ref, o_ref = refs[i:i + 3]
    i += 3
    if n_lat:
        v0l_ref, v1l_ref = refs[i:i + 2]
        i += 2
    v0c_ref, v1c_ref = refs[i:i + 2]

    def store_aug_t(v_ref, v0t_ref, v1t_ref, rows):
        vt = v_ref[rows, :].astype(F32).T
        top = lax.broadcasted_iota(jnp.int32, vt.shape, 0) < HEAD_DIM
        one = jnp.ones_like(vt)
        v0t_ref[:, rows] = jnp.where(top, vt, one).astype(BF16)
        v1t_ref[:, rows] = jnp.where(top, one, vt).astype(BF16)

    @pl.when(pl.program_id(1) == 0)
    def _():
        for c in range(n_lat):
            store_aug_t(vl_ref, v0l_ref, v1l_ref, slice(c * tk, (c + 1) * tk))
        store_aug_t(vc_ref, v0c_ref, v1c_ref, slice(None))

    qs = _stack_heads(q_ref[...])
    half = 2 * tq

    chunks = [(kl_ref, v0l_ref, v1l_ref, slice(c * tk, (c + 1) * tk)) for c in range(n_lat)]
    chunks.append((kc_ref, v0c_ref, v1c_ref, slice(None)))

    def scores_t(chunk):
        k_ref, _, _, rows = chunk
        return _dot_nt(k_ref[rows, :], qs)

    def sink_row():
        return jnp.concatenate([jnp.full((1, tq), sink_ref[h] * LOG2E, F32) for h in range(4)], axis=1)

    m = sink_row() if has_sink else jnp.full((1, 4 * tq), NEG_INF, F32)
    acc0 = jnp.zeros((LANES, half), F32)
    acc1 = jnp.zeros((LANES, half), F32)
    s_next = scores_t(chunks[0])
    for idx, (_, v0t_ref, v1t_ref, rows) in enumerate(chunks):
        s = s_next
        if idx + 1 < len(chunks):
            s_next = scores_t(chunks[idx + 1])
        m_new = jnp.maximum(m, jnp.max(s, axis=0, keepdims=True))
        alpha = jnp.exp2(m - m_new)
        p = jnp.exp2(s - m_new).astype(BF16)
        acc0 = acc0 * alpha[:, :half] + jnp.dot(v0t_ref[:, rows], p[:, :half], preferred_element_type=F32)
        acc1 = acc1 * alpha[:, half:] + jnp.dot(v1t_ref[:, rows], p[:, half:], preferred_element_type=F32)
        m = m_new
    l0, l1 = acc0[HEAD_DIM:], acc1[:HEAD_DIM]
    if has_sink:
        e = jnp.exp2(sink_row() - m)
        l0 = l0 + e[:, :half]
        l1 = l1 + e[:, half:]
    n0 = acc0[:HEAD_DIM] / l0
    n1 = acc1[HEAD_DIM:] / l1
    out_t = jnp.concatenate([n0[:, :tq], n0[:, tq:], n1[:, :tq], n1[:, tq:]], axis=0)
    o_ref[...] = out_t.T.astype(BF16)


def _attention(pq, q_blk, p_lat, p_ctx, k_blk, v_blk, sink, tq, tk):
    bsz, lq, _ = pq.shape
    lc = p_ctx.shape[1]
    assert p_lat is None or p_lat.shape[1] % tk == 0
    n_lat = 0 if p_lat is None else p_lat.shape[1] // tk
    has_sink = sink is not None
    in_specs = [pl.BlockSpec((None, tq, 256), lambda b, i, *_: (b, i, q_blk))]
    args = [pq]
    scratch = []
    if n_lat:
        ll = p_lat.shape[1]
        in_specs += [pl.BlockSpec((None, ll, LANES), lambda b, i, *_: (b, 0, k_blk)),
                     pl.BlockSpec((None, ll, LANES), lambda b, i, *_: (b, 0, v_blk))]
        args += [p_lat, p_lat]
        scratch += [pltpu.VMEM((LANES, ll), BF16), pltpu.VMEM((LANES, ll), BF16)]
    in_specs += [pl.BlockSpec((None, lc, LANES), lambda b, i, *_: (b, 0, k_blk)),
                 pl.BlockSpec((None, lc, LANES), lambda b, i, *_: (b, 0, v_blk))]
    args += [p_ctx, p_ctx]
    scratch += [pltpu.VMEM((LANES, lc), BF16), pltpu.VMEM((LANES, lc), BF16)]
    kern = functools.partial(_attn_kernel, tq=tq, tk=tk, n_lat=n_lat, has_sink=has_sink)
    grid_spec = pltpu.PrefetchScalarGridSpec(
        num_scalar_prefetch=1 if has_sink else 0,
        grid=(bsz, lq // tq),
        in_specs=in_specs,
        out_specs=pl.BlockSpec((None, tq, 256), lambda b, i, *_: (b, i, 0)),
        scratch_shapes=scratch)
    call = pl.pallas_call(
        kern, out_shape=jax.ShapeDtypeStruct((bsz, lq, 256), BF16), grid_spec=grid_spec,
        compiler_params=_cparams(("arbitrary", "arbitrary"), ATTN_FLAGS),
        name="attn_sink" if has_sink else ("attn_global" if n_lat else "attn_ctx"))
    return call(sink, *args) if has_sink else call(*args)


WIN_BLOCKS_PER_STEP = 4


def _win_kernel(sink_ref, q_ref, kl_ref, vl_ref, kc_ref, vc_ref, o_ref, *, nb):
    tq = Q_BLOCK
    r = lax.broadcasted_iota(jnp.int32, (4 * tq, tq), 0) & (tq - 1)
    j = lax.broadcasted_iota(jnp.int32, (4 * tq, tq), 1)
    in_prev = j >= r
    in_next = j <= r
    snk = _sink_column(sink_ref, tq)
    kc, vc = kc_ref[...], vc_ref[...]

    def rows(ref, blk):
        return ref[pl.ds(pl.multiple_of(blk * tq, tq), tq), :]

    for t in range(WIN_BLOCKS_PER_STEP):
        i = pl.program_id(1) * WIN_BLOCKS_PER_STEP + t
        prev = jnp.maximum(i - 1, 0)
        nxt = jnp.minimum(i + 1, nb - 1)
        k = jnp.concatenate([rows(kl_ref, prev), rows(kl_ref, i), rows(kl_ref, nxt), kc], axis=0)
        v = jnp.concatenate([rows(vl_ref, prev), rows(vl_ref, i), rows(vl_ref, nxt), vc], axis=0)
        qs = _stack_heads(q_ref[t * tq:(t + 1) * tq, :])
        s = _dot_nt(qs, k)
        off_prev = jnp.where(i > 0, 0.0, NEG_INF)
        off_next = jnp.where(i < nb - 1, 0.0, NEG_INF)
        s = jnp.concatenate([jnp.where(in_prev, s[:, 0:tq] + off_prev, NEG_INF), s[:, tq:2 * tq],
                             jnp.where(in_next, s[:, 2 * tq:3 * tq] + off_next, NEG_INF), s[:, 3 * tq:]], axis=1)
        m = jnp.maximum(jnp.max(s, axis=1, keepdims=True), snk)
        p = jnp.exp2(s - m).astype(BF16)
        v0, v1 = _aug_values(v)
        acc0 = jnp.dot(p[:2 * tq], v0, preferred_element_type=F32)
        acc1 = jnp.dot(p[2 * tq:], v1, preferred_element_type=F32)
        o_ref[t * tq:(t + 1) * tq, :] = _finish_heads(acc0, acc1, jnp.exp2(snk - m), tq).astype(BF16)


def _window_attention(p_lat, p_ctx, sink):
    bsz, ll, _ = p_lat.shape
    lc = p_ctx.shape[1]
    nb = ll // Q_BLOCK
    tqs = WIN_BLOCKS_PER_STEP * Q_BLOCK
    assert ll % tqs == 0
    grid_spec = pltpu.PrefetchScalarGridSpec(
        num_scalar_prefetch=1,
        grid=(bsz, ll // tqs),
        in_specs=[pl.BlockSpec((None, tqs, 256), lambda b, i, *_: (b, i, 2)),
                  pl.BlockSpec((None, ll, LANES), lambda b, i, *_: (b, 0, 6)),
                  pl.BlockSpec((None, ll, LANES), lambda b, i, *_: (b, 0, 7)),
                  pl.BlockSpec((None, lc, LANES), lambda b, i, *_: (b, 0, 6)),
                  pl.BlockSpec((None, lc, LANES), lambda b, i, *_: (b, 0, 7))],
        out_specs=pl.BlockSpec((None, tqs, 256), lambda b, i, *_: (b, i, 0)))
    return pl.pallas_call(
        functools.partial(_win_kernel, nb=nb),
        out_shape=jax.ShapeDtypeStruct((bsz, ll, 256), BF16), grid_spec=grid_spec,
        compiler_params=_cparams(("arbitrary", "arbitrary")),
        name="attn_window",
    )(sink, p_lat, p_lat, p_lat, p_ctx, p_ctx)


FFT_ROWS = 8


def _fft1_kernel(u_ref, w_ref, y_ref):
    l1, rows, w = u_ref.shape
    u = u_ref[...].reshape(l1 * rows, w).astype(BF16)
    y = jnp.dot(w_ref[...], u, preferred_element_type=F32)
    y_ref[...] = y.reshape(2, l1, rows, w)


def _channel_mix(ab, g_ref, fm_ref):
    z = jnp.dot(ab.astype(BF16), g_ref[...], preferred_element_type=F32)
    return jnp.dot(z.astype(BF16), fm_ref[...], preferred_element_type=F32)


def _fft2_kernel(y_ref, c_ref, s_ref, g_ref, fm_ref, o_ref):
    l2 = y_ref.shape[2]
    ab = []
    for r in range(FFT_ROWS):
        yr, yi = y_ref[0, r].astype(BF16), y_ref[1, r].astype(BF16)
        cs = jnp.concatenate([c_ref[r], s_ref[r]], axis=1)
        rhs = jnp.concatenate([jnp.concatenate([yr, yi], axis=1),
                               jnp.concatenate([yi, -yr], axis=1)], axis=0)
        ab.append(jnp.dot(cs, rhs, preferred_element_type=F32))
    o = _channel_mix(jnp.concatenate(ab, axis=0), g_ref, fm_ref)
    for r in range(FFT_ROWS):
        o_ref[:, r, :] = o[r * l2:(r + 1) * l2]


def _fft_direct_kernel(u_ref, cs_ref, g_ref, fm_ref, o_ref):
    n = u_ref.shape[0]
    y = jnp.dot(cs_ref[...], u_ref[...].astype(BF16), preferred_element_type=F32)
    o_ref[...] = _channel_mix(jnp.concatenate([y[:n], y[n:]], axis=1), g_ref, fm_ref)


def _mxu_const(a):
    return jnp.asarray(a, F32).astype(BF16)


def _dft_tables(n_rows, n_cols, length, row_stride=1, row_offset=0):
    k = row_offset + row_stride * np.arange(n_rows, dtype=np.int64)
    n = np.arange(n_cols, dtype=np.int64)
    ang = 2.0 * np.pi * ((k[:, None] * n[None, :]) % length).astype(np.float64) / length
    return np.cos(ang), np.sin(ang)


def _channel_dft(width, length):
    c, s = _dft_tables(HEAD_DIM, HEAD_DIM, HEAD_DIM)
    eye = np.eye(width // HEAD_DIM) / np.sqrt(float(length) * HEAD_DIM)
    return _mxu_const(np.concatenate([np.kron(eye, c), np.kron(eye, s)], axis=0))


def _fourier_latent(u, fm_bd):
    bsz, length, w = u.shape
    l2 = FFT_L2
    l1 = length // l2
    rows = FFT_ROWS
    c1, s1 = _dft_tables(l1, l1, l1)
    w1 = _mxu_const(np.kron(np.concatenate([c1, -s1], axis=0), np.eye(rows)))
    y = pl.pallas_call(
        _fft1_kernel,
        out_shape=jax.ShapeDtypeStruct((bsz, 2, l1, l2, w), F32),
        grid=(bsz, l2 // rows),
        in_specs=[pl.BlockSpec((None, l1, rows, w), lambda b, j: (b, 0, j, 0)),
                  pl.BlockSpec(w1.shape, lambda b, j: (0, 0))],
        out_specs=pl.BlockSpec((None, 2, l1, rows, w), lambda b, j: (b, 0, 0, j, 0)),
        compiler_params=_cparams(("arbitrary", "arbitrary")),
        name="fourier_stage1",
    )(u.reshape(bsz, l1, l2, w), w1)
    tabs = [_dft_tables(l2, l2, length, row_stride=l1, row_offset=k1) for k1 in range(l1)]
    ck = _mxu_const(np.stack([t[0] for t in tabs]))
    sk = _mxu_const(np.stack([t[1] for t in tabs]))
    const2 = lambda b, k: (0, 0)
    out = pl.pallas_call(
        _fft2_kernel,
        out_shape=jax.ShapeDtypeStruct((bsz, l2, l1, w), F32),
        grid=(bsz, l1 // rows),
        in_specs=[pl.BlockSpec((None, 2, rows, l2, w), lambda b, k: (b, 0, k, 0, 0)),
                  pl.BlockSpec((rows, l2, l2), lambda b, k: (k, 0, 0)),
                  pl.BlockSpec((rows, l2, l2), lambda b, k: (k, 0, 0)),
                  pl.BlockSpec((2 * w, w), const2), pl.BlockSpec((w, w), const2)],
        out_specs=pl.BlockSpec((None, l2, rows, w), lambda b, k: (b, 0, k, 0)),
        compiler_params=_cparams(("arbitrary", "arbitrary")),
        name="fourier_stage2",
    )(y, ck, sk, _channel_dft(w, length), fm_bd)
    return out.reshape(bsz, length, w)


def _fourier_direct(u, fm_bd):
    bsz, length, w = u.shape
    c, s = _dft_tables(length, length, length)
    cs = _mxu_const(np.concatenate([c, -s], axis=0))
    const2 = lambda b: (0, 0)
    return pl.pallas_call(
        _fft_direct_kernel,
        out_shape=jax.ShapeDtypeStruct((bsz, length, w), F32),
        grid=(bsz,),
        in_specs=[pl.BlockSpec((None, length, w), lambda b: (b, 0, 0)),
                  pl.BlockSpec((2 * length, length), const2),
                  pl.BlockSpec((2 * w, w), const2), pl.BlockSpec((w, w), const2)],
        out_specs=pl.BlockSpec((None, length, w), lambda b: (b, 0, 0)),
        compiler_params=_cparams(("arbitrary",)),
        name="fourier_ctx",
    )(u, cs, _channel_dft(w, length), fm_bd)


def _log_sigmoid(x):
    return jnp.minimum(x, 0.0) - jnp.log1p(jnp.exp(-jnp.abs(x)))


def _ret_kernel(*refs, need_ctx):
    (rdl_ref, rdh_ref, qf_ref, kf_ref, vf_ref, qb_ref, kb_ref, vb_ref, qc_ref, kc_ref, vc_ref) = refs[:11]
    if need_ctx:
        of_ref, ob_ref, oc_ref = refs[11:14]
        scr = refs[14:]
    else:
        of_ref, ob_ref = refs[11:13]
        oc_ref = None
        scr = refs[13:]
    sf_ref, sb_ref, din_ref, tab_ref = scr
    c = RET_CHUNK
    w = 4 * HEAD_DIM
    j = pl.program_id(1)
    head_shift = HEAD_DIM.bit_length() - 1
    lane_head = lax.broadcasted_iota(jnp.int32, (c, w), 1) >> head_shift
    blockdiag = ((lax.broadcasted_iota(jnp.int32, (w, w), 0) >> head_shift)
                 == (lax.broadcasted_iota(jnp.int32, (w, w), 1) >> head_shift))

    def chunk(q, k, v, d):
        qf = q.astype(F32)
        o = jnp.zeros((c, w), F32)
        for h in range(4):
            hm = lane_head == h
            qh = jnp.where(hm, qf, 0.0).astype(BF16)
            inner = (_dot_nt(qh, k) * din_ref[d, h]).astype(BF16)
            o = o + jnp.where(hm, jnp.dot(inner, v, preferred_element_type=F32), 0.0)
        kz = (k.astype(F32) * tab_ref[d, 1]).T.astype(BF16)
        kv = jnp.where(blockdiag, jnp.dot(kz, v, preferred_element_type=F32), 0.0)
        return o, kv

    @pl.when(j == 0)
    def _():
        t = lax.broadcasted_iota(jnp.int32, (c, w), 0).astype(F32)
        rr = lax.broadcasted_iota(jnp.int32, (c, c), 0)
        cc = lax.broadcasted_iota(jnp.int32, (c, c), 1)
        for d in range(2):
            lg = _log_sigmoid(rdl_ref[d])
            tab_ref[d, 0] = jnp.exp(lg * ((t + 1.0) if d == 0 else (c - t)))
            tab_ref[d, 1] = jnp.exp(lg * ((c - 1.0 - t) if d == 0 else t))
            tab_ref[d, 2] = jnp.exp(jnp.broadcast_to(lg, (c, w)) * float(c))
            diff = (rr - cc) if d == 0 else (cc - rr)
            dpos = jnp.maximum(diff, 0).astype(F32)
            for h in range(4):
                lgh = _log_sigmoid(rdh_ref[d, h])
                din_ref[d, h] = jnp.where(diff >= 0, jnp.exp(lgh * dpos), 0.0)
        q, k, v = qc_ref[...], kc_ref[...], vc_ref[...]
        o_f, kv_f = chunk(q, k, v, 0)
        o_b, kv_b = chunk(q, k, v, 1)
        sf_ref[...] = kv_f
        sb_ref[...] = kv_b
        if need_ctx:
            oc_ref[...] = o_f + o_b

    @pl.when(j > 0)
    def _():
        n_sub = qf_ref.shape[0] // c
        for d, (q_ref, k_ref, v_ref, s_ref, o_ref) in enumerate(
                ((qf_ref, kf_ref, vf_ref, sf_ref, of_ref), (qb_ref, kb_ref, vb_ref, sb_ref, ob_ref))):
            state = s_ref[...]
            for t in (range(n_sub) if d == 0 else reversed(range(n_sub))):
                rs = slice(t * c, (t + 1) * c)
                q, k, v = q_ref[rs, :], k_ref[rs, :], v_ref[rs, :]
                o, kv = chunk(q, k, v, d)
                o_ref[rs, :] = o + jnp.dot(q, state.astype(BF16), preferred_element_type=F32) * tab_ref[d, 0]
                state = state * tab_ref[d, 2, 0:1, :] + kv
            s_ref[...] = state


def _retention(p_lat, p_ctx, rdl, rdh, need_ctx):
    bsz, ll, _ = p_lat.shape
    lc = p_ctx.shape[1]
    c = RET_CHUNK
    w = 4 * HEAD_DIM
    cs = c * RET_STEP_CHUNKS
    assert lc == c and ll % cs == 0
    n = ll // cs
    fwd = lambda blk: (lambda b, j: (b, jnp.maximum(j - 1, 0), blk))
    bwd = lambda blk: (lambda b, j: (b, n - 1 - jnp.maximum(j - 1, 0), blk))
    ctx = lambda blk: (lambda b, j: (b, 0, blk))
    in_specs = [pl.BlockSpec((2, 1, w), lambda b, j: (0, 0, 0)),
                pl.BlockSpec((2, 4, 1, c), lambda b, j: (0, 0, 0, 0))]
    in_specs += [pl.BlockSpec((None, cs, w), fwd(blk)) for blk in (4, 5, 6)]
    in_specs += [pl.BlockSpec((None, cs, w), bwd(blk)) for blk in (4, 5, 6)]
    in_specs += [pl.BlockSpec((None, c, w), ctx(blk)) for blk in (4, 5, 6)]
    out_shape = [jax.ShapeDtypeStruct((bsz, ll, w), F32), jax.ShapeDtypeStruct((bsz, ll, w), F32)]
    out_specs = [pl.BlockSpec((None, cs, w), fwd(0)), pl.BlockSpec((None, cs, w), bwd(0))]
    if need_ctx:
        out_shape.append(jax.ShapeDtypeStruct((bsz, lc, w), F32))
        out_specs.append(pl.BlockSpec((None, c, w), ctx(0)))
    return pl.pallas_call(
        functools.partial(_ret_kernel, need_ctx=need_ctx),
        out_shape=tuple(out_shape),
        grid=(bsz, n + 1),
        in_specs=in_specs,
        out_specs=tuple(out_specs),
        scratch_shapes=[pltpu.VMEM((w, w), F32), pltpu.VMEM((w, w), F32),
                        pltpu.VMEM((2, 4, c, c), F32), pltpu.VMEM((2, 3, c, w), F32)],
        compiler_params=_cparams(("arbitrary", "arbitrary")),
        name="retention_ctx_out" if need_ctx else "retention",
    )(rdl, rdh, *([p_lat] * 6), *([p_ctx] * 3))


OUT_ROW_SLABS = 4
OUT_TILE = 1024


def _out_kernel(*refs, n_o, alpha):
    a_ref, b_ref, f_ref, g_ref = refs[:4]
    o_refs = refs[4:4 + n_o]
    x_ref, g1_ref, lnw_ref, lnb_ref, w_ref, gnw_ref, gavg_ref, out_ref = refs[4 + n_o:]
    gavg = gavg_ref[...]
    slabs = OUT_ROW_SLABS if x_ref.shape[0] % (16 * OUT_ROW_SLABS) == 0 else 1
    rows = x_ref.shape[0] // slabs
    for s in range(slabs):
        rs = slice(s * rows, (s + 1) * rows)
        o = o_refs[0][rs, :]
        for r in o_refs[1:]:
            o = o + r[rs, :]
        dlt = o - _group_mean(o, gavg)
        on = dlt * lax.rsqrt(_group_mean(dlt * dlt, gavg) + NORM_EPS) * gnw_ref[...]
        ret = (_silu(g_ref[rs, :]) * on).astype(BF16)
        cat = jnp.concatenate([a_ref[rs, :], b_ref[rs, :], f_ref[rs, :].astype(BF16), ret], axis=1)
        y = jnp.dot(cat, w_ref[...], preferred_element_type=F32)
        z = alpha * x_ref[rs, :] + g1_ref[...] * y
        out_ref[rs, :] = _ln(z) * lnw_ref[...] + lnb_ref[...]


def _out_proj(a, b, f, g, o_parts, x, mod, mod_row, lnw, lnb, w, layer, gnw, gavg, alpha, tm):
    bsz, length, d = x.shape
    row = lambda bb, i: (bb, i, 0)
    const2 = lambda bb, i: (0, 0)
    blk256 = pl.BlockSpec((None, tm, 256), row)
    in_specs = [blk256] * (4 + len(o_parts)) + [
        pl.BlockSpec((None, tm, d), row),
        pl.BlockSpec((None, None, 1, d), lambda bb, i: (mod_row(bb), 2, 0, 0)),
        pl.BlockSpec((1, d), const2), pl.BlockSpec((1, d), const2),
        pl.BlockSpec((None, d, d), lambda bb, i: (layer, 0, 0)),
        pl.BlockSpec((1, 256), const2), pl.BlockSpec((256, 256), const2)]
    return pl.pallas_call(
        functools.partial(_out_kernel, n_o=len(o_parts), alpha=alpha),
        out_shape=jax.ShapeDtypeStruct((bsz, length, d), F32),
        grid=(bsz, length // tm),
        in_specs=in_specs,
        out_specs=pl.BlockSpec((None, tm, d), row),
        compiler_params=_cparams(("arbitrary", "arbitrary")),
        name="out_proj",
    )(a, b, f, g, *o_parts, x, mod, lnw, lnb, w, gnw, gavg)


FFN_ROW_SLABS = 2


def _ffn_kernel(x_ref, sh_ref, sc_ref, g2_ref, lnw_ref, lnb_ref, wg_ref, wu_ref, wd_ref, out_ref, *, fc, alpha):
    rows = x_ref.shape[0] // FFN_ROW_SLABS
    n_chunks = wg_ref.shape[1] // fc
    work = [(r, c) for r in range(FFN_ROW_SLABS) for c in range(n_chunks)]
    xs, hs, accs = {}, {}, {}

    def gate_up(r, c):
        if r not in hs:
            xs[r] = x_ref[r * rows:(r + 1) * rows, :]
            hs[r] = (_ln(xs[r]) * (1.0 + sc_ref[...]) + sh_ref[...]).astype(BF16)
        cols = slice(c * fc, (c + 1) * fc)
        return (jnp.dot(hs[r], wg_ref[:, cols], preferred_element_type=F32),
                jnp.dot(hs[r], wu_ref[:, cols], preferred_element_type=F32))

    ahead = gate_up(*work[0])
    for idx, (r, c) in enumerate(work):
        gate, up = ahead
        if idx + 1 < len(work):
            ahead = gate_up(*work[idx + 1])
        act = (_silu(gate) * up).astype(BF16)
        down = jnp.dot(act, wd_ref[c * fc:(c + 1) * fc, :], preferred_element_type=F32)
        accs[r] = down if c == 0 else accs[r] + down
        if c == n_chunks - 1:
            z = alpha * xs[r] + g2_ref[...] * accs[r]
            out_ref[r * rows:(r + 1) * rows, :] = _ln(z) * lnw_ref[...] + lnb_ref[...]


def _ffn(x, mod, mod_row, lnw, lnb, wgu, wd, layer, alpha, tm):
    bsz, length, d = x.shape
    ff = wd.shape[1]
    fc = ff // 2 if (ff // 2) % LANES == 0 else ff
    row = lambda bb, i: (bb, i, 0)
    const2 = lambda bb, i: (0, 0)
    modspec = lambda which: pl.BlockSpec((None, None, 1, d), lambda bb, i: (mod_row(bb), which, 0, 0))
    resident = lambda shape, col: pl.BlockSpec(shape, lambda bb, i: (layer, 0, col), pipeline_mode=pl.Buffered(1))
    return pl.pallas_call(
        functools.partial(_ffn_kernel, fc=fc, alpha=alpha),
        out_shape=jax.ShapeDtypeStruct((bsz, length, d), F32),
        grid=(bsz, length // tm),
        in_specs=[pl.BlockSpec((None, tm, d), row), modspec(3), modspec(4), modspec(5),
                  pl.BlockSpec((1, d), const2), pl.BlockSpec((1, d), const2),
                  resident((None, d, ff), 0), resident((None, d, ff), 1), resident((None, ff, d), 0)],
        out_specs=pl.BlockSpec((None, tm, d), row),
        compiler_params=_cparams(("arbitrary", "arbitrary")),
        name="ffn",
    )(x, mod, mod, mod, lnw, lnb, wgu, wgu, wd)


def _rope_tables(seq):
    t = np.arange(seq)
    f32 = np.float32

    def tab(pos, n_freq):
        inv = f32(ROPE_THETA) ** (-np.arange(n_freq, dtype=f32) / f32(n_freq))
        ang = (pos.astype(f32)[:, None] * inv[None, :]).astype(np.float64)
        return np.cos(ang), np.sin(ang)

    cr, sr = tab(t // GRID_W, HEAD_DIM // 4)
    cc, sc = tab(t % GRID_W, HEAD_DIM // 4)
    ct, st = tab(t, HEAD_DIM // 2)
    tables = (np.concatenate([cr, cr, cc, cc], -1), np.concatenate([-sr, sr, -sc, sc], -1),
              np.concatenate([ct, ct], -1), np.concatenate([-st, st], -1))
    return tuple(jnp.asarray(np.tile(a, (1, 2)), F32) for a in tables)


def kernel(x, c, ctx, c_ctx, w_mod, b_mod, w_in, a_q_norm, a_k_norm, b_sink, f_mix, r_decay, r_gn_w, w_out,
           ln1_w, ln1_b, w_gate_up, w_down, ln2_w, ln2_b):
    bsz, seq, d = x.shape
    depth = w_in.shape[0]
    ff = w_down.shape[1]
    gw = d // 4
    assert gw == 4 * HEAD_DIM and a_q_norm.shape[-1] == HEAD_DIM and seq % (FFT_L2 * 8) == 0
    alpha = (2.0 * depth) ** 0.25

    tabs = _rope_tables(seq)
    gavg = jnp.asarray(np.kron(np.eye(gw // HEAD_DIM), np.full((HEAD_DIM, HEAD_DIM), 1.0 / HEAD_DIM)), BF16)
    cc = jnp.zeros((8, d), F32).at[:bsz].set(c).at[bsz].set(c_ctx)
    mod_all = _modulation(cc, w_mod, b_mod).reshape(depth, 8, 6, 1, d)
    lat_row = lambda b: b
    ctx_row = lambda b: bsz

    eye_g = jnp.eye(gw // HEAD_DIM, dtype=F32)
    w_in_b, w_out_b = w_in.astype(BF16), w_out.astype(BF16)
    w_gu_b, w_dn_b = w_gate_up.astype(BF16), w_down.astype(BF16)

    tm = 512
    for layer in range(depth):
        need_ctx = layer < depth - 1
        mod = mod_all[layer]
        qn = jnp.tile(a_q_norm[layer], 4)[None, :]
        kn = jnp.tile(a_k_norm[layer], 2)[None, :]
        gnw = r_gn_w[layer][None, :]
        lnw1, lnb1 = ln1_w[layer][None, :], ln1_b[layer][None, :]
        lnw2, lnb2 = ln2_w[layer][None, :], ln2_b[layer][None, :]
        fm_bd = jnp.einsum('gh,gce->gche', eye_g, f_mix[layer]).reshape(gw, gw).astype(BF16)
        rd = r_decay[layer]
        rdl = jnp.repeat(rd, HEAD_DIM, axis=1)[:, None, :]
        rdh = jnp.broadcast_to(rd[:, :, None, None], (2, 4, 1, RET_CHUNK))
        sink = b_sink[layer]

        p_l, u_l, g_l = _in_proj(x, mod, lat_row, w_in_b, layer, qn, kn, gavg, tabs, min(IN_TILE, seq))
        p_c, u_c, g_c = _in_proj(ctx, mod, ctx_row, w_in_b, layer, qn, kn, gavg, None, ctx.shape[1])

        a_l = _attention(p_l, 0, p_l, p_c, 2, 3, None, tq=2 * Q_BLOCK, tk=min(2048, seq))
        b_l = _window_attention(p_l, p_c, sink)
        f_l = _fourier_latent(u_l, fm_bd)
        r_out = _retention(p_l, p_c, rdl, rdh, need_ctx)
        x = _out_proj(a_l, b_l, f_l, g_l, r_out[:2], x, mod, lat_row, lnw1, lnb1, w_out_b, layer, gnw, gavg, alpha,
                      min(OUT_TILE, seq))
        x = _ffn(x, mod, lat_row, lnw2, lnb2, w_gu_b, w_dn_b, layer, alpha, tm)
        if need_ctx:
            lc = ctx.shape[1]
            a_c = _attention(p_c, 0, None, p_c, 2, 3, None, tq=Q_BLOCK, tk=2048)
            b_c = _attention(p_c, 2, None, p_c, 6, 7, sink, tq=Q_BLOCK, tk=2048)
            f_c = _fourier_direct(u_c, fm_bd)
            ctx = _out_proj(a_c, b_c, f_c, g_c, r_out[2:], ctx, mod, ctx_row, lnw1, lnb1, w_out_b, layer, gnw, gavg,
                            alpha, lc)
            ctx = _ffn(ctx, mod, ctx_row, lnw2, lnb2, w_gu_b, w_dn_b, layer, alpha, lc)
    return x
```

```python
import functools

import numpy as np
import jax
import jax.numpy as jnp
from jax import lax
from jax.experimental import pallas as pl
from jax.experimental.pallas import tpu as pltpu

F32 = jnp.float32
BF16 = jnp.bfloat16
HIGHEST = lax.Precision.HIGHEST

HEAD_DIM = 64
GRID_W = 64
Q_BLOCK = 128
ROPE_THETA = 10000.0
NORM_EPS = 1e-6
NEG_INF = -1e30
LOG2E = 1.4426950408889634

LANES = 128
VMEM_LIMIT_BYTES = 56 * 1024 * 1024

RET_CHUNK = 256
RET_STEP_CHUNKS = 4
FFT_L2 = 128


def _cparams(sem):
    return pltpu.CompilerParams(dimension_semantics=sem, vmem_limit_bytes=VMEM_LIMIT_BYTES)


def _ln(x):
    mu = jnp.mean(x, axis=-1, keepdims=True)
    xc = x - mu
    var = jnp.mean(xc * xc, axis=-1, keepdims=True)
    return xc * lax.rsqrt(var + NORM_EPS)


def _silu(x):
    return x * jax.nn.sigmoid(x)


def _group_mean(t, g):
    hi = t.astype(BF16)
    lo = (t - hi.astype(F32)).astype(BF16)
    return (jnp.dot(hi, g, preferred_element_type=F32) + jnp.dot(lo, g, preferred_element_type=F32))


def _dot_nt(a, b):
    return lax.dot_general(a, b, (((1,), (1,)), ((), ())), preferred_element_type=F32)


def _mod_kernel(c_ref, w_ref, b_ref, o_ref):
    h = _silu(c_ref[...])
    o_ref[...] = jnp.dot(h, w_ref[...], precision=HIGHEST, preferred_element_type=F32) + b_ref[...]


def _modulation(cc, w_mod, b_mod):
    depth, d, n = w_mod.shape
    tn = 2048
    return pl.pallas_call(
        _mod_kernel,
        out_shape=jax.ShapeDtypeStruct((depth, 8, n), F32),
        grid=(depth, n // tn),
        in_specs=[pl.BlockSpec((8, d), lambda l, j: (0, 0)),
                  pl.BlockSpec((None, d, tn), lambda l, j: (l, 0, j)),
                  pl.BlockSpec((None, 1, tn), lambda l, j: (l, 0, j))],
        out_specs=pl.BlockSpec((None, 8, tn), lambda l, j: (l, 0, j)),
        compiler_params=_cparams(("arbitrary", "arbitrary")),
        name="modulation",
    )(cc, w_mod, b_mod.reshape(depth, 1, n))


P_COLS = 14 * LANES


def _rope_lanes(t, c, ss, half):
    first = (lax.broadcasted_iota(jnp.int32, (t.shape[0], LANES), 1) & half) == 0
    outs = []
    for j in range(t.shape[1] // LANES):
        tj = t[:, j * LANES:(j + 1) * LANES]
        partner = jnp.where(first, pltpu.roll(tj, LANES - half, 1), pltpu.roll(tj, half, 1))
        outs.append(tj * c + partner * ss)
    return outs[0] if len(outs) == 1 else jnp.concatenate(outs, axis=1)


IN_ROW_SLABS = 4
IN_TILE = 1024


def _pair_heads_by_kv(q):
    a, b = q[:, 0:LANES], q[:, LANES:2 * LANES]
    lo = lax.broadcasted_iota(jnp.int32, a.shape, 1) < HEAD_DIM
    return jnp.concatenate([jnp.where(lo, a, pltpu.roll(b, HEAD_DIM, 1)),
                            jnp.where(lo, pltpu.roll(a, HEAD_DIM, 1), b)], axis=1)


def _in_kernel(*refs, rope):
    x_ref, sh_ref, sc_ref, w_ref, qn_ref, kn_ref, gavg_ref = refs[:7]
    if rope:
        c2_ref, ss2_ref, c1_ref, ss1_ref, p_ref, u_ref, g_ref = refs[7:]
    else:
        p_ref, u_ref, g_ref = refs[7:]
    def rms(t, w, g):
        return t * lax.rsqrt(_group_mean(t * t, g) + NORM_EPS) * w

    scale = HEAD_DIM ** -0.5
    qscale = scale * LOG2E
    slabs = IN_ROW_SLABS if x_ref.shape[0] % (8 * IN_ROW_SLABS) == 0 else 1
    rows = x_ref.shape[0] // slabs
    for r in range(slabs):
        rs = slice(r * rows, (r + 1) * rows)

        def rope2(t):
            if not rope:
                return t
            return _rope_lanes(t, c2_ref[rs, :], ss2_ref[rs, :], HEAD_DIM // 4)

        def rope1(t):
            if not rope:
                return t
            return _rope_lanes(t, c1_ref[rs, :], ss1_ref[rs, :], HEAD_DIM // 2)

        h = _ln(x_ref[rs, :]) * (1.0 + sc_ref[...]) + sh_ref[...]
        y = jnp.dot(h.astype(BF16), w_ref[...], preferred_element_type=F32)
        qa = rope2(rms(y[:, 0:256], qn_ref[...], gavg_ref[...])) * qscale
        ka = rope2(rms(y[:, 256:384], kn_ref[...], gavg_ref[0:LANES, 0:LANES]))
        p_ref[rs, 0:256] = _pair_heads_by_kv(qa).astype(BF16)
        p_ref[rs, 256:384] = ka.astype(BF16)
        p_ref[rs, 384:512] = y[:, 384:512].astype(BF16)
        p_ref[rs, 512:768] = _pair_heads_by_kv(rope2(y[:, 512:768]) * qscale).astype(BF16)
        p_ref[rs, 768:896] = rope2(y[:, 768:896]).astype(BF16)
        p_ref[rs, 896:1024] = y[:, 896:1024].astype(BF16)
        u_ref[rs, :] = y[:, 1024:1280]
        p_ref[rs, 1024:1280] = rope1(y[:, 1280:1536]).astype(BF16)
        p_ref[rs, 1280:1536] = (rope1(y[:, 1536:1792]) * scale).astype(BF16)
        p_ref[rs, 1536:1792] = y[:, 1792:2048].astype(BF16)
        g_ref[rs, :] = y[:, 2048:2304]


def _in_proj(x, mod, mod_row, w, layer, qn, kn, gavg, tabs, tm):
    bsz, length, d = x.shape
    nw = w.shape[2]
    nt = length // tm
    rope = tabs is not None
    row = lambda b, i: (b, i, 0)
    const2 = lambda b, i: (0, 0)
    in_specs = [pl.BlockSpec((None, tm, d), row),
                pl.BlockSpec((None, None, 1, d), lambda b, i: (mod_row(b), 0, 0, 0)),
                pl.BlockSpec((None, None, 1, d), lambda b, i: (mod_row(b), 1, 0, 0)),
                pl.BlockSpec((None, d, nw), lambda b, i: (layer, 0, 0)),
                pl.BlockSpec((1, 256), const2),
                pl.BlockSpec((1, LANES), const2),
                pl.BlockSpec((256, 256), const2)]
    args = [x, mod, mod, w, qn, kn, gavg]
    if rope:
        in_specs += [pl.BlockSpec((tm, LANES), lambda b, i: (i, 0))] * len(tabs)
        args += list(tabs)
    return pl.pallas_call(
        functools.partial(_in_kernel, rope=rope),
        out_shape=(jax.ShapeDtypeStruct((bsz, length, P_COLS), BF16),
                   jax.ShapeDtypeStruct((bsz, length, 256), F32),
                   jax.ShapeDtypeStruct((bsz, length, 256), F32)),
        grid=(bsz, nt),
        in_specs=in_specs,
        out_specs=(pl.BlockSpec((None, tm, P_COLS), row),
                   pl.BlockSpec((None, tm, 256), row),
                   pl.BlockSpec((None, tm, 256), row)),
        compiler_params=_cparams(("arbitrary", "arbitrary")),
        name="in_proj_rope" if rope else "in_proj_ctx",
    )(*args)


def _stack_heads(q):
    qf = q.astype(F32)
    lo = lax.broadcasted_iota(jnp.int32, (q.shape[0], LANES), 1) < HEAD_DIM
    q0, q1 = qf[:, 0:LANES], qf[:, LANES:2 * LANES]
    z = jnp.zeros_like(q0)
    return jnp.concatenate([jnp.where(lo, q0, z), jnp.where(lo, q1, z),
                            jnp.where(lo, z, q0), jnp.where(lo, z, q1)], axis=0).astype(BF16)


def _aug_values(v):
    vf = v.astype(F32)
    lo = lax.broadcasted_iota(jnp.int32, vf.shape, 1) < HEAD_DIM
    one = jnp.ones_like(vf)
    return jnp.where(lo, vf, one).astype(BF16), jnp.where(lo, one, vf).astype(BF16)


def _finish_heads(acc0, acc1, e, tq):
    l0 = pltpu.roll(acc0, HEAD_DIM, 1)
    l1 = pltpu.roll(acc1, HEAD_DIM, 1)
    if e is not None:
        l0 = l0 + e[:2 * tq]
        l1 = l1 + e[2 * tq:]
    n0 = acc0 / l0
    n1 = acc1 / l1
    lo = lax.broadcasted_iota(jnp.int32, (tq, LANES), 1) < HEAD_DIM
    return jnp.concatenate([jnp.where(lo, n0[:tq], pltpu.roll(n0[tq:], HEAD_DIM, 1)),
                            jnp.where(lo, pltpu.roll(n1[:tq], HEAD_DIM, 1), n1[tq:])], axis=1)


def _sink_column(sink_ref, tq):
    return jnp.concatenate([jnp.full((tq, 1), sink_ref[h] * LOG2E, F32) for h in range(4)], axis=0)


ATTN_Q_TILE = 256
ATTN_K_CHUNK = 2048

def _attn_kernel(*refs, tq, tk, n_lat, has_sink):
    i = 0
    sink_ref = None
    if has_sink:
        sink_ref = refs[0]
        i = 1
    q_ref = refs[i]
    i += 1
    if n_lat:
        kl_ref, vl_ref = refs[i:i + 2]
        i += 2
    kc_ref, vc_ref, o_ref = refs[i:i + 3]
    i += 3
    if n_lat:
        v0l_ref, v1l_ref = refs[i:i + 2]
        i += 2
    v0c_ref, v1c_ref = refs[i:i + 2]

    @pl.when(pl.program_id(1) == 0)
    def _():
        if n_lat:
            a0, a1 = _aug_values(vl_ref[...])
            v0l_ref[...] = a0
            v1l_ref[...] = a1
        a0, a1 = _aug_values(vc_ref[...])
        v0c_ref[...] = a0
        v1c_ref[...] = a1

    qs = _stack_heads(q_ref[...])
    half = 2 * tq

    chunks = [(kl_ref, v0l_ref, v1l_ref, slice(c * tk, (c + 1) * tk)) for c in range(n_lat)]
    chunks.append((kc_ref, v0c_ref, v1c_ref, slice(None)))

    def scores(chunk):
        k_ref, _, _, rows = chunk
        return _dot_nt(qs, k_ref[rows, :])

    m = _sink_column(sink_ref, tq) if has_sink else jnp.full((4 * tq, 1), NEG_INF, F32)
    acc0 = jnp.zeros((half, LANES), F32)
    acc1 = jnp.zeros((half, LANES), F32)
    s_next = scores(chunks[0])
    for idx, (_, v0_ref, v1_ref, rows) in enumerate(chunks):
        s = s_next
        if idx + 1 < len(chunks):
            s_next = scores(chunks[idx + 1])
        m_new = jnp.maximum(m, jnp.max(s, axis=1, keepdims=True))
        alpha = jnp.exp2(m - m_new)
        p = jnp.exp2(s - m_new).astype(BF16)
        acc0 = acc0 * alpha[:half] + jnp.dot(p[:half], v0_ref[rows, :], preferred_element_type=F32)
        acc1 = acc1 * alpha[half:] + jnp.dot(p[half:], v1_ref[rows, :], preferred_element_type=F32)
        m = m_new
    e = jnp.exp2(_sink_column(sink_ref, tq) - m) if has_sink else None
    o_ref[...] = _finish_heads(acc0, acc1, e, tq).astype(BF16)


def _attention(pq, q_blk, p_lat, p_ctx, k_blk, v_blk, sink, tq, tk):
    bsz, lq, _ = pq.shape
    lc = p_ctx.shape[1]
    assert p_lat is None or p_lat.shape[1] % tk == 0
    n_lat = 0 if p_lat is None else p_lat.shape[1] // tk
    has_sink = sink is not None
    in_specs = [pl.BlockSpec((None, tq, 256), lambda b, i, *_: (b, i, q_blk))]
    args = [pq]
    scratch = []
    if n_lat:
        ll = p_lat.shape[1]
        in_specs += [pl.BlockSpec((None, ll, LANES), lambda b, i, *_: (b, 0, k_blk)),
                     pl.BlockSpec((None, ll, LANES), lambda b, i, *_: (b, 0, v_blk))]
        args += [p_lat, p_lat]
        scratch += [pltpu.VMEM((ll, LANES), BF16), pltpu.VMEM((ll, LANES), BF16)]
    in_specs += [pl.BlockSpec((None, lc, LANES), lambda b, i, *_: (b, 0, k_blk)),
                 pl.BlockSpec((None, lc, LANES), lambda b, i, *_: (b, 0, v_blk))]
    args += [p_ctx, p_ctx]
    scratch += [pltpu.VMEM((lc, LANES), BF16), pltpu.VMEM((lc, LANES), BF16)]
    kern = functools.partial(_attn_kernel, tq=tq, tk=tk, n_lat=n_lat, has_sink=has_sink)
    grid_spec = pltpu.PrefetchScalarGridSpec(
        num_scalar_prefetch=1 if has_sink else 0,
        grid=(bsz, lq // tq),
        in_specs=in_specs,
        out_specs=pl.BlockSpec((None, tq, 256), lambda b, i, *_: (b, i, 0)),
        scratch_shapes=scratch)
    call = pl.pallas_call(
        kern, out_shape=jax.ShapeDtypeStruct((bsz, lq, 256), BF16), grid_spec=grid_spec,
        compiler_params=_cparams(("arbitrary", "arbitrary")),
        name="attn_sink" if has_sink else ("attn_global" if n_lat else "attn_ctx"))
    return call(sink, *args) if has_sink else call(*args)


WIN_BLOCKS_PER_STEP = 4


def _win_kernel(sink_ref, q_ref, kl_ref, vl_ref, kc_ref, vc_ref, o_ref, *, nb):
    tq = Q_BLOCK
    r = lax.broadcasted_iota(jnp.int32, (4 * tq, tq), 0) & (tq - 1)
    j = lax.broadcasted_iota(jnp.int32, (4 * tq, tq), 1)
    in_prev = j >= r
    in_next = j <= r
    snk = _sink_column(sink_ref, tq)
    kc, vc = kc_ref[...], vc_ref[...]

    def rows(ref, blk):
        return ref[pl.ds(pl.multiple_of(blk * tq, tq), tq), :]

    for t in range(WIN_BLOCKS_PER_STEP):
        i = pl.program_id(1) * WIN_BLOCKS_PER_STEP + t
        prev = jnp.maximum(i - 1, 0)
        nxt = jnp.minimum(i + 1, nb - 1)
        k = jnp.concatenate([rows(kl_ref, prev), rows(kl_ref, i), rows(kl_ref, nxt), kc], axis=0)
        v = jnp.concatenate([rows(vl_ref, prev), rows(vl_ref, i), rows(vl_ref, nxt), vc], axis=0)
        qs = _stack_heads(q_ref[t * tq:(t + 1) * tq, :])
        s = _dot_nt(qs, k)
        off_prev = jnp.where(i > 0, 0.0, NEG_INF)
        off_next = jnp.where(i < nb - 1, 0.0, NEG_INF)
        s = jnp.concatenate([jnp.where(in_prev, s[:, 0:tq] + off_prev, NEG_INF), s[:, tq:2 * tq],
                             jnp.where(in_next, s[:, 2 * tq:3 * tq] + off_next, NEG_INF), s[:, 3 * tq:]], axis=1)
        m = jnp.maximum(jnp.max(s, axis=1, keepdims=True), snk)
        p = jnp.exp2(s - m).astype(BF16)
        v0, v1 = _aug_values(v)
        acc0 = jnp.dot(p[:2 * tq], v0, preferred_element_type=F32)
        acc1 = jnp.dot(p[2 * tq:], v1, preferred_element_type=F32)
        o_ref[t * tq:(t + 1) * tq, :] = _finish_heads(acc0, acc1, jnp.exp2(snk - m), tq).astype(BF16)


def _window_attention(p_lat, p_ctx, sink):
    bsz, ll, _ = p_lat.shape
    lc = p_ctx.shape[1]
    nb = ll // Q_BLOCK
    tqs = WIN_BLOCKS_PER_STEP * Q_BLOCK
    assert ll % tqs == 0
    grid_spec = pltpu.PrefetchScalarGridSpec(
        num_scalar_prefetch=1,
        grid=(bsz, ll // tqs),
        in_specs=[pl.BlockSpec((None, tqs, 256), lambda b, i, *_: (b, i, 2)),
                  pl.BlockSpec((None, ll, LANES), lambda b, i, *_: (b, 0, 6)),
                  pl.BlockSpec((None, ll, LANES), lambda b, i, *_: (b, 0, 7)),
                  pl.BlockSpec((None, lc, LANES), lambda b, i, *_: (b, 0, 6)),
                  pl.BlockSpec((None, lc, LANES), lambda b, i, *_: (b, 0, 7))],
        out_specs=pl.BlockSpec((None, tqs, 256), lambda b, i, *_: (b, i, 0)))
    return pl.pallas_call(
        functools.partial(_win_kernel, nb=nb),
        out_shape=jax.ShapeDtypeStruct((bsz, ll, 256), BF16), grid_spec=grid_spec,
        compiler_params=_cparams(("arbitrary", "arbitrary")),
        name="attn_window",
    )(sink, p_lat, p_lat, p_lat, p_ctx, p_ctx)


FFT_ROWS = 8
FFT1_STEP_ROWS = 16


def _fft1_kernel(u_ref, w_ref, y_ref):
    l1, rows, w = u_ref.shape
    ys = []
    for h in range(rows // FFT_ROWS):
        u = u_ref[:, h * FFT_ROWS:(h + 1) * FFT_ROWS, :].reshape(l1 * FFT_ROWS, w).astype(BF16)
        ys.append(jnp.dot(w_ref[...], u, preferred_element_type=F32).reshape(2, l1, FFT_ROWS, w))
    y_ref[...] = jnp.concatenate(ys, axis=2).astype(BF16)


def _channel_mix(ab, g_ref, fm_ref):
    z = jnp.dot(ab.astype(BF16), g_ref[...], preferred_element_type=F32)
    return jnp.dot(z.astype(BF16), fm_ref[...], preferred_element_type=F32)


def _fft2_kernel(y_ref, c_ref, s_ref, g_ref, fm_ref, o_ref):
    l2 = y_ref.shape[2]
    ab = []
    for r in range(FFT_ROWS):
        yr, yi = y_ref[0, r], y_ref[1, r]
        cs = jnp.concatenate([c_ref[r], s_ref[r]], axis=1)
        rhs = jnp.concatenate([jnp.concatenate([yr, yi], axis=1),
                               jnp.concatenate([yi, -yr], axis=1)], axis=0)
        ab.append(jnp.dot(cs, rhs, preferred_element_type=F32))
    o = _channel_mix(jnp.concatenate(ab, axis=0), g_ref, fm_ref)
    for r in range(FFT_ROWS):
        o_ref[:, r, :] = o[r * l2:(r + 1) * l2]


def _fft_direct_kernel(u_ref, cs_ref, g_ref, fm_ref, o_ref):
    n = u_ref.shape[0]
    y = jnp.dot(cs_ref[...], u_ref[...].astype(BF16), preferred_element_type=F32)
    o_ref[...] = _channel_mix(jnp.concatenate([y[:n], y[n:]], axis=1), g_ref, fm_ref)


def _mxu_const(a):
    return jnp.asarray(a, F32).astype(BF16)


def _dft_tables(n_rows, n_cols, length, row_stride=1, row_offset=0):
    k = row_offset + row_stride * np.arange(n_rows, dtype=np.int64)
    n = np.arange(n_cols, dtype=np.int64)
    ang = 2.0 * np.pi * ((k[:, None] * n[None, :]) % length).astype(np.float64) / length
    return np.cos(ang), np.sin(ang)


def _channel_dft(width, length):
    c, s = _dft_tables(HEAD_DIM, HEAD_DIM, HEAD_DIM)
    eye = np.eye(width // HEAD_DIM) / np.sqrt(float(length) * HEAD_DIM)
    return _mxu_const(np.concatenate([np.kron(eye, c), np.kron(eye, s)], axis=0))


def _fourier_latent(u, fm_bd):
    bsz, length, w = u.shape
    l2 = FFT_L2
    l1 = length // l2
    rows = FFT_ROWS
    c1, s1 = _dft_tables(l1, l1, l1)
    w1 = _mxu_const(np.kron(np.concatenate([c1, -s1], axis=0), np.eye(rows)))
    y = pl.pallas_call(
        _fft1_kernel,
        out_shape=jax.ShapeDtypeStruct((bsz, 2, l1, l2, w), BF16),
        grid=(bsz, l2 // FFT1_STEP_ROWS),
        in_specs=[pl.BlockSpec((None, l1, FFT1_STEP_ROWS, w), lambda b, j: (b, 0, j, 0)),
                  pl.BlockSpec(w1.shape, lambda b, j: (0, 0))],
        out_specs=pl.BlockSpec((None, 2, l1, FFT1_STEP_ROWS, w), lambda b, j: (b, 0, 0, j, 0)),
        compiler_params=_cparams(("arbitrary", "arbitrary")),
        name="fourier_stage1",
    )(u.reshape(bsz, l1, l2, w), w1)
    tabs = [_dft_tables(l2, l2, length, row_stride=l1, row_offset=k1) for k1 in range(l1)]
    ck = _mxu_const(np.stack([t[0] for t in tabs]))
    sk = _mxu_const(np.stack([t[1] for t in tabs]))
    const2 = lambda b, k: (0, 0)
    out = pl.pallas_call(
        _fft2_kernel,
        out_shape=jax.ShapeDtypeStruct((bsz, l2, l1, w), F32),
        grid=(bsz, l1 // rows),
        in_specs=[pl.BlockSpec((None, 2, rows, l2, w), lambda b, k: (b, 0, k, 0, 0)),
                  pl.BlockSpec((rows, l2, l2), lambda b, k: (k, 0, 0)),
                  pl.BlockSpec((rows, l2, l2), lambda b, k: (k, 0, 0)),
                  pl.BlockSpec((2 * w, w), const2), pl.BlockSpec((w, w), const2)],
        out_specs=pl.BlockSpec((None, l2, rows, w), lambda b, k: (b, 0, k, 0)),
        compiler_params=_cparams(("arbitrary", "arbitrary")),
        name="fourier_stage2",
    )(y, ck, sk, _channel_dft(w, length), fm_bd)
    return out.reshape(bsz, length, w)


def _fourier_direct(u, fm_bd):
    bsz, length, w = u.shape
    c, s = _dft_tables(length, length, length)
    cs = _mxu_const(np.concatenate([c, -s], axis=0))
    const2 = lambda b: (0, 0)
    return pl.pallas_call(
        _fft_direct_kernel,
        out_shape=jax.ShapeDtypeStruct((bsz, length, w), F32),
        grid=(bsz,),
        in_specs=[pl.BlockSpec((None, length, w), lambda b: (b, 0, 0)),
                  pl.BlockSpec((2 * length, length), const2),
                  pl.BlockSpec((2 * w, w), const2), pl.BlockSpec((w, w), const2)],
        out_specs=pl.BlockSpec((None, length, w), lambda b: (b, 0, 0)),
        compiler_params=_cparams(("arbitrary",)),
        name="fourier_ctx",
    )(u, cs, _channel_dft(w, length), fm_bd)


def _log_sigmoid(x):
    return jnp.minimum(x, 0.0) - jnp.log1p(jnp.exp(-jnp.abs(x)))


def _ret_kernel(*refs, need_ctx):
    (rdl_ref, rdh_ref, qf_ref, kf_ref, vf_ref, qb_ref, kb_ref, vb_ref, qc_ref, kc_ref, vc_ref) = refs[:11]
    if need_ctx:
        of_ref, ob_ref, oc_ref = refs[11:14]
        scr = refs[14:]
    else:
        of_ref, ob_ref = refs[11:13]
        oc_ref = None
        scr = refs[13:]
    sf_ref, sb_ref, din_ref, tab_ref = scr
    c = RET_CHUNK
    w = 4 * HEAD_DIM
    j = pl.program_id(1)
    head_shift = HEAD_DIM.bit_length() - 1
    lane_head = lax.broadcasted_iota(jnp.int32, (c, w), 1) >> head_shift
    blockdiag = ((lax.broadcasted_iota(jnp.int32, (w, w), 0) >> head_shift)
                 == (lax.broadcasted_iota(jnp.int32, (w, w), 1) >> head_shift))

    def chunk(q, k, v, d):
        qf = q.astype(F32)
        o = jnp.zeros((c, w), F32)
        for h in range(4):
            hm = lane_head == h
            qh = jnp.where(hm, qf, 0.0).astype(BF16)
            inner = (_dot_nt(qh, k) * din_ref[d, h]).astype(BF16)
            o = o + jnp.where(hm, jnp.dot(inner, v, preferred_element_type=F32), 0.0)
        kz = (k.astype(F32) * tab_ref[d, 1]).T.astype(BF16)
        kv = jnp.where(blockdiag, jnp.dot(kz, v, preferred_element_type=F32), 0.0)
        return o, kv

    @pl.when(j == 0)
    def _():
        t = lax.broadcasted_iota(jnp.int32, (c, w), 0).astype(F32)
        rr = lax.broadcasted_iota(jnp.int32, (c, c), 0)
        cc = lax.broadcasted_iota(jnp.int32, (c, c), 1)
        for d in range(2):
            lg = _log_sigmoid(rdl_ref[d])
            tab_ref[d, 0] = jnp.exp(lg * ((t + 1.0) if d == 0 else (c - t)))
            tab_ref[d, 1] = jnp.exp(lg * ((c - 1.0 - t) if d == 0 else t))
            tab_ref[d, 2] = jnp.exp(jnp.broadcast_to(lg, (c, w)) * float(c))
            diff = (rr - cc) if d == 0 else (cc - rr)
            dpos = jnp.maximum(diff, 0).astype(F32)
            for h in range(4):
                lgh = _log_sigmoid(rdh_ref[d, h])
                din_ref[d, h] = jnp.where(diff >= 0, jnp.exp(lgh * dpos), 0.0)
        q, k, v = qc_ref[...], kc_ref[...], vc_ref[...]
        o_f, kv_f = chunk(q, k, v, 0)
        o_b, kv_b = chunk(q, k, v, 1)
        sf_ref[...] = kv_f
        sb_ref[...] = kv_b
        if need_ctx:
            oc_ref[...] = o_f + o_b

    @pl.when(j > 0)
    def _():
        n_sub = qf_ref.shape[0] // c
        for d, (q_ref, k_ref, v_ref, s_ref, o_ref) in enumerate(
                ((qf_ref, kf_ref, vf_ref, sf_ref, of_ref), (qb_ref, kb_ref, vb_ref, sb_ref, ob_ref))):
            state = s_ref[...]
            for t in (range(n_sub) if d == 0 else reversed(range(n_sub))):
                rs = slice(t * c, (t + 1) * c)
                q, k, v = q_ref[rs, :], k_ref[rs, :], v_ref[rs, :]
                o, kv = chunk(q, k, v, d)
                o_ref[rs, :] = o + jnp.dot(q, state.astype(BF16), preferred_element_type=F32) * tab_ref[d, 0]
                state = state * tab_ref[d, 2, 0:1, :] + kv
            s_ref[...] = state


def _retention(p_lat, p_ctx, rdl, rdh, need_ctx):
    bsz, ll, _ = p_lat.shape
    lc = p_ctx.shape[1]
    c = RET_CHUNK
    w = 4 * HEAD_DIM
    cs = c * RET_STEP_CHUNKS
    assert lc == c and ll % cs == 0
    n = ll // cs
    fwd = lambda blk: (lambda b, j: (b, jnp.maximum(j - 1, 0), blk))
    bwd = lambda blk: (lambda b, j: (b, n - 1 - jnp.maximum(j - 1, 0), blk))
    ctx = lambda blk: (lambda b, j: (b, 0, blk))
    in_specs = [pl.BlockSpec((2, 1, w), lambda b, j: (0, 0, 0)),
                pl.BlockSpec((2, 4, 1, c), lambda b, j: (0, 0, 0, 0))]
    in_specs += [pl.BlockSpec((None, cs, w), fwd(blk)) for blk in (4, 5, 6)]
    in_specs += [pl.BlockSpec((None, cs, w), bwd(blk)) for blk in (4, 5, 6)]
    in_specs += [pl.BlockSpec((None, c, w), ctx(blk)) for blk in (4, 5, 6)]
    out_shape = [jax.ShapeDtypeStruct((bsz, ll, w), F32), jax.ShapeDtypeStruct((bsz, ll, w), F32)]
    out_specs = [pl.BlockSpec((None, cs, w), fwd(0)), pl.BlockSpec((None, cs, w), bwd(0))]
    if need_ctx:
        out_shape.append(jax.ShapeDtypeStruct((bsz, lc, w), F32))
        out_specs.append(pl.BlockSpec((None, c, w), ctx(0)))
    return pl.pallas_call(
        functools.partial(_ret_kernel, need_ctx=need_ctx),
        out_shape=tuple(out_shape),
        grid=(bsz, n + 1),
        in_specs=in_specs,
        out_specs=tuple(out_specs),
        scratch_shapes=[pltpu.VMEM((w, w), F32), pltpu.VMEM((w, w), F32),
                        pltpu.VMEM((2, 4, c, c), F32), pltpu.VMEM((2, 3, c, w), F32)],
        compiler_params=_cparams(("arbitrary", "arbitrary")),
        name="retention_ctx_out" if need_ctx else "retention",
    )(rdl, rdh, *([p_lat] * 6), *([p_ctx] * 3))


OUT_ROW_SLABS = 4
OUT_TILE = 1024


def _out_kernel(*refs, n_o, alpha):
    a_ref, b_ref, f_ref, g_ref = refs[:4]
    o_refs = refs[4:4 + n_o]
    x_ref, g1_ref, lnw_ref, lnb_ref, w_ref, gnw_ref, gavg_ref, out_ref = refs[4 + n_o:]
    gavg = gavg_ref[...]
    slabs = OUT_ROW_SLABS if x_ref.shape[0] % (16 * OUT_ROW_SLABS) == 0 else 1
    rows = x_ref.shape[0] // slabs
    for s in range(slabs):
        rs = slice(s * rows, (s + 1) * rows)
        o = o_refs[0][rs, :]
        for r in o_refs[1:]:
            o = o + r[rs, :]
        dlt = o - _group_mean(o, gavg)
        on = dlt * lax.rsqrt(_group_mean(dlt * dlt, gavg) + NORM_EPS) * gnw_ref[...]
        ret = (_silu(g_ref[rs, :]) * on).astype(BF16)
        cat = jnp.concatenate([a_ref[rs, :], b_ref[rs, :], f_ref[rs, :].astype(BF16), ret], axis=1)
        y = jnp.dot(cat, w_ref[...], preferred_element_type=F32)
        z = alpha * x_ref[rs, :] + g1_ref[...] * y
        out_ref[rs, :] = _ln(z) * lnw_ref[...] + lnb_ref[...]


def _out_proj(a, b, f, g, o_parts, x, mod, mod_row, lnw, lnb, w, layer, gnw, gavg, alpha, tm):
    bsz, length, d = x.shape
    row = lambda bb, i: (bb, i, 0)
    const2 = lambda bb, i: (0, 0)
    blk256 = pl.BlockSpec((None, tm, 256), row)
    in_specs = [blk256] * (4 + len(o_parts)) + [
        pl.BlockSpec((None, tm, d), row),
        pl.BlockSpec((None, None, 1, d), lambda bb, i: (mod_row(bb), 2, 0, 0)),
        pl.BlockSpec((1, d), const2), pl.BlockSpec((1, d), const2),
        pl.BlockSpec((None, d, d), lambda bb, i: (layer, 0, 0)),
        pl.BlockSpec((1, 256), const2), pl.BlockSpec((256, 256), const2)]
    return pl.pallas_call(
        functools.partial(_out_kernel, n_o=len(o_parts), alpha=alpha),
        out_shape=jax.ShapeDtypeStruct((bsz, length, d), F32),
        grid=(bsz, length // tm),
        in_specs=in_specs,
        out_specs=pl.BlockSpec((None, tm, d), row),
        compiler_params=_cparams(("arbitrary", "arbitrary")),
        name="out_proj",
    )(a, b, f, g, *o_parts, x, mod, lnw, lnb, w, gnw, gavg)


FFN_ROW_SLABS = 2
FFN_TILE = 512


def _ffn_kernel(x_ref, sh_ref, sc_ref, g2_ref, lnw_ref, lnb_ref, wg_ref, wu_ref, wd_ref, out_ref, *, fc, alpha):
    rows = x_ref.shape[0] // FFN_ROW_SLABS
    n_chunks = wg_ref.shape[1] // fc
    work = [(r, c) for r in range(FFN_ROW_SLABS) for c in range(n_chunks)]
    xs, hs, accs = {}, {}, {}

    def gate_up(r, c):
        if r not in hs:
            xs[r] = x_ref[r * rows:(r + 1) * rows, :]
            hs[r] = (_ln(xs[r]) * (1.0 + sc_ref[...]) + sh_ref[...]).astype(BF16)
        cols = slice(c * fc, (c + 1) * fc)
        return (jnp.dot(hs[r], wg_ref[:, cols], preferred_element_type=F32),
                jnp.dot(hs[r], wu_ref[:, cols], preferred_element_type=F32))

    ahead = gate_up(*work[0])
    for idx, (r, c) in enumerate(work):
        gate, up = ahead
        if idx + 1 < len(work):
            ahead = gate_up(*work[idx + 1])
        act = (_silu(gate) * up).astype(BF16)
        down = jnp.dot(act, wd_ref[c * fc:(c + 1) * fc, :], preferred_element_type=F32)
        accs[r] = down if c == 0 else accs[r] + down
        if c == n_chunks - 1:
            z = alpha * xs[r] + g2_ref[...] * accs[r]
            out_ref[r * rows:(r + 1) * rows, :] = _ln(z) * lnw_ref[...] + lnb_ref[...]


def _ffn(x, mod, mod_row, lnw, lnb, wgu, wd, layer, alpha, tm):
    bsz, length, d = x.shape
    ff = wd.shape[1]
    fc = ff // 2 if (ff // 2) % LANES == 0 else ff
    row = lambda bb, i: (bb, i, 0)
    const2 = lambda bb, i: (0, 0)
    modspec = lambda which: pl.BlockSpec((None, None, 1, d), lambda bb, i: (mod_row(bb), which, 0, 0))
    resident = lambda shape, col: pl.BlockSpec(shape, lambda bb, i: (layer, 0, col), pipeline_mode=pl.Buffered(1))
    return pl.pallas_call(
        functools.partial(_ffn_kernel, fc=fc, alpha=alpha),
        out_shape=jax.ShapeDtypeStruct((bsz, length, d), F32),
        grid=(bsz, length // tm),
        in_specs=[pl.BlockSpec((None, tm, d), row), modspec(3), modspec(4), modspec(5),
                  pl.BlockSpec((1, d), const2), pl.BlockSpec((1, d), const2),
                  resident((None, d, ff), 0), resident((None, d, ff), 1), resident((None, ff, d), 0)],
        out_specs=pl.BlockSpec((None, tm, d), row),
        compiler_params=_cparams(("arbitrary", "arbitrary")),
        name="ffn",
    )(x, mod, mod, mod, lnw, lnb, wgu, wgu, wd)


def _rope_tables(seq):
    t = np.arange(seq)
    f32 = np.float32

    def tab(pos, n_freq):
        inv = f32(ROPE_THETA) ** (-np.arange(n_freq, dtype=f32) / f32(n_freq))
        ang = (pos.astype(f32)[:, None] * inv[None, :]).astype(np.float64)
        return np.cos(ang), np.sin(ang)

    cr, sr = tab(t // GRID_W, HEAD_DIM // 4)
    cc, sc = tab(t % GRID_W, HEAD_DIM // 4)
    ct, st = tab(t, HEAD_DIM // 2)
    tables = (np.concatenate([cr, cr, cc, cc], -1), np.concatenate([-sr, sr, -sc, sc], -1),
              np.concatenate([ct, ct], -1), np.concatenate([-st, st], -1))
    return tuple(jnp.asarray(np.tile(a, (1, 2)), F32) for a in tables)


def kernel(x, c, ctx, c_ctx, w_mod, b_mod, w_in, a_q_norm, a_k_norm, b_sink, f_mix, r_decay, r_gn_w, w_out,
           ln1_w, ln1_b, w_gate_up, w_down, ln2_w, ln2_b):
    bsz, seq, d = x.shape
    depth = w_in.shape[0]
    gw = d // 4
    assert gw == 4 * HEAD_DIM and a_q_norm.shape[-1] == HEAD_DIM and seq % (FFT_L2 * 8) == 0
    alpha = (2.0 * depth) ** 0.25

    tabs = _rope_tables(seq)
    gavg = jnp.asarray(np.kron(np.eye(gw // HEAD_DIM), np.full((HEAD_DIM, HEAD_DIM), 1.0 / HEAD_DIM)), BF16)
    cc = jnp.zeros((8, d), F32).at[:bsz].set(c).at[bsz].set(c_ctx)
    mod_all = _modulation(cc, w_mod, b_mod).reshape(depth, 8, 6, 1, d)
    lat_row = lambda b: b
    ctx_row = lambda b: bsz

    eye_g = jnp.eye(gw // HEAD_DIM, dtype=F32)
    w_in_b, w_out_b = w_in.astype(BF16), w_out.astype(BF16)
    w_gu_b, w_dn_b = w_gate_up.astype(BF16), w_down.astype(BF16)

    for layer in range(depth):
        need_ctx = layer < depth - 1
        mod = mod_all[layer]
        qn = jnp.tile(a_q_norm[layer], 4)[None, :]
        kn = jnp.tile(a_k_norm[layer], 2)[None, :]
        gnw = r_gn_w[layer][None, :]
        lnw1, lnb1 = ln1_w[layer][None, :], ln1_b[layer][None, :]
        lnw2, lnb2 = ln2_w[layer][None, :], ln2_b[layer][None, :]
        fm_bd = jnp.einsum('gh,gce->gche', eye_g, f_mix[layer]).reshape(gw, gw).astype(BF16)
        rd = r_decay[layer]
        rdl = jnp.repeat(rd, HEAD_DIM, axis=1)[:, None, :]
        rdh = jnp.broadcast_to(rd[:, :, None, None], (2, 4, 1, RET_CHUNK))
        sink = b_sink[layer]

        p_l, u_l, g_l = _in_proj(x, mod, lat_row, w_in_b, layer, qn, kn, gavg, tabs, min(IN_TILE, seq))
        p_c, u_c, g_c = _in_proj(ctx, mod, ctx_row, w_in_b, layer, qn, kn, gavg, None, ctx.shape[1])

        a_l = _attention(p_l, 0, p_l, p_c, 2, 3, None, tq=ATTN_Q_TILE, tk=min(ATTN_K_CHUNK, seq))
        b_l = _window_attention(p_l, p_c, sink)
        f_l = _fourier_latent(u_l, fm_bd)
        r_out = _retention(p_l, p_c, rdl, rdh, need_ctx)
        x = _out_proj(a_l, b_l, f_l, g_l, r_out[:2], x, mod, lat_row, lnw1, lnb1, w_out_b, layer, gnw, gavg, alpha,
                      min(OUT_TILE, seq))
        x = _ffn(x, mod, lat_row, lnw2, lnb2, w_gu_b, w_dn_b, layer, alpha, min(FFN_TILE, seq))
        if need_ctx:
            lc = ctx.shape[1]
            a_c = _attention(p_c, 0, None, p_c, 2, 3, None, tq=Q_BLOCK, tk=ATTN_K_CHUNK)
            b_c = _attention(p_c, 2, None, p_c, 6, 7, sink, tq=Q_BLOCK, tk=ATTN_K_CHUNK)
            f_c = _fourier_direct(u_c, fm_bd)
            ctx = _out_proj(a_c, b_c, f_c, g_c, r_out[2:], ctx, mod, ctx_row, lnw1, lnb1, w_out_b, layer, gnw, gavg,
                            alpha, lc)
            ctx = _ffn(ctx, mod, ctx_row, lnw2, lnb2, w_gu_b, w_dn_b, layer, alpha, lc)
    return x
```

```python
import functools

import numpy as np
import jax
import jax.numpy as jnp
from jax import lax
from jax.experimental import pallas as pl
from jax.experimental.pallas import tpu as pltpu

F32 = jnp.float32
BF16 = jnp.bfloat16
HIGHEST = lax.Precision.HIGHEST

HEAD_DIM = 64
GRID_W = 64
Q_BLOCK = 128
ROPE_THETA = 10000.0
NORM_EPS = 1e-6
NEG_INF = -1e30
LOG2E = 1.4426950408889634

LANES = 128
VMEM_LIMIT_BYTES = 56 * 1024 * 1024

RET_CHUNK = 256
RET_STEP_CHUNKS = 4
FFT_L2 = 128


def _cparams(sem):
    return pltpu.CompilerParams(dimension_semantics=sem, vmem_limit_bytes=VMEM_LIMIT_BYTES)


def _ln(x):
    mu = jnp.mean(x, axis=-1, keepdims=True)
    xc = x - mu
    var = jnp.mean(xc * xc, axis=-1, keepdims=True)
    return xc * lax.rsqrt(var + NORM_EPS)


def _silu(x):
    return x * jax.nn.sigmoid(x)


def _group_mean(t, g):
    hi = t.astype(BF16)
    lo = (t - hi.astype(F32)).astype(BF16)
    return (jnp.dot(hi, g, preferred_element_type=F32) + jnp.dot(lo, g, preferred_element_type=F32))


def _dot_nt(a, b):
    return lax.dot_general(a, b, (((1,), (1,)), ((), ())), preferred_element_type=F32)


def _mod_kernel(c_ref, w_ref, b_ref, o_ref):
    h = _silu(c_ref[...])
    o_ref[...] = jnp.dot(h, w_ref[...], precision=HIGHEST, preferred_element_type=F32) + b_ref[...]


def _modulation(cc, w_mod, b_mod):
    depth, d, n = w_mod.shape
    tn = 2048
    return pl.pallas_call(
        _mod_kernel,
        out_shape=jax.ShapeDtypeStruct((depth, 8, n), F32),
        grid=(depth, n // tn),
        in_specs=[pl.BlockSpec((8, d), lambda l, j: (0, 0)),
                  pl.BlockSpec((None, d, tn), lambda l, j: (l, 0, j)),
                  pl.BlockSpec((None, 1, tn), lambda l, j: (l, 0, j))],
        out_specs=pl.BlockSpec((None, 8, tn), lambda l, j: (l, 0, j)),
        compiler_params=_cparams(("arbitrary", "arbitrary")),
        name="modulation",
    )(cc, w_mod, b_mod.reshape(depth, 1, n))


P_COLS = 14 * LANES


def _rope_lanes(t, c, ss, half):
    first = (lax.broadcasted_iota(jnp.int32, (t.shape[0], LANES), 1) & half) == 0
    outs = []
    for j in range(t.shape[1] // LANES):
        tj = t[:, j * LANES:(j + 1) * LANES]
        partner = jnp.where(first, pltpu.roll(tj, LANES - half, 1), pltpu.roll(tj, half, 1))
        outs.append(tj * c + partner * ss)
    return outs[0] if len(outs) == 1 else jnp.concatenate(outs, axis=1)


IN_ROW_SLABS = 4
IN_TILE = 1024


def _pair_heads_by_kv(q):
    a, b = q[:, 0:LANES], q[:, LANES:2 * LANES]
    lo = lax.broadcasted_iota(jnp.int32, a.shape, 1) < HEAD_DIM
    return jnp.concatenate([jnp.where(lo, a, pltpu.roll(b, HEAD_DIM, 1)),
                            jnp.where(lo, pltpu.roll(a, HEAD_DIM, 1), b)], axis=1)


def _in_kernel(*refs, rope):
    x_ref, sh_ref, sc_ref, w_ref, qn_ref, kn_ref, gavg_ref = refs[:7]
    if rope:
        c2_ref, ss2_ref, c1_ref, ss1_ref, p_ref, u_ref, g_ref = refs[7:]
    else:
        p_ref, u_ref, g_ref = refs[7:]
    def rms(t, w, g):
        return t * lax.rsqrt(_group_mean(t * t, g) + NORM_EPS) * w

    scale = HEAD_DIM ** -0.5
    qscale = scale * LOG2E
    slabs = IN_ROW_SLABS if x_ref.shape[0] % (8 * IN_ROW_SLABS) == 0 else 1
    rows = x_ref.shape[0] // slabs
    for r in range(slabs):
        rs = slice(r * rows, (r + 1) * rows)

        def rope2(t):
            if not rope:
                return t
            return _rope_lanes(t, c2_ref[rs, :], ss2_ref[rs, :], HEAD_DIM // 4)

        def rope1(t):
            if not rope:
                return t
            return _rope_lanes(t, c1_ref[rs, :], ss1_ref[rs, :], HEAD_DIM // 2)

        h = _ln(x_ref[rs, :]) * (1.0 + sc_ref[...]) + sh_ref[...]
        y = jnp.dot(h.astype(BF16), w_ref[...], preferred_element_type=F32)
        qa = rope2(rms(y[:, 0:256], qn_ref[...], gavg_ref[...])) * qscale
        ka = rope2(rms(y[:, 256:384], kn_ref[...], gavg_ref[0:LANES, 0:LANES]))
        p_ref[rs, 0:256] = _pair_heads_by_kv(qa).astype(BF16)
        p_ref[rs, 256:384] = ka.astype(BF16)
        p_ref[rs, 384:512] = y[:, 384:512].astype(BF16)
        p_ref[rs, 512:768] = _pair_heads_by_kv(rope2(y[:, 512:768]) * qscale).astype(BF16)
        p_ref[rs, 768:896] = rope2(y[:, 768:896]).astype(BF16)
        p_ref[rs, 896:1024] = y[:, 896:1024].astype(BF16)
        u_ref[rs, :] = y[:, 1024:1280]
        p_ref[rs, 1024:1280] = rope1(y[:, 1280:1536]).astype(BF16)
        p_ref[rs, 1280:1536] = (rope1(y[:, 1536:1792]) * scale).astype(BF16)
        p_ref[rs, 1536:1792] = y[:, 1792:2048].astype(BF16)
        g_ref[rs, :] = y[:, 2048:2304]


def _in_proj(x, mod, mod_row, w, layer, qn, kn, gavg, tabs, tm):
    bsz, length, d = x.shape
    nw = w.shape[2]
    nt = length // tm
    rope = tabs is not None
    row = lambda b, i: (b, i, 0)
    const2 = lambda b, i: (0, 0)
    in_specs = [pl.BlockSpec((None, tm, d), row),
                pl.BlockSpec((None, None, 1, d), lambda b, i: (mod_row(b), 0, 0, 0)),
                pl.BlockSpec((None, None, 1, d), lambda b, i: (mod_row(b), 1, 0, 0)),
                pl.BlockSpec((None, d, nw), lambda b, i: (layer, 0, 0)),
                pl.BlockSpec((1, 256), const2),
                pl.BlockSpec((1, LANES), const2),
                pl.BlockSpec((256, 256), const2)]
    args = [x, mod, mod, w, qn, kn, gavg]
    if rope:
        in_specs += [pl.BlockSpec((tm, LANES), lambda b, i: (i, 0))] * len(tabs)
        args += list(tabs)
    return pl.pallas_call(
        functools.partial(_in_kernel, rope=rope),
        out_shape=(jax.ShapeDtypeStruct((bsz, length, P_COLS), BF16),
                   jax.ShapeDtypeStruct((bsz, length, 256), F32),
                   jax.ShapeDtypeStruct((bsz, length, 256), F32)),
        grid=(bsz, nt),
        in_specs=in_specs,
        out_specs=(pl.BlockSpec((None, tm, P_COLS), row),
                   pl.BlockSpec((None, tm, 256), row),
                   pl.BlockSpec((None, tm, 256), row)),
        compiler_params=_cparams(("arbitrary", "arbitrary")),
        name="in_proj_rope" if rope else "in_proj_ctx",
    )(*args)


def _stack_heads(q):
    qf = q.astype(F32)
    lo = lax.broadcasted_iota(jnp.int32, (q.shape[0], LANES), 1) < HEAD_DIM
    q0, q1 = qf[:, 0:LANES], qf[:, LANES:2 * LANES]
    z = jnp.zeros_like(q0)
    return jnp.concatenate([jnp.where(lo, q0, z), jnp.where(lo, q1, z),
                            jnp.where(lo, z, q0), jnp.where(lo, z, q1)], axis=0).astype(BF16)


def _aug_values(v):
    vf = v.astype(F32)
    lo = lax.broadcasted_iota(jnp.int32, vf.shape, 1) < HEAD_DIM
    one = jnp.ones_like(vf)
    return jnp.where(lo, vf, one).astype(BF16), jnp.where(lo, one, vf).astype(BF16)


def _finish_heads(acc0, acc1, e, tq):
    l0 = pltpu.roll(acc0, HEAD_DIM, 1)
    l1 = pltpu.roll(acc1, HEAD_DIM, 1)
    if e is not None:
        l0 = l0 + e[:2 * tq]
        l1 = l1 + e[2 * tq:]
    n0 = acc0 / l0
    n1 = acc1 / l1
    lo = lax.broadcasted_iota(jnp.int32, (tq, LANES), 1) < HEAD_DIM
    return jnp.concatenate([jnp.where(lo, n0[:tq], pltpu.roll(n0[tq:], HEAD_DIM, 1)),
                            jnp.where(lo, pltpu.roll(n1[:tq], HEAD_DIM, 1), n1[tq:])], axis=1)


def _sink_column(sink_ref, tq):
    return jnp.concatenate([jnp.full((tq, 1), sink_ref[h] * LOG2E, F32) for h in range(4)], axis=0)


ATTN_Q_TILE = 256
ATTN_K_CHUNK = 2048

def _attn_kernel(*refs, tq, tk, n_lat, has_sink):
    i = 0
    sink_ref = None
    if has_sink:
        sink_ref = refs[0]
        i = 1
    q_ref = refs[i]
    i += 1
    if n_lat:
        kl_ref, vl_ref = refs[i:i + 2]
        i += 2
    kc_ref, vc_ref, o_ref = refs[i:i + 3]
    i += 3
    if n_lat:
        v0l_ref, v1l_ref = refs[i:i + 2]
        i += 2
    v0c_ref, v1c_ref = refs[i:i + 2]

    @pl.when(pl.program_id(1) == 0)
    def _():
        if n_lat:
            a0, a1 = _aug_values(vl_ref[...])
            v0l_ref[...] = a0
            v1l_ref[...] = a1
        a0, a1 = _aug_values(vc_ref[...])
        v0c_ref[...] = a0
        v1c_ref[...] = a1

    qs = _stack_heads(q_ref[...])
    half = 2 * tq

    chunks = [(kl_ref, v0l_ref, v1l_ref, slice(c * tk, (c + 1) * tk)) for c in range(n_lat)]
    chunks.append((kc_ref, v0c_ref, v1c_ref, slice(None)))

    def scores(chunk):
        k_ref, _, _, rows = chunk
        return _dot_nt(qs, k_ref[rows, :])

    m = _sink_column(sink_ref, tq) if has_sink else jnp.full((4 * tq, 1), NEG_INF, F32)
    acc0 = jnp.zeros((half, LANES), F32)
    acc1 = jnp.zeros((half, LANES), F32)
    s_next = scores(chunks[0])
    for idx, (_, v0_ref, v1_ref, rows) in enumerate(chunks):
        s = s_next
        if idx + 1 < len(chunks):
            s_next = scores(chunks[idx + 1])
        m_new = jnp.maximum(m, jnp.max(s, axis=1, keepdims=True))
        alpha = jnp.exp2(m - m_new)
        p = jnp.exp2(s - m_new).astype(BF16)
        acc0 = acc0 * alpha[:half] + jnp.dot(p[:half], v0_ref[rows, :], preferred_element_type=F32)
        acc1 = acc1 * alpha[half:] + jnp.dot(p[half:], v1_ref[rows, :], preferred_element_type=F32)
        m = m_new
    e = jnp.exp2(_sink_column(sink_ref, tq) - m) if has_sink else None
    o_ref[...] = _finish_heads(acc0, acc1, e, tq).astype(BF16)


def _attention(pq, q_blk, p_lat, p_ctx, k_blk, v_blk, sink, tq, tk):
    bsz, lq, _ = pq.shape
    lc = p_ctx.shape[1]
    assert p_lat is None or p_lat.shape[1] % tk == 0
    n_lat = 0 if p_lat is None else p_lat.shape[1] // tk
    has_sink = sink is not None
    in_specs = [pl.BlockSpec((None, tq, 256), lambda b, i, *_: (b, i, q_blk))]
    args = [pq]
    scratch = []
    if n_lat:
        ll = p_lat.shape[1]
        in_specs += [pl.BlockSpec((None, ll, LANES), lambda b, i, *_: (b, 0, k_blk)),
                     pl.BlockSpec((None, ll, LANES), lambda b, i, *_: (b, 0, v_blk))]
        args += [p_lat, p_lat]
        scratch += [pltpu.VMEM((ll, LANES), BF16), pltpu.VMEM((ll, LANES), BF16)]
    in_specs += [pl.BlockSpec((None, lc, LANES), lambda b, i, *_: (b, 0, k_blk)),
                 pl.BlockSpec((None, lc, LANES), lambda b, i, *_: (b, 0, v_blk))]
    args += [p_ctx, p_ctx]
    scratch += [pltpu.VMEM((lc, LANES), BF16), pltpu.VMEM((lc, LANES), BF16)]
    kern = functools.partial(_attn_kernel, tq=tq, tk=tk, n_lat=n_lat, has_sink=has_sink)
    grid_spec = pltpu.PrefetchScalarGridSpec(
        num_scalar_prefetch=1 if has_sink else 0,
        grid=(bsz, lq // tq),
        in_specs=in_specs,
        out_specs=pl.BlockSpec((None, tq, 256), lambda b, i, *_: (b, i, 0)),
        scratch_shapes=scratch)
    call = pl.pallas_call(
        kern, out_shape=jax.ShapeDtypeStruct((bsz, lq, 256), BF16), grid_spec=grid_spec,
        compiler_params=_cparams(("arbitrary", "arbitrary")),
        name="attn_sink" if has_sink else ("attn_global" if n_lat else "attn_ctx"))
    return call(sink, *args) if has_sink else call(*args)


WIN_BLOCKS_PER_STEP = 8


def _win_kernel(sink_ref, q_ref, kl_ref, vl_ref, kc_ref, vc_ref, o_ref, *, nb):
    tq = Q_BLOCK
    r = lax.broadcasted_iota(jnp.int32, (4 * tq, tq), 0) & (tq - 1)
    j = lax.broadcasted_iota(jnp.int32, (4 * tq, tq), 1)
    in_prev = j >= r
    in_next = j <= r
    snk = _sink_column(sink_ref, tq)
    kc, vc = kc_ref[...], vc_ref[...]

    def rows(ref, blk):
        return ref[pl.ds(pl.multiple_of(blk * tq, tq), tq), :]

    def scores(t):
        i = pl.program_id(1) * WIN_BLOCKS_PER_STEP + t
        prev = jnp.maximum(i - 1, 0)
        nxt = jnp.minimum(i + 1, nb - 1)
        k = jnp.concatenate([rows(kl_ref, prev), rows(kl_ref, i), rows(kl_ref, nxt), kc], axis=0)
        qs = _stack_heads(q_ref[t * tq:(t + 1) * tq, :])
        return _dot_nt(qs, k), i, prev, nxt

    ahead = scores(0)
    for t in range(WIN_BLOCKS_PER_STEP):
        s, i, prev, nxt = ahead
        if t + 1 < WIN_BLOCKS_PER_STEP:
            ahead = scores(t + 1)
        v = jnp.concatenate([rows(vl_ref, prev), rows(vl_ref, i), rows(vl_ref, nxt), vc], axis=0)
        off_prev = jnp.where(i > 0, 0.0, NEG_INF)
        off_next = jnp.where(i < nb - 1, 0.0, NEG_INF)
        s = jnp.concatenate([jnp.where(in_prev, s[:, 0:tq] + off_prev, NEG_INF), s[:, tq:2 * tq],
                             jnp.where(in_next, s[:, 2 * tq:3 * tq] + off_next, NEG_INF), s[:, 3 * tq:]], axis=1)
        m = jnp.maximum(jnp.max(s, axis=1, keepdims=True), snk)
        p = jnp.exp2(s - m).astype(BF16)
        v0, v1 = _aug_values(v)
        acc0 = jnp.dot(p[:2 * tq], v0, preferred_element_type=F32)
        acc1 = jnp.dot(p[2 * tq:], v1, preferred_element_type=F32)
        o_ref[t * tq:(t + 1) * tq, :] = _finish_heads(acc0, acc1, jnp.exp2(snk - m), tq).astype(BF16)


def _window_attention(p_lat, p_ctx, sink):
    bsz, ll, _ = p_lat.shape
    lc = p_ctx.shape[1]
    nb = ll // Q_BLOCK
    tqs = WIN_BLOCKS_PER_STEP * Q_BLOCK
    assert ll % tqs == 0
    grid_spec = pltpu.PrefetchScalarGridSpec(
        num_scalar_prefetch=1,
        grid=(bsz, ll // tqs),
        in_specs=[pl.BlockSpec((None, tqs, 256), lambda b, i, *_: (b, i, 2)),
                  pl.BlockSpec((None, ll, LANES), lambda b, i, *_: (b, 0, 6)),
                  pl.BlockSpec((None, ll, LANES), lambda b, i, *_: (b, 0, 7)),
                  pl.BlockSpec((None, lc, LANES), lambda b, i, *_: (b, 0, 6)),
                  pl.BlockSpec((None, lc, LANES), lambda b, i, *_: (b, 0, 7))],
        out_specs=pl.BlockSpec((None, tqs, 256), lambda b, i, *_: (b, i, 0)))
    return pl.pallas_call(
        functools.partial(_win_kernel, nb=nb),
        out_shape=jax.ShapeDtypeStruct((bsz, ll, 256), BF16), grid_spec=grid_spec,
        compiler_params=_cparams(("arbitrary", "arbitrary")),
        name="attn_window",
    )(sink, p_lat, p_lat, p_lat, p_ctx, p_ctx)


FFT_ROWS = 8
FFT1_STEP_ROWS = 16


def _fft1_kernel(u_ref, w_ref, y_ref):
    l1, rows, w = u_ref.shape
    ys = []
    for h in range(rows // FFT_ROWS):
        u = u_ref[:, h * FFT_ROWS:(h + 1) * FFT_ROWS, :].reshape(l1 * FFT_ROWS, w).astype(BF16)
        ys.append(jnp.dot(w_ref[...], u, preferred_element_type=F32).reshape(2, l1, FFT_ROWS, w))
    y_ref[...] = jnp.concatenate(ys, axis=2).astype(BF16)


def _channel_mix(ab, g_ref, fm_ref):
    z = jnp.dot(ab.astype(BF16), g_ref[...], preferred_element_type=F32)
    return jnp.dot(z.astype(BF16), fm_ref[...], preferred_element_type=F32)


def _fft2_kernel(y_ref, c_ref, s_ref, g_ref, fm_ref, o_ref):
    l2 = y_ref.shape[2]
    ab = []
    for r in range(FFT_ROWS):
        yr, yi = y_ref[0, r], y_ref[1, r]
        cs = jnp.concatenate([c_ref[r], s_ref[r]], axis=1)
        rhs = jnp.concatenate([jnp.concatenate([yr, yi], axis=1),
                               jnp.concatenate([yi, -yr], axis=1)], axis=0)
        ab.append(jnp.dot(cs, rhs, preferred_element_type=F32))
    o = _channel_mix(jnp.concatenate(ab, axis=0), g_ref, fm_ref)
    for r in range(FFT_ROWS):
        o_ref[:, r, :] = o[r * l2:(r + 1) * l2]


def _fft_direct_kernel(u_ref, cs_ref, g_ref, fm_ref, o_ref):
    n = u_ref.shape[0]
    y = jnp.dot(cs_ref[...], u_ref[...].astype(BF16), preferred_element_type=F32)
    o_ref[...] = _channel_mix(jnp.concatenate([y[:n], y[n:]], axis=1), g_ref, fm_ref)


def _mxu_const(a):
    return jnp.asarray(a, F32).astype(BF16)


def _dft_tables(n_rows, n_cols, length, row_stride=1, row_offset=0):
    k = row_offset + row_stride * np.arange(n_rows, dtype=np.int64)
    n = np.arange(n_cols, dtype=np.int64)
    ang = 2.0 * np.pi * ((k[:, None] * n[None, :]) % length).astype(np.float64) / length
    return np.cos(ang), np.sin(ang)


def _channel_dft(width, length):
    c, s = _dft_tables(HEAD_DIM, HEAD_DIM, HEAD_DIM)
    eye = np.eye(width // HEAD_DIM) / np.sqrt(float(length) * HEAD_DIM)
    return _mxu_const(np.concatenate([np.kron(eye, c), np.kron(eye, s)], axis=0))


def _fourier_latent(u, fm_bd):
    bsz, length, w = u.shape
    l2 = FFT_L2
    l1 = length // l2
    rows = FFT_ROWS
    c1, s1 = _dft_tables(l1, l1, l1)
    w1 = _mxu_const(np.kron(np.concatenate([c1, -s1], axis=0), np.eye(rows)))
    y = pl.pallas_call(
        _fft1_kernel,
        out_shape=jax.ShapeDtypeStruct((bsz, 2, l1, l2, w), BF16),
        grid=(bsz, l2 // FFT1_STEP_ROWS),
        in_specs=[pl.BlockSpec((None, l1, FFT1_STEP_ROWS, w), lambda b, j: (b, 0, j, 0)),
                  pl.BlockSpec(w1.shape, lambda b, j: (0, 0))],
        out_specs=pl.BlockSpec((None, 2, l1, FFT1_STEP_ROWS, w), lambda b, j: (b, 0, 0, j, 0)),
        compiler_params=_cparams(("arbitrary", "arbitrary")),
        name="fourier_stage1",
    )(u.reshape(bsz, l1, l2, w), w1)
    tabs = [_dft_tables(l2, l2, length, row_stride=l1, row_offset=k1) for k1 in range(l1)]
    ck = _mxu_const(np.stack([t[0] for t in tabs]))
    sk = _mxu_const(np.stack([t[1] for t in tabs]))
    const2 = lambda b, k: (0, 0)
    out = pl.pallas_call(
        _fft2_kernel,
        out_shape=jax.ShapeDtypeStruct((bsz, l2, l1, w), F32),
        grid=(bsz, l1 // rows),
        in_specs=[pl.BlockSpec((None, 2, rows, l2, w), lambda b, k: (b, 0, k, 0, 0)),
                  pl.BlockSpec((rows, l2, l2), lambda b, k: (k, 0, 0)),
                  pl.BlockSpec((rows, l2, l2), lambda b, k: (k, 0, 0)),
                  pl.BlockSpec((2 * w, w), const2), pl.BlockSpec((w, w), const2)],
        out_specs=pl.BlockSpec((None, l2, rows, w), lambda b, k: (b, 0, k, 0)),
        compiler_params=_cparams(("arbitrary", "arbitrary")),
        name="fourier_stage2",
    )(y, ck, sk, _channel_dft(w, length), fm_bd)
    return out.reshape(bsz, length, w)


def _fourier_direct(u, fm_bd):
    bsz, length, w = u.shape
    c, s = _dft_tables(length, length, length)
    cs = _mxu_const(np.concatenate([c, -s], axis=0))
    const2 = lambda b: (0, 0)
    return pl.pallas_call(
        _fft_direct_kernel,
        out_shape=jax.ShapeDtypeStruct((bsz, length, w), F32),
        grid=(bsz,),
        in_specs=[pl.BlockSpec((None, length, w), lambda b: (b, 0, 0)),
                  pl.BlockSpec((2 * length, length), const2),
                  pl.BlockSpec((2 * w, w), const2), pl.BlockSpec((w, w), const2)],
        out_specs=pl.BlockSpec((None, length, w), lambda b: (b, 0, 0)),
        compiler_params=_cparams(("arbitrary",)),
        name="fourier_ctx",
    )(u, cs, _channel_dft(w, length), fm_bd)


def _log_sigmoid(x):
    return jnp.minimum(x, 0.0) - jnp.log1p(jnp.exp(-jnp.abs(x)))


def _ret_kernel(*refs, need_ctx):
    (rdl_ref, rdh_ref, qf_ref, kf_ref, vf_ref, qb_ref, kb_ref, vb_ref, qc_ref, kc_ref, vc_ref) = refs[:11]
    if need_ctx:
        of_ref, ob_ref, oc_ref = refs[11:14]
        scr = refs[14:]
    else:
        of_ref, ob_ref = refs[11:13]
        oc_ref = None
        scr = refs[13:]
    sf_ref, sb_ref, din_ref, tab_ref = scr
    c = RET_CHUNK
    w = 4 * HEAD_DIM
    j = pl.program_id(1)
    head_shift = HEAD_DIM.bit_length() - 1
    lane_head = lax.broadcasted_iota(jnp.int32, (c, w), 1) >> head_shift
    blockdiag = ((lax.broadcasted_iota(jnp.int32, (w, w), 0) >> head_shift)
                 == (lax.broadcasted_iota(jnp.int32, (w, w), 1) >> head_shift))

    def decayed_scores(q, k, d):
        qf = q.astype(F32)
        inner = []
        for h in range(4):
            qh = jnp.where(lane_head == h, qf, 0.0).astype(BF16)
            inner.append((_dot_nt(qh, k) * din_ref[d, h]).astype(BF16))
        return inner, (k.astype(F32) * tab_ref[d, 1]).T.astype(BF16)

    def chunk_outputs(scores, v):
        inner, kz = scores
        o = jnp.zeros((c, w), F32)
        for h in range(4):
            o = o + jnp.where(lane_head == h, jnp.dot(inner[h], v, preferred_element_type=F32), 0.0)
        kv = jnp.where(blockdiag, jnp.dot(kz, v, preferred_element_type=F32), 0.0)
        return o, kv

    def chunk(q, k, v, d):
        return chunk_outputs(decayed_scores(q, k, d), v)

    @pl.when(j == 0)
    def _():
        t = lax.broadcasted_iota(jnp.int32, (c, w), 0).astype(F32)
        rr = lax.broadcasted_iota(jnp.int32, (c, c), 0)
        cc = lax.broadcasted_iota(jnp.int32, (c, c), 1)
        for d in range(2):
            lg = _log_sigmoid(rdl_ref[d])
            tab_ref[d, 0] = jnp.exp(lg * ((t + 1.0) if d == 0 else (c - t)))
            tab_ref[d, 1] = jnp.exp(lg * ((c - 1.0 - t) if d == 0 else t))
            tab_ref[d, 2] = jnp.exp(jnp.broadcast_to(lg, (c, w)) * float(c))
            diff = (rr - cc) if d == 0 else (cc - rr)
            dpos = jnp.maximum(diff, 0).astype(F32)
            for h in range(4):
                lgh = _log_sigmoid(rdh_ref[d, h])
                din_ref[d, h] = jnp.where(diff >= 0, jnp.exp(lgh * dpos), 0.0)
        q, k, v = qc_ref[...], kc_ref[...], vc_ref[...]
        o_f, kv_f = chunk(q, k, v, 0)
        o_b, kv_b = chunk(q, k, v, 1)
        sf_ref[...] = kv_f
        sb_ref[...] = kv_b
        if need_ctx:
            oc_ref[...] = o_f + o_b

    @pl.when(j > 0)
    def _():
        n_sub = qf_ref.shape[0] // c
        io = ((qf_ref, kf_ref, vf_ref, of_ref), (qb_ref, kb_ref, vb_ref, ob_ref))
        states = [sf_ref[...], sb_ref[...]]
        work = [(d, t if d == 0 else n_sub - 1 - t) for t in range(n_sub) for d in range(2)]

        def start(d, t):
            rs = slice(t * c, (t + 1) * c)
            q, k, v = io[d][0][rs, :], io[d][1][rs, :], io[d][2][rs, :]
            return q, v, decayed_scores(q, k, d)

        ahead = start(*work[0])
        for idx, (d, t) in enumerate(work):
            q, v, scores = ahead
            if idx + 1 < len(work):
                ahead = start(*work[idx + 1])
            o, kv = chunk_outputs(scores, v)
            cross = jnp.dot(q, states[d].astype(BF16), preferred_element_type=F32) * tab_ref[d, 0]
            io[d][3][t * c:(t + 1) * c, :] = o + cross
            states[d] = states[d] * tab_ref[d, 2, 0:1, :] + kv
        sf_ref[...] = states[0]
        sb_ref[...] = states[1]


def _retention(p_lat, p_ctx, rdl, rdh, need_ctx):
    bsz, ll, _ = p_lat.shape
    lc = p_ctx.shape[1]
    c = RET_CHUNK
    w = 4 * HEAD_DIM
    cs = c * RET_STEP_CHUNKS
    assert lc == c and ll % cs == 0
    n = ll // cs
    fwd = lambda blk: (lambda b, j: (b, jnp.maximum(j - 1, 0), blk))
    bwd = lambda blk: (lambda b, j: (b, n - 1 - jnp.maximum(j - 1, 0), blk))
    ctx = lambda blk: (lambda b, j: (b, 0, blk))
    in_specs = [pl.BlockSpec((2, 1, w), lambda b, j: (0, 0, 0)),
                pl.BlockSpec((2, 4, 1, c), lambda b, j: (0, 0, 0, 0))]
    in_specs += [pl.BlockSpec((None, cs, w), fwd(blk)) for blk in (4, 5, 6)]
    in_specs += [pl.BlockSpec((None, cs, w), bwd(blk)) for blk in (4, 5, 6)]
    in_specs += [pl.BlockSpec((None, c, w), ctx(blk)) for blk in (4, 5, 6)]
    out_shape = [jax.ShapeDtypeStruct((bsz, ll, w), F32), jax.ShapeDtypeStruct((bsz, ll, w), F32)]
    out_specs = [pl.BlockSpec((None, cs, w), fwd(0)), pl.BlockSpec((None, cs, w), bwd(0))]
    if need_ctx:
        out_shape.append(jax.ShapeDtypeStruct((bsz, lc, w), F32))
        out_specs.append(pl.BlockSpec((None, c, w), ctx(0)))
    return pl.pallas_call(
        functools.partial(_ret_kernel, need_ctx=need_ctx),
        out_shape=tuple(out_shape),
        grid=(bsz, n + 1),
        in_specs=in_specs,
        out_specs=tuple(out_specs),
        scratch_shapes=[pltpu.VMEM((w, w), F32), pltpu.VMEM((w, w), F32),
                        pltpu.VMEM((2, 4, c, c), F32), pltpu.VMEM((2, 3, c, w), F32)],
        compiler_params=_cparams(("arbitrary", "arbitrary")),
        name="retention_ctx_out" if need_ctx else "retention",
    )(rdl, rdh, *([p_lat] * 6), *([p_ctx] * 3))


OUT_ROW_SLABS = 4
OUT_TILE = 1024


def _out_kernel(*refs, n_o, alpha):
    a_ref, b_ref, f_ref, g_ref = refs[:4]
    o_refs = refs[4:4 + n_o]
    x_ref, g1_ref, lnw_ref, lnb_ref, w_ref, gnw_ref, gavg_ref, out_ref = refs[4 + n_o:]
    gavg = gavg_ref[...]
    slabs = OUT_ROW_SLABS if x_ref.shape[0] % (16 * OUT_ROW_SLABS) == 0 else 1
    rows = x_ref.shape[0] // slabs
    for s in range(slabs):
        rs = slice(s * rows, (s + 1) * rows)
        o = o_refs[0][rs, :]
        for r in o_refs[1:]:
            o = o + r[rs, :]
        dlt = o - _group_mean(o, gavg)
        on = dlt * lax.rsqrt(_group_mean(dlt * dlt, gavg) + NORM_EPS) * gnw_ref[...]
        ret = (_silu(g_ref[rs, :]) * on).astype(BF16)
        cat = jnp.concatenate([a_ref[rs, :], b_ref[rs, :], f_ref[rs, :].astype(BF16), ret], axis=1)
        y = jnp.dot(cat, w_ref[...], preferred_element_type=F32)
        z = alpha * x_ref[rs, :] + g1_ref[...] * y
        out_ref[rs, :] = _ln(z) * lnw_ref[...] + lnb_ref[...]


def _out_proj(a, b, f, g, o_parts, x, mod, mod_row, lnw, lnb, w, layer, gnw, gavg, alpha, tm):
    bsz, length, d = x.shape
    row = lambda bb, i: (bb, i, 0)
    const2 = lambda bb, i: (0, 0)
    blk256 = pl.BlockSpec((None, tm, 256), row)
    in_specs = [blk256] * (4 + len(o_parts)) + [
        pl.BlockSpec((None, tm, d), row),
        pl.BlockSpec((None, None, 1, d), lambda bb, i: (mod_row(bb), 2, 0, 0)),
        pl.BlockSpec((1, d), const2), pl.BlockSpec((1, d), const2),
        pl.BlockSpec((None, d, d), lambda bb, i: (layer, 0, 0)),
        pl.BlockSpec((1, 256), const2), pl.BlockSpec((256, 256), const2)]
    return pl.pallas_call(
        functools.partial(_out_kernel, n_o=len(o_parts), alpha=alpha),
        out_shape=jax.ShapeDtypeStruct((bsz, length, d), F32),
        grid=(bsz, length // tm),
        in_specs=in_specs,
        out_specs=pl.BlockSpec((None, tm, d), row),
        compiler_params=_cparams(("arbitrary", "arbitrary")),
        name="out_proj",
    )(a, b, f, g, *o_parts, x, mod, lnw, lnb, w, gnw, gavg)


FFN_ROW_SLABS = 2
FFN_TILE = 512


def _ffn_kernel(x_ref, sh_ref, sc_ref, g2_ref, lnw_ref, lnb_ref, wg_ref, wu_ref, wd_ref, out_ref, *, fc, alpha):
    rows = x_ref.shape[0] // FFN_ROW_SLABS
    n_chunks = wg_ref.shape[1] // fc
    work = [(r, c) for r in range(FFN_ROW_SLABS) for c in range(n_chunks)]
    xs, hs, accs = {}, {}, {}

    def gate_up(r, c):
        if r not in hs:
            xs[r] = x_ref[r * rows:(r + 1) * rows, :]
            hs[r] = (_ln(xs[r]) * (1.0 + sc_ref[...]) + sh_ref[...]).astype(BF16)
        cols = slice(c * fc, (c + 1) * fc)
        return (jnp.dot(hs[r], wg_ref[:, cols], preferred_element_type=F32),
                jnp.dot(hs[r], wu_ref[:, cols], preferred_element_type=F32))

    ahead = gate_up(*work[0])
    for idx, (r, c) in enumerate(work):
        gate, up = ahead
        if idx + 1 < len(work):
            ahead = gate_up(*work[idx + 1])
        act = (_silu(gate) * up).astype(BF16)
        down = jnp.dot(act, wd_ref[c * fc:(c + 1) * fc, :], preferred_element_type=F32)
        accs[r] = down if c == 0 else accs[r] + down
        if c == n_chunks - 1:
            z = alpha * xs[r] + g2_ref[...] * accs[r]
            out_ref[r * rows:(r + 1) * rows, :] = _ln(z) * lnw_ref[...] + lnb_ref[...]


def _ffn(x, mod, mod_row, lnw, lnb, wgu, wd, layer, alpha, tm):
    bsz, length, d = x.shape
    ff = wd.shape[1]
    fc = ff // 2 if (ff // 2) % LANES == 0 else ff
    row = lambda bb, i: (bb, i, 0)
    const2 = lambda bb, i: (0, 0)
    modspec = lambda which: pl.BlockSpec((None, None, 1, d), lambda bb, i: (mod_row(bb), which, 0, 0))
    resident = lambda shape, col: pl.BlockSpec(shape, lambda bb, i: (layer, 0, col), pipeline_mode=pl.Buffered(1))
    return pl.pallas_call(
        functools.partial(_ffn_kernel, fc=fc, alpha=alpha),
        out_shape=jax.ShapeDtypeStruct((bsz, length, d), F32),
        grid=(bsz, length // tm),
        in_specs=[pl.BlockSpec((None, tm, d), row), modspec(3), modspec(4), modspec(5),
                  pl.BlockSpec((1, d), const2), pl.BlockSpec((1, d), const2),
                  resident((None, d, ff), 0), resident((None, d, ff), 1), resident((None, ff, d), 0)],
        out_specs=pl.BlockSpec((None, tm, d), row),
        compiler_params=_cparams(("arbitrary", "arbitrary")),
        name="ffn",
    )(x, mod, mod, mod, lnw, lnb, wgu, wgu, wd)


def _rope_tables(seq):
    t = np.arange(seq)
    f32 = np.float32

    def tab(pos, n_freq):
        inv = f32(ROPE_THETA) ** (-np.arange(n_freq, dtype=f32) / f32(n_freq))
        ang = (pos.astype(f32)[:, None] * inv[None, :]).astype(np.float64)
        return np.cos(ang), np.sin(ang)

    cr, sr = tab(t // GRID_W, HEAD_DIM // 4)
    cc, sc = tab(t % GRID_W, HEAD_DIM // 4)
    ct, st = tab(t, HEAD_DIM // 2)
    tables = (np.concatenate([cr, cr, cc, cc], -1), np.concatenate([-sr, sr, -sc, sc], -1),
              np.concatenate([ct, ct], -1), np.concatenate([-st, st], -1))
    return tuple(jnp.asarray(np.tile(a, (1, 2)), F32) for a in tables)


def kernel(x, c, ctx, c_ctx, w_mod, b_mod, w_in, a_q_norm, a_k_norm, b_sink, f_mix, r_decay, r_gn_w, w_out,
           ln1_w, ln1_b, w_gate_up, w_down, ln2_w, ln2_b):
    bsz, seq, d = x.shape
    depth = w_in.shape[0]
    gw = d // 4
    assert gw == 4 * HEAD_DIM and a_q_norm.shape[-1] == HEAD_DIM and seq % (FFT_L2 * 8) == 0
    alpha = (2.0 * depth) ** 0.25

    tabs = _rope_tables(seq)
    gavg = jnp.asarray(np.kron(np.eye(gw // HEAD_DIM), np.full((HEAD_DIM, HEAD_DIM), 1.0 / HEAD_DIM)), BF16)
    cc = jnp.zeros((8, d), F32).at[:bsz].set(c).at[bsz].set(c_ctx)
    mod_all = _modulation(cc, w_mod, b_mod).reshape(depth, 8, 6, 1, d)
    lat_row = lambda b: b
    ctx_row = lambda b: bsz

    eye_g = jnp.eye(gw // HEAD_DIM, dtype=F32)
    w_in_b, w_out_b = w_in.astype(BF16), w_out.astype(BF16)
    w_gu_b, w_dn_b = w_gate_up.astype(BF16), w_down.astype(BF16)

    for layer in range(depth):
        need_ctx = layer < depth - 1
        mod = mod_all[layer]
        qn = jnp.tile(a_q_norm[layer], 4)[None, :]
        kn = jnp.tile(a_k_norm[layer], 2)[None, :]
        gnw = r_gn_w[layer][None, :]
        lnw1, lnb1 = ln1_w[layer][None, :], ln1_b[layer][None, :]
        lnw2, lnb2 = ln2_w[layer][None, :], ln2_b[layer][None, :]
        fm_bd = jnp.einsum('gh,gce->gche', eye_g, f_mix[layer]).reshape(gw, gw).astype(BF16)
        rd = r_decay[layer]
        rdl = jnp.repeat(rd, HEAD_DIM, axis=1)[:, None, :]
        rdh = jnp.broadcast_to(rd[:, :, None, None], (2, 4, 1, RET_CHUNK))
        sink = b_sink[layer]

        p_l, u_l, g_l = _in_proj(x, mod, lat_row, w_in_b, layer, qn, kn, gavg, tabs, min(IN_TILE, seq))
        p_c, u_c, g_c = _in_proj(ctx, mod, ctx_row, w_in_b, layer, qn, kn, gavg, None, ctx.shape[1])

        a_l = _attention(p_l, 0, p_l, p_c, 2, 3, None, tq=ATTN_Q_TILE, tk=min(ATTN_K_CHUNK, seq))
        b_l = _window_attention(p_l, p_c, sink)
        f_l = _fourier_latent(u_l, fm_bd)
        r_out = _retention(p_l, p_c, rdl, rdh, need_ctx)
        x = _out_proj(a_l, b_l, f_l, g_l, r_out[:2], x, mod, lat_row, lnw1, lnb1, w_out_b, layer, gnw, gavg, alpha,
                      min(OUT_TILE, seq))
        x = _ffn(x, mod, lat_row, lnw2, lnb2, w_gu_b, w_dn_b, layer, alpha, min(FFN_TILE, seq))
        if need_ctx:
            lc = ctx.shape[1]
            a_c = _attention(p_c, 0, None, p_c, 2, 3, None, tq=Q_BLOCK, tk=ATTN_K_CHUNK)
            b_c = _attention(p_c, 2, None, p_c, 6, 7, sink, tq=Q_BLOCK, tk=ATTN_K_CHUNK)
            f_c = _fourier_direct(u_c, fm_bd)
            ctx = _out_proj(a_c, b_c, f_c, g_c, r_out[2:], ctx, mod, ctx_row, lnw1, lnb1, w_out_b, layer, gnw, gavg,
                            alpha, lc)
            ctx = _ffn(ctx, mod, ctx_row, lnw2, lnb2, w_gu_b, w_dn_b, layer, alpha, lc)
    return x
```

```python
import functools

import numpy as np
import jax
import jax.numpy as jnp
from jax import lax
from jax.experimental import pallas as pl
from jax.experimental.pallas import tpu as pltpu

F32 = jnp.float32
BF16 = jnp.bfloat16
HIGHEST = lax.Precision.HIGHEST

HEAD_DIM = 64
GRID_W = 64
Q_BLOCK = 128
ROPE_THETA = 10000.0
NORM_EPS = 1e-6
NEG_INF = -1e30
LOG2E = 1.4426950408889634

LANES = 128
VMEM_LIMIT_BYTES = 56 * 1024 * 1024

RET_CHUNK = 256
RET_STEP_CHUNKS = 4
FFT_L2 = 128


def _cparams(sem):
    return pltpu.CompilerParams(dimension_semantics=sem, vmem_limit_bytes=VMEM_LIMIT_BYTES)


def _ln(x):
    mu = jnp.mean(x, axis=-1, keepdims=True)
    xc = x - mu
    var = jnp.mean(xc * xc, axis=-1, keepdims=True)
    return xc * lax.rsqrt(var + NORM_EPS)


def _silu(x):
    return x * jax.nn.sigmoid(x)


def _group_mean(t, g):
    hi = t.astype(BF16)
    lo = (t - hi.astype(F32)).astype(BF16)
    return (jnp.dot(hi, g, preferred_element_type=F32) + jnp.dot(lo, g, preferred_element_type=F32))


def _dot_nt(a, b):
    return lax.dot_general(a, b, (((1,), (1,)), ((), ())), preferred_element_type=F32)


def _mod_kernel(c_ref, w_ref, b_ref, o_ref):
    h = _silu(c_ref[...])
    o_ref[...] = jnp.dot(h, w_ref[...], precision=HIGHEST, preferred_element_type=F32) + b_ref[...]


def _modulation(cc, w_mod, b_mod):
    depth, d, n = w_mod.shape
    tn = 2048
    return pl.pallas_call(
        _mod_kernel,
        out_shape=jax.ShapeDtypeStruct((depth, 8, n), F32),
        grid=(depth, n // tn),
        in_specs=[pl.BlockSpec((8, d), lambda l, j: (0, 0)),
                  pl.BlockSpec((None, d, tn), lambda l, j: (l, 0, j)),
                  pl.BlockSpec((None, 1, tn), lambda l, j: (l, 0, j))],
        out_specs=pl.BlockSpec((None, 8, tn), lambda l, j: (l, 0, j)),
        compiler_params=_cparams(("arbitrary", "arbitrary")),
        name="modulation",
    )(cc, w_mod, b_mod.reshape(depth, 1, n))


P_COLS = 14 * LANES


def _rope_lanes(t, c, ss, half):
    first = (lax.broadcasted_iota(jnp.int32, (t.shape[0], LANES), 1) & half) == 0
    outs = []
    for j in range(t.shape[1] // LANES):
        tj = t[:, j * LANES:(j + 1) * LANES]
        partner = jnp.where(first, pltpu.roll(tj, LANES - half, 1), pltpu.roll(tj, half, 1))
        outs.append(tj * c + partner * ss)
    return outs[0] if len(outs) == 1 else jnp.concatenate(outs, axis=1)


IN_ROW_SLABS = 4
IN_TILE = 1024


def _pair_heads_by_kv(q):
    a, b = q[:, 0:LANES], q[:, LANES:2 * LANES]
    lo = lax.broadcasted_iota(jnp.int32, a.shape, 1) < HEAD_DIM
    return jnp.concatenate([jnp.where(lo, a, pltpu.roll(b, HEAD_DIM, 1)),
                            jnp.where(lo, pltpu.roll(a, HEAD_DIM, 1), b)], axis=1)


def _in_kernel(*refs, rope):
    x_ref, sh_ref, sc_ref, w_ref, qn_ref, kn_ref, gavg_ref = refs[:7]
    if rope:
        c2_ref, ss2_ref, c1_ref, ss1_ref, p_ref, u_ref, g_ref = refs[7:]
    else:
        p_ref, u_ref, g_ref = refs[7:]
    def rms(t, w, g):
        return t * lax.rsqrt(_group_mean(t * t, g) + NORM_EPS) * w

    scale = HEAD_DIM ** -0.5
    qscale = scale * LOG2E
    slabs = IN_ROW_SLABS if x_ref.shape[0] % (8 * IN_ROW_SLABS) == 0 else 1
    rows = x_ref.shape[0] // slabs
    for r in range(slabs):
        rs = slice(r * rows, (r + 1) * rows)

        def rope2(t):
            if not rope:
                return t
            return _rope_lanes(t, c2_ref[rs, :], ss2_ref[rs, :], HEAD_DIM // 4)

        def rope1(t):
            if not rope:
                return t
            return _rope_lanes(t, c1_ref[rs, :], ss1_ref[rs, :], HEAD_DIM // 2)

        h = _ln(x_ref[rs, :]) * (1.0 + sc_ref[...]) + sh_ref[...]
        y = jnp.dot(h.astype(BF16), w_ref[...], preferred_element_type=F32)
        qa = rope2(rms(y[:, 0:256], qn_ref[...], gavg_ref[...])) * qscale
        ka = rope2(rms(y[:, 256:384], kn_ref[...], gavg_ref[0:LANES, 0:LANES]))
        p_ref[rs, 0:256] = _pair_heads_by_kv(qa).astype(BF16)
        p_ref[rs, 256:384] = ka.astype(BF16)
        p_ref[rs, 384:512] = y[:, 384:512].astype(BF16)
        p_ref[rs, 512:768] = _pair_heads_by_kv(rope2(y[:, 512:768]) * qscale).astype(BF16)
        p_ref[rs, 768:896] = rope2(y[:, 768:896]).astype(BF16)
        p_ref[rs, 896:1024] = y[:, 896:1024].astype(BF16)
        u_ref[rs, :] = y[:, 1024:1280]
        p_ref[rs, 1024:1280] = rope1(y[:, 1280:1536]).astype(BF16)
        p_ref[rs, 1280:1536] = (rope1(y[:, 1536:1792]) * scale).astype(BF16)
        p_ref[rs, 1536:1792] = y[:, 1792:2048].astype(BF16)
        g_ref[rs, :] = y[:, 2048:2304]


def _in_proj(x, mod, mod_row, w, layer, qn, kn, gavg, tabs, tm):
    bsz, length, d = x.shape
    nw = w.shape[2]
    nt = length // tm
    rope = tabs is not None
    row = lambda b, i: (b, i, 0)
    const2 = lambda b, i: (0, 0)
    in_specs = [pl.BlockSpec((None, tm, d), row),
                pl.BlockSpec((None, None, 1, d), lambda b, i: (mod_row(b), 0, 0, 0)),
                pl.BlockSpec((None, None, 1, d), lambda b, i: (mod_row(b), 1, 0, 0)),
                pl.BlockSpec((None, d, nw), lambda b, i: (layer, 0, 0)),
                pl.BlockSpec((1, 256), const2),
                pl.BlockSpec((1, LANES), const2),
                pl.BlockSpec((256, 256), const2)]
    args = [x, mod, mod, w, qn, kn, gavg]
    if rope:
        in_specs += [pl.BlockSpec((tm, LANES), lambda b, i: (i, 0))] * len(tabs)
        args += list(tabs)
    return pl.pallas_call(
        functools.partial(_in_kernel, rope=rope),
        out_shape=(jax.ShapeDtypeStruct((bsz, length, P_COLS), BF16),
                   jax.ShapeDtypeStruct((bsz, length, 256), F32),
                   jax.ShapeDtypeStruct((bsz, length, 256), F32)),
        grid=(bsz, nt),
        in_specs=in_specs,
        out_specs=(pl.BlockSpec((None, tm, P_COLS), row),
                   pl.BlockSpec((None, tm, 256), row),
                   pl.BlockSpec((None, tm, 256), row)),
        compiler_params=_cparams(("arbitrary", "arbitrary")),
        name="in_proj_rope" if rope else "in_proj_ctx",
    )(*args)


def _stack_heads(q):
    qf = q.astype(F32)
    lo = lax.broadcasted_iota(jnp.int32, (q.shape[0], LANES), 1) < HEAD_DIM
    q0, q1 = qf[:, 0:LANES], qf[:, LANES:2 * LANES]
    z = jnp.zeros_like(q0)
    return jnp.concatenate([jnp.where(lo, q0, z), jnp.where(lo, q1, z),
                            jnp.where(lo, z, q0), jnp.where(lo, z, q1)], axis=0).astype(BF16)


def _aug_values(v):
    vf = v.astype(F32)
    lo = lax.broadcasted_iota(jnp.int32, vf.shape, 1) < HEAD_DIM
    one = jnp.ones_like(vf)
    return jnp.where(lo, vf, one).astype(BF16), jnp.where(lo, one, vf).astype(BF16)


def _finish_heads(acc0, acc1, e, tq):
    l0 = pltpu.roll(acc0, HEAD_DIM, 1)
    l1 = pltpu.roll(acc1, HEAD_DIM, 1)
    if e is not None:
        l0 = l0 + e[:2 * tq]
        l1 = l1 + e[2 * tq:]
    n0 = acc0 / l0
    n1 = acc1 / l1
    lo = lax.broadcasted_iota(jnp.int32, (tq, LANES), 1) < HEAD_DIM
    return jnp.concatenate([jnp.where(lo, n0[:tq], pltpu.roll(n0[tq:], HEAD_DIM, 1)),
                            jnp.where(lo, pltpu.roll(n1[:tq], HEAD_DIM, 1), n1[tq:])], axis=1)


def _sink_column(sink_ref, tq):
    return jnp.concatenate([jnp.full((tq, 1), sink_ref[h] * LOG2E, F32) for h in range(4)], axis=0)


ATTN_Q_TILE = 256
ATTN_K_CHUNK = 2048

def _attn_kernel(*refs, tq, tk, n_lat, has_sink):
    i = 0
    sink_ref = None
    if has_sink:
        sink_ref = refs[0]
        i = 1
    q_ref = refs[i]
    i += 1
    if n_lat:
        kl_ref, vl_ref = refs[i:i + 2]
        i += 2
    kc_ref, vc_ref, o_ref = refs[i:i + 3]
    i += 3
    if n_lat:
        v0l_ref, v1l_ref = refs[i:i + 2]
        i += 2
    v0c_ref, v1c_ref = refs[i:i + 2]

    @pl.when(pl.program_id(1) == 0)
    def _():
        if n_lat:
            a0, a1 = _aug_values(vl_ref[...])
            v0l_ref[...] = a0
            v1l_ref[...] = a1
        a0, a1 = _aug_values(vc_ref[...])
        v0c_ref[...] = a0
        v1c_ref[...] = a1

    qs = _stack_heads(q_ref[...])
    half = 2 * tq

    chunks = [(kl_ref, v0l_ref, v1l_ref, slice(c * tk, (c + 1) * tk)) for c in range(n_lat)]
    chunks.append((kc_ref, v0c_ref, v1c_ref, slice(None)))

    def scores(chunk):
        k_ref, _, _, rows = chunk
        return _dot_nt(qs, k_ref[rows, :])

    m = _sink_column(sink_ref, tq) if has_sink else jnp.full((4 * tq, 1), NEG_INF, F32)
    acc0 = jnp.zeros((half, LANES), F32)
    acc1 = jnp.zeros((half, LANES), F32)
    s_next = scores(chunks[0])
    for idx, (_, v0_ref, v1_ref, rows) in enumerate(chunks):
        s = s_next
        if idx + 1 < len(chunks):
            s_next = scores(chunks[idx + 1])
        m_new = jnp.maximum(m, jnp.max(s, axis=1, keepdims=True))
        alpha = jnp.exp2(m - m_new)
        p = jnp.exp2(s - m_new).astype(BF16)
        acc0 = acc0 * alpha[:half] + jnp.dot(p[:half], v0_ref[rows, :], preferred_element_type=F32)
        acc1 = acc1 * alpha[half:] + jnp.dot(p[half:], v1_ref[rows, :], preferred_element_type=F32)
        m = m_new
    e = jnp.exp2(_sink_column(sink_ref, tq) - m) if has_sink else None
    o_ref[...] = _finish_heads(acc0, acc1, e, tq).astype(BF16)


def _attention(pq, q_blk, p_lat, p_ctx, k_blk, v_blk, sink, tq, tk):
    bsz, lq, _ = pq.shape
    lc = p_ctx.shape[1]
    assert p_lat is None or p_lat.shape[1] % tk == 0
    n_lat = 0 if p_lat is None else p_lat.shape[1] // tk
    has_sink = sink is not None
    in_specs = [pl.BlockSpec((None, tq, 256), lambda b, i, *_: (b, i, q_blk))]
    args = [pq]
    scratch = []
    if n_lat:
        ll = p_lat.shape[1]
        in_specs += [pl.BlockSpec((None, ll, LANES), lambda b, i, *_: (b, 0, k_blk)),
                     pl.BlockSpec((None, ll, LANES), lambda b, i, *_: (b, 0, v_blk))]
        args += [p_lat, p_lat]
        scratch += [pltpu.VMEM((ll, LANES), BF16), pltpu.VMEM((ll, LANES), BF16)]
    in_specs += [pl.BlockSpec((None, lc, LANES), lambda b, i, *_: (b, 0, k_blk)),
                 pl.BlockSpec((None, lc, LANES), lambda b, i, *_: (b, 0, v_blk))]
    args += [p_ctx, p_ctx]
    scratch += [pltpu.VMEM((lc, LANES), BF16), pltpu.VMEM((lc, LANES), BF16)]
    kern = functools.partial(_attn_kernel, tq=tq, tk=tk, n_lat=n_lat, has_sink=has_sink)
    grid_spec = pltpu.PrefetchScalarGridSpec(
        num_scalar_prefetch=1 if has_sink else 0,
        grid=(bsz, lq // tq),
        in_specs=in_specs,
        out_specs=pl.BlockSpec((None, tq, 256), lambda b, i, *_: (b, i, 0)),
        scratch_shapes=scratch)
    call = pl.pallas_call(
        kern, out_shape=jax.ShapeDtypeStruct((bsz, lq, 256), BF16), grid_spec=grid_spec,
        compiler_params=_cparams(("arbitrary", "arbitrary")),
        name="attn_sink" if has_sink else ("attn_global" if n_lat else "attn_ctx"))
    return call(sink, *args) if has_sink else call(*args)


WIN_BLOCKS_PER_STEP = 8


def _win_kernel(sink_ref, q_ref, kl_ref, vl_ref, kc_ref, vc_ref, o_ref, *, nb):
    tq = Q_BLOCK
    r = lax.broadcasted_iota(jnp.int32, (4 * tq, tq), 0) & (tq - 1)
    j = lax.broadcasted_iota(jnp.int32, (4 * tq, tq), 1)
    in_prev = j >= r
    in_next = j <= r
    snk = _sink_column(sink_ref, tq)
    kc, vc = kc_ref[...], vc_ref[...]

    def rows(ref, blk):
        return ref[pl.ds(pl.multiple_of(blk * tq, tq), tq), :]

    def scores(t):
        i = pl.program_id(1) * WIN_BLOCKS_PER_STEP + t
        prev = jnp.maximum(i - 1, 0)
        nxt = jnp.minimum(i + 1, nb - 1)
        k = jnp.concatenate([rows(kl_ref, prev), rows(kl_ref, i), rows(kl_ref, nxt), kc], axis=0)
        qs = _stack_heads(q_ref[t * tq:(t + 1) * tq, :])
        return _dot_nt(qs, k), i, prev, nxt

    ahead = scores(0)
    for t in range(WIN_BLOCKS_PER_STEP):
        s, i, prev, nxt = ahead
        if t + 1 < WIN_BLOCKS_PER_STEP:
            ahead = scores(t + 1)
        v = jnp.concatenate([rows(vl_ref, prev), rows(vl_ref, i), rows(vl_ref, nxt), vc], axis=0)
        off_prev = jnp.where(i > 0, 0.0, NEG_INF)
        off_next = jnp.where(i < nb - 1, 0.0, NEG_INF)
        s = jnp.concatenate([jnp.where(in_prev, s[:, 0:tq] + off_prev, NEG_INF), s[:, tq:2 * tq],
                             jnp.where(in_next, s[:, 2 * tq:3 * tq] + off_next, NEG_INF), s[:, 3 * tq:]], axis=1)
        m = jnp.maximum(jnp.max(s, axis=1, keepdims=True), snk)
        p = jnp.exp2(s - m).astype(BF16)
        v0, v1 = _aug_values(v)
        acc0 = jnp.dot(p[:2 * tq], v0, preferred_element_type=F32)
        acc1 = jnp.dot(p[2 * tq:], v1, preferred_element_type=F32)
        o_ref[t * tq:(t + 1) * tq, :] = _finish_heads(acc0, acc1, jnp.exp2(snk - m), tq).astype(BF16)


def _window_attention(p_lat, p_ctx, sink):
    bsz, ll, _ = p_lat.shape
    lc = p_ctx.shape[1]
    nb = ll // Q_BLOCK
    tqs = WIN_BLOCKS_PER_STEP * Q_BLOCK
    assert ll % tqs == 0
    grid_spec = pltpu.PrefetchScalarGridSpec(
        num_scalar_prefetch=1,
        grid=(bsz, ll // tqs),
        in_specs=[pl.BlockSpec((None, tqs, 256), lambda b, i, *_: (b, i, 2)),
                  pl.BlockSpec((None, ll, LANES), lambda b, i, *_: (b, 0, 6)),
                  pl.BlockSpec((None, ll, LANES), lambda b, i, *_: (b, 0, 7)),
                  pl.BlockSpec((None, lc, LANES), lambda b, i, *_: (b, 0, 6)),
                  pl.BlockSpec((None, lc, LANES), lambda b, i, *_: (b, 0, 7))],
        out_specs=pl.BlockSpec((None, tqs, 256), lambda b, i, *_: (b, i, 0)))
    return pl.pallas_call(
        functools.partial(_win_kernel, nb=nb),
        out_shape=jax.ShapeDtypeStruct((bsz, ll, 256), BF16), grid_spec=grid_spec,
        compiler_params=_cparams(("arbitrary", "arbitrary")),
        name="attn_window",
    )(sink, p_lat, p_lat, p_lat, p_ctx, p_ctx)


FFT_ROWS = 8
FFT1_STEP_ROWS = 16


def _fft1_kernel(u_ref, w_ref, y_ref):
    l1, rows, w = u_ref.shape
    ys = []
    for h in range(rows // FFT_ROWS):
        u = u_ref[:, h * FFT_ROWS:(h + 1) * FFT_ROWS, :].reshape(l1 * FFT_ROWS, w).astype(BF16)
        ys.append(jnp.dot(w_ref[...], u, preferred_element_type=F32).reshape(2, l1, FFT_ROWS, w))
    y_ref[...] = jnp.concatenate(ys, axis=2).astype(BF16)


def _channel_mix(ab, g_ref, fm_ref):
    z = jnp.dot(ab.astype(BF16), g_ref[...], preferred_element_type=F32)
    return jnp.dot(z.astype(BF16), fm_ref[...], preferred_element_type=F32)


def _fft2_kernel(y_ref, c_ref, s_ref, g_ref, fm_ref, o_ref):
    l2 = y_ref.shape[2]
    ab = []
    for r in range(FFT_ROWS):
        yr, yi = y_ref[0, r], y_ref[1, r]
        cs = jnp.concatenate([c_ref[r], s_ref[r]], axis=1)
        rhs = jnp.concatenate([jnp.concatenate([yr, yi], axis=1),
                               jnp.concatenate([yi, -yr], axis=1)], axis=0)
        ab.append(jnp.dot(cs, rhs, preferred_element_type=F32))
    o = _channel_mix(jnp.concatenate(ab, axis=0), g_ref, fm_ref)
    for r in range(FFT_ROWS):
        o_ref[:, r, :] = o[r * l2:(r + 1) * l2]


def _fft_direct_kernel(u_ref, cs_ref, g_ref, fm_ref, o_ref):
    n = u_ref.shape[0]
    y = jnp.dot(cs_ref[...], u_ref[...].astype(BF16), preferred_element_type=F32)
    o_ref[...] = _channel_mix(jnp.concatenate([y[:n], y[n:]], axis=1), g_ref, fm_ref)


def _mxu_const(a):
    return jnp.asarray(a, F32).astype(BF16)


def _dft_tables(n_rows, n_cols, length, row_stride=1, row_offset=0):
    k = row_offset + row_stride * np.arange(n_rows, dtype=np.int64)
    n = np.arange(n_cols, dtype=np.int64)
    ang = 2.0 * np.pi * ((k[:, None] * n[None, :]) % length).astype(np.float64) / length
    return np.cos(ang), np.sin(ang)


def _channel_dft(width, length):
    c, s = _dft_tables(HEAD_DIM, HEAD_DIM, HEAD_DIM)
    eye = np.eye(width // HEAD_DIM) / np.sqrt(float(length) * HEAD_DIM)
    return _mxu_const(np.concatenate([np.kron(eye, c), np.kron(eye, s)], axis=0))


def _fourier_latent(u, fm_bd):
    bsz, length, w = u.shape
    l2 = FFT_L2
    l1 = length // l2
    rows = FFT_ROWS
    c1, s1 = _dft_tables(l1, l1, l1)
    w1 = _mxu_const(np.kron(np.concatenate([c1, -s1], axis=0), np.eye(rows)))
    y = pl.pallas_call(
        _fft1_kernel,
        out_shape=jax.ShapeDtypeStruct((bsz, 2, l1, l2, w), BF16),
        grid=(bsz, l2 // FFT1_STEP_ROWS),
        in_specs=[pl.BlockSpec((None, l1, FFT1_STEP_ROWS, w), lambda b, j: (b, 0, j, 0)),
                  pl.BlockSpec(w1.shape, lambda b, j: (0, 0))],
        out_specs=pl.BlockSpec((None, 2, l1, FFT1_STEP_ROWS, w), lambda b, j: (b, 0, 0, j, 0)),
        compiler_params=_cparams(("arbitrary", "arbitrary")),
        name="fourier_stage1",
    )(u.reshape(bsz, l1, l2, w), w1)
    tabs = [_dft_tables(l2, l2, length, row_stride=l1, row_offset=k1) for k1 in range(l1)]
    ck = _mxu_const(np.stack([t[0] for t in tabs]))
    sk = _mxu_const(np.stack([t[1] for t in tabs]))
    const2 = lambda b, k: (0, 0)
    out = pl.pallas_call(
        _fft2_kernel,
        out_shape=jax.ShapeDtypeStruct((bsz, l2, l1, w), F32),
        grid=(bsz, l1 // rows),
        in_specs=[pl.BlockSpec((None, 2, rows, l2, w), lambda b, k: (b, 0, k, 0, 0)),
                  pl.BlockSpec((rows, l2, l2), lambda b, k: (k, 0, 0)),
                  pl.BlockSpec((rows, l2, l2), lambda b, k: (k, 0, 0)),
                  pl.BlockSpec((2 * w, w), const2), pl.BlockSpec((w, w), const2)],
        out_specs=pl.BlockSpec((None, l2, rows, w), lambda b, k: (b, 0, k, 0)),
        compiler_params=_cparams(("arbitrary", "arbitrary")),
        name="fourier_stage2",
    )(y, ck, sk, _channel_dft(w, length), fm_bd)
    return out.reshape(bsz, length, w)


def _fourier_direct(u, fm_bd):
    bsz, length, w = u.shape
    c, s = _dft_tables(length, length, length)
    cs = _mxu_const(np.concatenate([c, -s], axis=0))
    const2 = lambda b: (0, 0)
    return pl.pallas_call(
        _fft_direct_kernel,
        out_shape=jax.ShapeDtypeStruct((bsz, length, w), F32),
        grid=(bsz,),
        in_specs=[pl.BlockSpec((None, length, w), lambda b: (b, 0, 0)),
                  pl.BlockSpec((2 * length, length), const2),
                  pl.BlockSpec((2 * w, w), const2), pl.BlockSpec((w, w), const2)],
        out_specs=pl.BlockSpec((None, length, w), lambda b: (b, 0, 0)),
        compiler_params=_cparams(("arbitrary",)),
        name="fourier_ctx",
    )(u, cs, _channel_dft(w, length), fm_bd)


def _log_sigmoid(x):
    return jnp.minimum(x, 0.0) - jnp.log1p(jnp.exp(-jnp.abs(x)))


def _ret_kernel(*refs, need_ctx):
    (rdl_ref, rdh_ref, qf_ref, kf_ref, vf_ref, qb_ref, kb_ref, vb_ref, qc_ref, kc_ref, vc_ref) = refs[:11]
    if need_ctx:
        of_ref, ob_ref, oc_ref = refs[11:14]
        scr = refs[14:]
    else:
        of_ref, ob_ref = refs[11:13]
        oc_ref = None
        scr = refs[13:]
    sf_ref, sb_ref, din_ref, tab_ref = scr
    c = RET_CHUNK
    w = 4 * HEAD_DIM
    j = pl.program_id(1)
    head_shift = HEAD_DIM.bit_length() - 1
    lane_head = lax.broadcasted_iota(jnp.int32, (c, w), 1) >> head_shift
    blockdiag = ((lax.broadcasted_iota(jnp.int32, (w, w), 0) >> head_shift)
                 == (lax.broadcasted_iota(jnp.int32, (w, w), 1) >> head_shift))

    def decayed_scores(q, k, d):
        qf = q.astype(F32)
        inner = []
        for h in range(4):
            qh = jnp.where(lane_head == h, qf, 0.0).astype(BF16)
            inner.append((_dot_nt(qh, k) * din_ref[d, h]).astype(BF16))
        return inner, (k.astype(F32) * tab_ref[d, 1]).T.astype(BF16)

    def chunk_outputs(scores, v):
        inner, kz = scores
        o = jnp.zeros((c, w), F32)
        for h in range(4):
            o = o + jnp.where(lane_head == h, jnp.dot(inner[h], v, preferred_element_type=F32), 0.0)
        kv = jnp.where(blockdiag, jnp.dot(kz, v, preferred_element_type=F32), 0.0)
        return o, kv

    def chunk(q, k, v, d):
        return chunk_outputs(decayed_scores(q, k, d), v)

    @pl.when(j == 0)
    def _():
        t = lax.broadcasted_iota(jnp.int32, (c, w), 0).astype(F32)
        rr = lax.broadcasted_iota(jnp.int32, (c, c), 0)
        cc = lax.broadcasted_iota(jnp.int32, (c, c), 1)
        for d in range(2):
            lg = _log_sigmoid(rdl_ref[d])
            tab_ref[d, 0] = jnp.exp(lg * ((t + 1.0) if d == 0 else (c - t)))
            tab_ref[d, 1] = jnp.exp(lg * ((c - 1.0 - t) if d == 0 else t))
            tab_ref[d, 2] = jnp.exp(jnp.broadcast_to(lg, (c, w)) * float(c))
            diff = (rr - cc) if d == 0 else (cc - rr)
            dpos = jnp.maximum(diff, 0).astype(F32)
            for h in range(4):
                lgh = _log_sigmoid(rdh_ref[d, h])
                din_ref[d, h] = jnp.where(diff >= 0, jnp.exp(lgh * dpos), 0.0)
        q, k, v = qc_ref[...], kc_ref[...], vc_ref[...]
        o_f, kv_f = chunk(q, k, v, 0)
        o_b, kv_b = chunk(q, k, v, 1)
        sf_ref[...] = kv_f
        sb_ref[...] = kv_b
        if need_ctx:
            oc_ref[...] = o_f + o_b

    @pl.when(j > 0)
    def _():
        n_sub = qf_ref.shape[0] // c
        io = ((qf_ref, kf_ref, vf_ref, of_ref), (qb_ref, kb_ref, vb_ref, ob_ref))
        states = [sf_ref[...], sb_ref[...]]
        work = [(d, t if d == 0 else n_sub - 1 - t) for t in range(n_sub) for d in range(2)]

        def start(d, t):
            rs = slice(t * c, (t + 1) * c)
            q, k, v = io[d][0][rs, :], io[d][1][rs, :], io[d][2][rs, :]
            return q, v, decayed_scores(q, k, d)

        ahead = start(*work[0])
        for idx, (d, t) in enumerate(work):
            q, v, scores = ahead
            if idx + 1 < len(work):
                ahead = start(*work[idx + 1])
            o, kv = chunk_outputs(scores, v)
            cross = jnp.dot(q, states[d].astype(BF16), preferred_element_type=F32) * tab_ref[d, 0]
            io[d][3][t * c:(t + 1) * c, :] = o + cross
            states[d] = states[d] * tab_ref[d, 2, 0:1, :] + kv
        sf_ref[...] = states[0]
        sb_ref[...] = states[1]


def _retention(p_lat, p_ctx, rdl, rdh, need_ctx):
    bsz, ll, _ = p_lat.shape
    lc = p_ctx.shape[1]
    c = RET_CHUNK
    w = 4 * HEAD_DIM
    cs = c * RET_STEP_CHUNKS
    assert lc == c and ll % cs == 0
    n = ll // cs
    fwd = lambda blk: (lambda b, j: (b, jnp.maximum(j - 1, 0), blk))
    bwd = lambda blk: (lambda b, j: (b, n - 1 - jnp.maximum(j - 1, 0), blk))
    ctx = lambda blk: (lambda b, j: (b, 0, blk))
    in_specs = [pl.BlockSpec((2, 1, w), lambda b, j: (0, 0, 0)),
                pl.BlockSpec((2, 4, 1, c), lambda b, j: (0, 0, 0, 0))]
    in_specs += [pl.BlockSpec((None, cs, w), fwd(blk)) for blk in (4, 5, 6)]
    in_specs += [pl.BlockSpec((None, cs, w), bwd(blk)) for blk in (4, 5, 6)]
    in_specs += [pl.BlockSpec((None, c, w), ctx(blk)) for blk in (4, 5, 6)]
    out_shape = [jax.ShapeDtypeStruct((bsz, ll, w), F32), jax.ShapeDtypeStruct((bsz, ll, w), F32)]
    out_specs = [pl.BlockSpec((None, cs, w), fwd(0)), pl.BlockSpec((None, cs, w), bwd(0))]
    if need_ctx:
        out_shape.append(jax.ShapeDtypeStruct((bsz, lc, w), F32))
        out_specs.append(pl.BlockSpec((None, c, w), ctx(0)))
    return pl.pallas_call(
        functools.partial(_ret_kernel, need_ctx=need_ctx),
        out_shape=tuple(out_shape),
        grid=(bsz, n + 1),
        in_specs=in_specs,
        out_specs=tuple(out_specs),
        scratch_shapes=[pltpu.VMEM((w, w), F32), pltpu.VMEM((w, w), F32),
                        pltpu.VMEM((2, 4, c, c), F32), pltpu.VMEM((2, 3, c, w), F32)],
        compiler_params=_cparams(("arbitrary", "arbitrary")),
        name="retention_ctx_out" if need_ctx else "retention",
    )(rdl, rdh, *([p_lat] * 6), *([p_ctx] * 3))


OUT_ROW_SLABS = 4
OUT_TILE = 1024


def _out_kernel(*refs, n_o, alpha):
    a_ref, b_ref, f_ref, g_ref = refs[:4]
    o_refs = refs[4:4 + n_o]
    x_ref, g1_ref, lnw_ref, lnb_ref, w_ref, gnw_ref, gavg_ref, out_ref = refs[4 + n_o:]
    gavg = gavg_ref[...]
    slabs = OUT_ROW_SLABS if x_ref.shape[0] % (16 * OUT_ROW_SLABS) == 0 else 1
    rows = x_ref.shape[0] // slabs
    def mixer_outputs(s):
        rs = slice(s * rows, (s + 1) * rows)
        o = o_refs[0][rs, :]
        for r in o_refs[1:]:
            o = o + r[rs, :]
        dlt = o - _group_mean(o, gavg)
        on = dlt * lax.rsqrt(_group_mean(dlt * dlt, gavg) + NORM_EPS) * gnw_ref[...]
        ret = (_silu(g_ref[rs, :]) * on).astype(BF16)
        return jnp.concatenate([a_ref[rs, :], b_ref[rs, :], f_ref[rs, :].astype(BF16), ret], axis=1)

    ahead = mixer_outputs(0)
    for s in range(slabs):
        rs = slice(s * rows, (s + 1) * rows)
        cat = ahead
        if s + 1 < slabs:
            ahead = mixer_outputs(s + 1)
        y = jnp.dot(cat, w_ref[...], preferred_element_type=F32)
        z = alpha * x_ref[rs, :] + g1_ref[...] * y
        out_ref[rs, :] = _ln(z) * lnw_ref[...] + lnb_ref[...]


def _out_proj(a, b, f, g, o_parts, x, mod, mod_row, lnw, lnb, w, layer, gnw, gavg, alpha, tm):
    bsz, length, d = x.shape
    row = lambda bb, i: (bb, i, 0)
    const2 = lambda bb, i: (0, 0)
    blk256 = pl.BlockSpec((None, tm, 256), row)
    in_specs = [blk256] * (4 + len(o_parts)) + [
        pl.BlockSpec((None, tm, d), row),
        pl.BlockSpec((None, None, 1, d), lambda bb, i: (mod_row(bb), 2, 0, 0)),
        pl.BlockSpec((1, d), const2), pl.BlockSpec((1, d), const2),
        pl.BlockSpec((None, d, d), lambda bb, i: (layer, 0, 0)),
        pl.BlockSpec((1, 256), const2), pl.BlockSpec((256, 256), const2)]
    return pl.pallas_call(
        functools.partial(_out_kernel, n_o=len(o_parts), alpha=alpha),
        out_shape=jax.ShapeDtypeStruct((bsz, length, d), F32),
        grid=(bsz, length // tm),
        in_specs=in_specs,
        out_specs=pl.BlockSpec((None, tm, d), row),
        compiler_params=_cparams(("arbitrary", "arbitrary")),
        name="out_proj",
    )(a, b, f, g, *o_parts, x, mod, lnw, lnb, w, gnw, gavg)


FFN_ROW_SLABS = 4
FFN_TILE = 1024


def _ffn_kernel(x_ref, sh_ref, sc_ref, g2_ref, lnw_ref, lnb_ref, wg_ref, wu_ref, wd_ref, out_ref, *, fc, alpha):
    rows = x_ref.shape[0] // FFN_ROW_SLABS
    n_chunks = wg_ref.shape[1] // fc
    work = [(r, c) for r in range(FFN_ROW_SLABS) for c in range(n_chunks)]
    xs, hs, accs = {}, {}, {}

    def gate_up(r, c):
        if r not in hs:
            xs[r] = x_ref[r * rows:(r + 1) * rows, :]
            hs[r] = (_ln(xs[r]) * (1.0 + sc_ref[...]) + sh_ref[...]).astype(BF16)
        cols = slice(c * fc, (c + 1) * fc)
        return (jnp.dot(hs[r], wg_ref[:, cols], preferred_element_type=F32),
                jnp.dot(hs[r], wu_ref[:, cols], preferred_element_type=F32))

    ahead = gate_up(*work[0])
    for idx, (r, c) in enumerate(work):
        gate, up = ahead
        if idx + 1 < len(work):
            ahead = gate_up(*work[idx + 1])
        act = (_silu(gate) * up).astype(BF16)
        down = jnp.dot(act, wd_ref[c * fc:(c + 1) * fc, :], preferred_element_type=F32)
        accs[r] = down if c == 0 else accs[r] + down
        if c == n_chunks - 1:
            z = alpha * xs[r] + g2_ref[...] * accs[r]
            out_ref[r * rows:(r + 1) * rows, :] = _ln(z) * lnw_ref[...] + lnb_ref[...]


def _ffn(x, mod, mod_row, lnw, lnb, wgu, wd, layer, alpha, tm):
    bsz, length, d = x.shape
    ff = wd.shape[1]
    fc = ff // 2 if (ff // 2) % LANES == 0 else ff
    row = lambda bb, i: (bb, i, 0)
    const2 = lambda bb, i: (0, 0)
    modspec = lambda which: pl.BlockSpec((None, None, 1, d), lambda bb, i: (mod_row(bb), which, 0, 0))
    resident = lambda shape, col: pl.BlockSpec(shape, lambda bb, i: (layer, 0, col), pipeline_mode=pl.Buffered(1))
    return pl.pallas_call(
        functools.partial(_ffn_kernel, fc=fc, alpha=alpha),
        out_shape=jax.ShapeDtypeStruct((bsz, length, d), F32),
        grid=(bsz, length // tm),
        in_specs=[pl.BlockSpec((None, tm, d), row), modspec(3), modspec(4), modspec(5),
                  pl.BlockSpec((1, d), const2), pl.BlockSpec((1, d), const2),
                  resident((None, d, ff), 0), resident((None, d, ff), 1), resident((None, ff, d), 0)],
        out_specs=pl.BlockSpec((None, tm, d), row),
        compiler_params=_cparams(("arbitrary", "arbitrary")),
        name="ffn",
    )(x, mod, mod, mod, lnw, lnb, wgu, wgu, wd)


def _rope_tables(seq):
    t = np.arange(seq)
    f32 = np.float32

    def tab(pos, n_freq):
        inv = f32(ROPE_THETA) ** (-np.arange(n_freq, dtype=f32) / f32(n_freq))
        ang = (pos.astype(f32)[:, None] * inv[None, :]).astype(np.float64)
        return np.cos(ang), np.sin(ang)

    cr, sr = tab(t // GRID_W, HEAD_DIM // 4)
    cc, sc = tab(t % GRID_W, HEAD_DIM // 4)
    ct, st = tab(t, HEAD_DIM // 2)
    tables = (np.concatenate([cr, cr, cc, cc], -1), np.concatenate([-sr, sr, -sc, sc], -1),
              np.concatenate([ct, ct], -1), np.concatenate([-st, st], -1))
    return tuple(jnp.asarray(np.tile(a, (1, 2)), F32) for a in tables)


def kernel(x, c, ctx, c_ctx, w_mod, b_mod, w_in, a_q_norm, a_k_norm, b_sink, f_mix, r_decay, r_gn_w, w_out,
           ln1_w, ln1_b, w_gate_up, w_down, ln2_w, ln2_b):
    bsz, seq, d = x.shape
    depth = w_in.shape[0]
    gw = d // 4
    assert gw == 4 * HEAD_DIM and a_q_norm.shape[-1] == HEAD_DIM and seq % (FFT_L2 * 8) == 0
    alpha = (2.0 * depth) ** 0.25

    tabs = _rope_tables(seq)
    gavg = jnp.asarray(np.kron(np.eye(gw // HEAD_DIM), np.full((HEAD_DIM, HEAD_DIM), 1.0 / HEAD_DIM)), BF16)
    cc = jnp.zeros((8, d), F32).at[:bsz].set(c).at[bsz].set(c_ctx)
    mod_all = _modulation(cc, w_mod, b_mod).reshape(depth, 8, 6, 1, d)
    lat_row = lambda b: b
    ctx_row = lambda b: bsz

    eye_g = jnp.eye(gw // HEAD_DIM, dtype=F32)
    w_in_b, w_out_b = w_in.astype(BF16), w_out.astype(BF16)
    w_gu_b, w_dn_b = w_gate_up.astype(BF16), w_down.astype(BF16)

    for layer in range(depth):
        need_ctx = layer < depth - 1
        mod = mod_all[layer]
        qn = jnp.tile(a_q_norm[layer], 4)[None, :]
        kn = jnp.tile(a_k_norm[layer], 2)[None, :]
        gnw = r_gn_w[layer][None, :]
        lnw1, lnb1 = ln1_w[layer][None, :], ln1_b[layer][None, :]
        lnw2, lnb2 = ln2_w[layer][None, :], ln2_b[layer][None, :]
        fm_bd = jnp.einsum('gh,gce->gche', eye_g, f_mix[layer]).reshape(gw, gw).astype(BF16)
        rd = r_decay[layer]
        rdl = jnp.repeat(rd, HEAD_DIM, axis=1)[:, None, :]
        rdh = jnp.broadcast_to(rd[:, :, None, None], (2, 4, 1, RET_CHUNK))
        sink = b_sink[layer]

        p_l, u_l, g_l = _in_proj(x, mod, lat_row, w_in_b, layer, qn, kn, gavg, tabs, min(IN_TILE, seq))
        lc = ctx.shape[1]
        flat = lambda t: t.reshape(1, bsz * lc, t.shape[-1])
        unflat = lambda t: t.reshape(bsz, lc, t.shape[-1])
        p_c, u_c, g_c = map(unflat, _in_proj(flat(ctx), mod, ctx_row, w_in_b, layer, qn, kn, gavg, None, bsz * lc))

        a_l = _attention(p_l, 0, p_l, p_c, 2, 3, None, tq=ATTN_Q_TILE, tk=min(ATTN_K_CHUNK, seq))
        b_l = _window_attention(p_l, p_c, sink)
        f_l = _fourier_latent(u_l, fm_bd)
        r_out = _retention(p_l, p_c, rdl, rdh, need_ctx)
        x = _out_proj(a_l, b_l, f_l, g_l, r_out[:2], x, mod, lat_row, lnw1, lnb1, w_out_b, layer, gnw, gavg, alpha,
                      min(OUT_TILE, seq))
        x = _ffn(x, mod, lat_row, lnw2, lnb2, w_gu_b, w_dn_b, layer, alpha, min(FFN_TILE, seq))
        if need_ctx:
            a_c = _attention(p_c, 0, None, p_c, 2, 3, None, tq=Q_BLOCK, tk=ATTN_K_CHUNK)
            b_c = _attention(p_c, 2, None, p_c, 6, 7, sink, tq=Q_BLOCK, tk=ATTN_K_CHUNK)
            f_c = _fourier_direct(u_c, fm_bd)
            ctx = _out_proj(flat(a_c), flat(b_c), flat(f_c), flat(g_c), (flat(r_out[2]),), flat(ctx), mod, ctx_row,
                            lnw1, lnb1, w_out_b, layer, gnw, gavg, alpha, bsz * lc)
            ctx = unflat(_ffn(ctx, mod, ctx_row, lnw2, lnb2, w_gu_b, w_dn_b, layer, alpha, bsz * lc))
    return x
```

```python
import functools

import numpy as np
import jax
import jax.numpy as jnp
from jax import lax
from jax.experimental import pallas as pl
from jax.experimental.pallas import tpu as pltpu

F32 = jnp.float32
BF16 = jnp.bfloat16

HEAD_DIM = 64
GRID_W = 64
Q_BLOCK = 128
ROPE_THETA = 10000.0
NORM_EPS = 1e-6
NEG_INF = -1e30
LOG2E = 1.4426950408889634

LANES = 128
VMEM_LIMIT_BYTES = 56 * 1024 * 1024

RET_CHUNK = 256
RET_STEP_CHUNKS = 4
FFT_L2 = 128


def _cparams(sem):
    return pltpu.CompilerParams(dimension_semantics=sem, vmem_limit_bytes=VMEM_LIMIT_BYTES)


def _ln(x):
    mu = jnp.mean(x, axis=-1, keepdims=True)
    xc = x - mu
    var = jnp.mean(xc * xc, axis=-1, keepdims=True)
    return xc * lax.rsqrt(var + NORM_EPS)


def _silu(x):
    return x * jax.nn.sigmoid(x)


def _group_mean(t, g):
    hi = t.astype(BF16)
    lo = (t - hi.astype(F32)).astype(BF16)
    return (jnp.dot(hi, g, preferred_element_type=F32) + jnp.dot(lo, g, preferred_element_type=F32))


def _dot_nt(a, b):
    return lax.dot_general(a, b, (((1,), (1,)), ((), ())), preferred_element_type=F32)


def _mod_kernel(c_ref, w_ref, b_ref, o_ref):
    h = _silu(c_ref[...])
    w = w_ref[...]
    h_hi = h.astype(BF16)
    h_lo = (h - h_hi.astype(F32)).astype(BF16)
    w_hi = w.astype(BF16)
    w_lo = (w - w_hi.astype(F32)).astype(BF16)
    rows = h.shape[0]
    both = jnp.dot(jnp.concatenate([h_hi, h_lo], axis=0), w_hi, preferred_element_type=F32)
    o_ref[...] = both[:rows] + both[rows:] + jnp.dot(h_hi, w_lo, preferred_element_type=F32) + b_ref[...]


def _modulation(cc, w_mod, b_mod):
    depth, d, n = w_mod.shape
    tn = 2048
    return pl.pallas_call(
        _mod_kernel,
        out_shape=jax.ShapeDtypeStruct((depth, 8, n), F32),
        grid=(depth, n // tn),
        in_specs=[pl.BlockSpec((8, d), lambda l, j: (0, 0)),
                  pl.BlockSpec((None, d, tn), lambda l, j: (l, 0, j)),
                  pl.BlockSpec((None, 1, tn), lambda l, j: (l, 0, j))],
        out_specs=pl.BlockSpec((None, 8, tn), lambda l, j: (l, 0, j)),
        compiler_params=_cparams(("arbitrary", "arbitrary")),
        name="modulation",
    )(cc, w_mod, b_mod.reshape(depth, 1, n))


P_COLS = 14 * LANES


def _rope_lanes(t, c, ss, half):
    first = (lax.broadcasted_iota(jnp.int32, (t.shape[0], LANES), 1) & half) == 0
    outs = []
    for j in range(t.shape[1] // LANES):
        tj = t[:, j * LANES:(j + 1) * LANES]
        partner = jnp.where(first, pltpu.roll(tj, LANES - half, 1), pltpu.roll(tj, half, 1))
        outs.append(tj * c + partner * ss)
    return outs[0] if len(outs) == 1 else jnp.concatenate(outs, axis=1)


IN_ROW_SLABS = 4
IN_TILE = 1024


def _pair_heads_by_kv(q):
    a, b = q[:, 0:LANES], q[:, LANES:2 * LANES]
    lo = lax.broadcasted_iota(jnp.int32, a.shape, 1) < HEAD_DIM
    return jnp.concatenate([jnp.where(lo, a, pltpu.roll(b, HEAD_DIM, 1)),
                            jnp.where(lo, pltpu.roll(a, HEAD_DIM, 1), b)], axis=1)


def _in_kernel(*refs, rope):
    x_ref, sh_ref, sc_ref, w_ref, qn_ref, kn_ref, gavg_ref = refs[:7]
    if rope:
        c2_ref, ss2_ref, c1_ref, ss1_ref, p_ref, u_ref, g_ref = refs[7:]
    else:
        p_ref, u_ref, g_ref = refs[7:]
    def rms(t, w, g):
        return t * lax.rsqrt(_group_mean(t * t, g) + NORM_EPS) * w

    scale = HEAD_DIM ** -0.5
    qscale = scale * LOG2E
    slabs = IN_ROW_SLABS if x_ref.shape[0] % (8 * IN_ROW_SLABS) == 0 else 1
    rows = x_ref.shape[0] // slabs
    for r in range(slabs):
        rs = slice(r * rows, (r + 1) * rows)

        def rope2(t):
            if not rope:
                return t
            return _rope_lanes(t, c2_ref[rs, :], ss2_ref[rs, :], HEAD_DIM // 4)

        def rope1(t):
            if not rope:
                return t
            return _rope_lanes(t, c1_ref[rs, :], ss1_ref[rs, :], HEAD_DIM // 2)

        h = _ln(x_ref[rs, :]) * (1.0 + sc_ref[...]) + sh_ref[...]
        y = jnp.dot(h.astype(BF16), w_ref[...], preferred_element_type=F32)
        qa = rope2(rms(y[:, 0:256], qn_ref[...], gavg_ref[...])) * qscale
        ka = rope2(rms(y[:, 256:384], kn_ref[...], gavg_ref[0:LANES, 0:LANES]))
        p_ref[rs, 0:256] = _pair_heads_by_kv(qa).astype(BF16)
        p_ref[rs, 256:384] = ka.astype(BF16)
        p_ref[rs, 384:512] = y[:, 384:512].astype(BF16)
        p_ref[rs, 512:768] = _pair_heads_by_kv(rope2(y[:, 512:768]) * qscale).astype(BF16)
        p_ref[rs, 768:896] = rope2(y[:, 768:896]).astype(BF16)
        p_ref[rs, 896:1024] = y[:, 896:1024].astype(BF16)
        u_ref[rs, :] = y[:, 1024:1280]
        p_ref[rs, 1024:1280] = rope1(y[:, 1280:1536]).astype(BF16)
        p_ref[rs, 1280:1536] = (rope1(y[:, 1536:1792]) * scale).astype(BF16)
        p_ref[rs, 1536:1792] = y[:, 1792:2048].astype(BF16)
        g_ref[rs, :] = y[:, 2048:2304]


def _in_proj(x, mod, mod_row, w, layer, qn, kn, gavg, tabs, tm):
    bsz, length, d = x.shape
    nw = w.shape[2]
    nt = length // tm
    rope = tabs is not None
    row = lambda b, i: (b, i, 0)
    const2 = lambda b, i: (0, 0)
    in_specs = [pl.BlockSpec((None, tm, d), row),
                pl.BlockSpec((None, None, 1, d), lambda b, i: (mod_row(b), 0, 0, 0)),
                pl.BlockSpec((None, None, 1, d), lambda b, i: (mod_row(b), 1, 0, 0)),
                pl.BlockSpec((None, d, nw), lambda b, i: (layer, 0, 0)),
                pl.BlockSpec((1, 256), const2),
                pl.BlockSpec((1, LANES), const2),
                pl.BlockSpec((256, 256), const2)]
    args = [x, mod, mod, w, qn, kn, gavg]
    if rope:
        in_specs += [pl.BlockSpec((tm, LANES), lambda b, i: (i, 0))] * len(tabs)
        args += list(tabs)
    return pl.pallas_call(
        functools.partial(_in_kernel, rope=rope),
        out_shape=(jax.ShapeDtypeStruct((bsz, length, P_COLS), BF16),
                   jax.ShapeDtypeStruct((bsz, length, 256), F32),
                   jax.ShapeDtypeStruct((bsz, length, 256), F32)),
        grid=(bsz, nt),
        in_specs=in_specs,
        out_specs=(pl.BlockSpec((None, tm, P_COLS), row),
                   pl.BlockSpec((None, tm, 256), row),
                   pl.BlockSpec((None, tm, 256), row)),
        compiler_params=_cparams(("arbitrary", "arbitrary")),
        name="in_proj_rope" if rope else "in_proj_ctx",
    )(*args)


def _stack_heads(q):
    qf = q.astype(F32)
    lo = lax.broadcasted_iota(jnp.int32, (q.shape[0], LANES), 1) < HEAD_DIM
    q0, q1 = qf[:, 0:LANES], qf[:, LANES:2 * LANES]
    z = jnp.zeros_like(q0)
    return jnp.concatenate([jnp.where(lo, q0, z), jnp.where(lo, q1, z),
                            jnp.where(lo, z, q0), jnp.where(lo, z, q1)], axis=0).astype(BF16)


def _aug_values(v):
    vf = v.astype(F32)
    lo = lax.broadcasted_iota(jnp.int32, vf.shape, 1) < HEAD_DIM
    one = jnp.ones_like(vf)
    return jnp.where(lo, vf, one).astype(BF16), jnp.where(lo, one, vf).astype(BF16)


def _finish_heads(acc0, acc1, e, tq):
    l0 = pltpu.roll(acc0, HEAD_DIM, 1)
    l1 = pltpu.roll(acc1, HEAD_DIM, 1)
    if e is not None:
        l0 = l0 + e[:2 * tq]
        l1 = l1 + e[2 * tq:]
    n0 = acc0 / l0
    n1 = acc1 / l1
    lo = lax.broadcasted_iota(jnp.int32, (tq, LANES), 1) < HEAD_DIM
    return jnp.concatenate([jnp.where(lo, n0[:tq], pltpu.roll(n0[tq:], HEAD_DIM, 1)),
                            jnp.where(lo, pltpu.roll(n1[:tq], HEAD_DIM, 1), n1[tq:])], axis=1)


def _sink_column(sink_ref, tq):
    return jnp.concatenate([jnp.full((tq, 1), sink_ref[h] * LOG2E, F32) for h in range(4)], axis=0)


ATTN_Q_TILE = 256
ATTN_K_CHUNK = 2048

def _attn_kernel(*refs, tq, tk, n_lat, has_sink):
    i = 0
    sink_ref = None
    if has_sink:
        sink_ref = refs[0]
        i = 1
    q_ref = refs[i]
    i += 1
    if n_lat:
        kl_ref, vl_ref = refs[i:i + 2]
        i += 2
    kc_ref, vc_ref, o_ref = refs[i:i + 3]
    i += 3
    if n_lat:
        v0l_ref, v1l_ref = refs[i:i + 2]
        i += 2
    v0c_ref, v1c_ref = refs[i:i + 2]

    @pl.when(pl.program_id(1) == 0)
    def _():
        if n_lat:
            a0, a1 = _aug_values(vl_ref[...])
            v0l_ref[...] = a0
            v1l_ref[...] = a1
        a0, a1 = _aug_values(vc_ref[...])
        v0c_ref[...] = a0
        v1c_ref[...] = a1

    qs = _stack_heads(q_ref[...])
    half = 2 * tq

    chunks = [(kl_ref, v0l_ref, v1l_ref, slice(c * tk, (c + 1) * tk)) for c in range(n_lat)]
    chunks.append((kc_ref, v0c_ref, v1c_ref, slice(None)))

    def scores(chunk):
        k_ref, _, _, rows = chunk
        return _dot_nt(qs, k_ref[rows, :])

    m = _sink_column(sink_ref, tq) if has_sink else jnp.full((4 * tq, 1), NEG_INF, F32)
    acc0 = jnp.zeros((half, LANES), F32)
    acc1 = jnp.zeros((half, LANES), F32)
    s_next = scores(chunks[0])
    for idx, (_, v0_ref, v1_ref, rows) in enumerate(chunks):
        s = s_next
        if idx + 1 < len(chunks):
            s_next = scores(chunks[idx + 1])
        m_new = jnp.maximum(m, jnp.max(s, axis=1, keepdims=True))
        alpha = jnp.exp2(m - m_new)
        p = jnp.exp2(s - m_new).astype(BF16)
        acc0 = acc0 * alpha[:half] + jnp.dot(p[:half], v0_ref[rows, :], preferred_element_type=F32)
        acc1 = acc1 * alpha[half:] + jnp.dot(p[half:], v1_ref[rows, :], preferred_element_type=F32)
        m = m_new
    e = jnp.exp2(_sink_column(sink_ref, tq) - m) if has_sink else None
    o_ref[...] = _finish_heads(acc0, acc1, e, tq).astype(BF16)


def _attention(pq, q_blk, p_lat, p_ctx, k_blk, v_blk, sink, tq, tk):
    bsz, lq, _ = pq.shape
    lc = p_ctx.shape[1]
    assert p_lat is None or p_lat.shape[1] % tk == 0
    n_lat = 0 if p_lat is None else p_lat.shape[1] // tk
    has_sink = sink is not None
    in_specs = [pl.BlockSpec((None, tq, 256), lambda b, i, *_: (b, i, q_blk))]
    args = [pq]
    scratch = []
    if n_lat:
        ll = p_lat.shape[1]
        in_specs += [pl.BlockSpec((None, ll, LANES), lambda b, i, *_: (b, 0, k_blk)),
                     pl.BlockSpec((None, ll, LANES), lambda b, i, *_: (b, 0, v_blk))]
        args += [p_lat, p_lat]
        scratch += [pltpu.VMEM((ll, LANES), BF16), pltpu.VMEM((ll, LANES), BF16)]
    in_specs += [pl.BlockSpec((None, lc, LANES), lambda b, i, *_: (b, 0, k_blk)),
                 pl.BlockSpec((None, lc, LANES), lambda b, i, *_: (b, 0, v_blk))]
    args += [p_ctx, p_ctx]
    scratch += [pltpu.VMEM((lc, LANES), BF16), pltpu.VMEM((lc, LANES), BF16)]
    kern = functools.partial(_attn_kernel, tq=tq, tk=tk, n_lat=n_lat, has_sink=has_sink)
    grid_spec = pltpu.PrefetchScalarGridSpec(
        num_scalar_prefetch=1 if has_sink else 0,
        grid=(bsz, lq // tq),
        in_specs=in_specs,
        out_specs=pl.BlockSpec((None, tq, 256), lambda b, i, *_: (b, i, 0)),
        scratch_shapes=scratch)
    call = pl.pallas_call(
        kern, out_shape=jax.ShapeDtypeStruct((bsz, lq, 256), BF16), grid_spec=grid_spec,
        compiler_params=_cparams(("arbitrary", "arbitrary")),
        name="attn_sink" if has_sink else ("attn_global" if n_lat else "attn_ctx"))
    return call(sink, *args) if has_sink else call(*args)


WIN_BLOCKS_PER_STEP = 8


def _win_kernel(sink_ref, q_ref, kl_ref, vl_ref, kc_ref, vc_ref, o_ref, *, nb):
    tq = Q_BLOCK
    r = lax.broadcasted_iota(jnp.int32, (4 * tq, tq), 0) & (tq - 1)
    j = lax.broadcasted_iota(jnp.int32, (4 * tq, tq), 1)
    in_prev = j >= r
    in_next = j <= r
    snk = _sink_column(sink_ref, tq)
    kc, vc = kc_ref[...], vc_ref[...]

    def rows(ref, blk):
        return ref[pl.ds(pl.multiple_of(blk * tq, tq), tq), :]

    def scores(t):
        i = pl.program_id(1) * WIN_BLOCKS_PER_STEP + t
        prev = jnp.maximum(i - 1, 0)
        nxt = jnp.minimum(i + 1, nb - 1)
        k = jnp.concatenate([rows(kl_ref, prev), rows(kl_ref, i), rows(kl_ref, nxt), kc], axis=0)
        qs = _stack_heads(q_ref[t * tq:(t + 1) * tq, :])
        return _dot_nt(qs, k), i, prev, nxt

    ahead = scores(0)
    for t in range(WIN_BLOCKS_PER_STEP):
        s, i, prev, nxt = ahead
        if t + 1 < WIN_BLOCKS_PER_STEP:
            ahead = scores(t + 1)
        v = jnp.concatenate([rows(vl_ref, prev), rows(vl_ref, i), rows(vl_ref, nxt), vc], axis=0)
        off_prev = jnp.where(i > 0, 0.0, NEG_INF)
        off_next = jnp.where(i < nb - 1, 0.0, NEG_INF)
        s = jnp.concatenate([jnp.where(in_prev, s[:, 0:tq] + off_prev, NEG_INF), s[:, tq:2 * tq],
                             jnp.where(in_next, s[:, 2 * tq:3 * tq] + off_next, NEG_INF), s[:, 3 * tq:]], axis=1)
        m = jnp.maximum(jnp.max(s, axis=1, keepdims=True), snk)
        p = jnp.exp2(s - m).astype(BF16)
        v0, v1 = _aug_values(v)
        acc0 = jnp.dot(p[:2 * tq], v0, preferred_element_type=F32)
        acc1 = jnp.dot(p[2 * tq:], v1, preferred_element_type=F32)
        o_ref[t * tq:(t + 1) * tq, :] = _finish_heads(acc0, acc1, jnp.exp2(snk - m), tq).astype(BF16)


def _window_attention(p_lat, p_ctx, sink):
    bsz, ll, _ = p_lat.shape
    lc = p_ctx.shape[1]
    nb = ll // Q_BLOCK
    tqs = WIN_BLOCKS_PER_STEP * Q_BLOCK
    assert ll % tqs == 0
    grid_spec = pltpu.PrefetchScalarGridSpec(
        num_scalar_prefetch=1,
        grid=(bsz, ll // tqs),
        in_specs=[pl.BlockSpec((None, tqs, 256), lambda b, i, *_: (b, i, 2)),
                  pl.BlockSpec((None, ll, LANES), lambda b, i, *_: (b, 0, 6)),
                  pl.BlockSpec((None, ll, LANES), lambda b, i, *_: (b, 0, 7)),
                  pl.BlockSpec((None, lc, LANES), lambda b, i, *_: (b, 0, 6)),
                  pl.BlockSpec((None, lc, LANES), lambda b, i, *_: (b, 0, 7))],
        out_specs=pl.BlockSpec((None, tqs, 256), lambda b, i, *_: (b, i, 0)))
    return pl.pallas_call(
        functools.partial(_win_kernel, nb=nb),
        out_shape=jax.ShapeDtypeStruct((bsz, ll, 256), BF16), grid_spec=grid_spec,
        compiler_params=_cparams(("arbitrary", "arbitrary")),
        name="attn_window",
    )(sink, p_lat, p_lat, p_lat, p_ctx, p_ctx)


FFT_ROWS = 8
FFT1_STEP_ROWS = 16


def _fft1_kernel(u_ref, w_ref, y_ref):
    l1, rows, w = u_ref.shape
    ys = []
    for h in range(rows // FFT_ROWS):
        u = u_ref[:, h * FFT_ROWS:(h + 1) * FFT_ROWS, :].reshape(l1 * FFT_ROWS, w).astype(BF16)
        ys.append(jnp.dot(w_ref[...], u, preferred_element_type=F32).reshape(2, l1, FFT_ROWS, w))
    y_ref[...] = jnp.concatenate(ys, axis=2).astype(BF16)


def _channel_mix(ab, g_ref, fm_ref):
    z = jnp.dot(ab.astype(BF16), g_ref[...], preferred_element_type=F32)
    return jnp.dot(z.astype(BF16), fm_ref[...], preferred_element_type=F32)


def _fft2_kernel(y_ref, c_ref, s_ref, g_ref, fm_ref, o_ref):
    l2 = y_ref.shape[2]
    ab = []
    for r in range(FFT_ROWS):
        yr, yi = y_ref[0, r], y_ref[1, r]
        cs = jnp.concatenate([c_ref[r], s_ref[r]], axis=1)
        rhs = jnp.concatenate([jnp.concatenate([yr, yi], axis=1),
                               jnp.concatenate([yi, -yr], axis=1)], axis=0)
        ab.append(jnp.dot(cs, rhs, preferred_element_type=F32))
    o = _channel_mix(jnp.concatenate(ab, axis=0), g_ref, fm_ref)
    for r in range(FFT_ROWS):
        o_ref[:, r, :] = o[r * l2:(r + 1) * l2]


def _fft_direct_kernel(u_ref, cs_ref, g_ref, fm_ref, o_ref):
    n = u_ref.shape[0]
    y = jnp.dot(cs_ref[...], u_ref[...].astype(BF16), preferred_element_type=F32)
    o_ref[...] = _channel_mix(jnp.concatenate([y[:n], y[n:]], axis=1), g_ref, fm_ref)


def _mxu_const(a):
    return jnp.asarray(a, F32).astype(BF16)


def _dft_tables(n_rows, n_cols, length, row_stride=1, row_offset=0):
    k = row_offset + row_stride * np.arange(n_rows, dtype=np.int64)
    n = np.arange(n_cols, dtype=np.int64)
    ang = 2.0 * np.pi * ((k[:, None] * n[None, :]) % length).astype(np.float64) / length
    return np.cos(ang), np.sin(ang)


def _channel_dft(width, length):
    c, s = _dft_tables(HEAD_DIM, HEAD_DIM, HEAD_DIM)
    eye = np.eye(width // HEAD_DIM) / np.sqrt(float(length) * HEAD_DIM)
    return _mxu_const(np.concatenate([np.kron(eye, c), np.kron(eye, s)], axis=0))


def _fourier_latent(u, fm_bd):
    bsz, length, w = u.shape
    l2 = FFT_L2
    l1 = length // l2
    rows = FFT_ROWS
    c1, s1 = _dft_tables(l1, l1, l1)
    w1 = _mxu_const(np.kron(np.concatenate([c1, -s1], axis=0), np.eye(rows)))
    y = pl.pallas_call(
        _fft1_kernel,
        out_shape=jax.ShapeDtypeStruct((bsz, 2, l1, l2, w), BF16),
        grid=(bsz, l2 // FFT1_STEP_ROWS),
        in_specs=[pl.BlockSpec((None, l1, FFT1_STEP_ROWS, w), lambda b, j: (b, 0, j, 0)),
                  pl.BlockSpec(w1.shape, lambda b, j: (0, 0))],
        out_specs=pl.BlockSpec((None, 2, l1, FFT1_STEP_ROWS, w), lambda b, j: (b, 0, 0, j, 0)),
        compiler_params=_cparams(("arbitrary", "arbitrary")),
        name="fourier_stage1",
    )(u.reshape(bsz, l1, l2, w), w1)
    tabs = [_dft_tables(l2, l2, length, row_stride=l1, row_offset=k1) for k1 in range(l1)]
    ck = _mxu_const(np.stack([t[0] for t in tabs]))
    sk = _mxu_const(np.stack([t[1] for t in tabs]))
    const2 = lambda b, k: (0, 0)
    out = pl.pallas_call(
        _fft2_kernel,
        out_shape=jax.ShapeDtypeStruct((bsz, l2, l1, w), F32),
        grid=(bsz, l1 // rows),
        in_specs=[pl.BlockSpec((None, 2, rows, l2, w), lambda b, k: (b, 0, k, 0, 0)),
                  pl.BlockSpec((rows, l2, l2), lambda b, k: (k, 0, 0)),
                  pl.BlockSpec((rows, l2, l2), lambda b, k: (k, 0, 0)),
                  pl.BlockSpec((2 * w, w), const2), pl.BlockSpec((w, w), const2)],
        out_specs=pl.BlockSpec((None, l2, rows, w), lambda b, k: (b, 0, k, 0)),
        compiler_params=_cparams(("arbitrary", "arbitrary")),
        name="fourier_stage2",
    )(y, ck, sk, _channel_dft(w, length), fm_bd)
    return out.reshape(bsz, length, w)


def _fourier_direct(u, fm_bd):
    bsz, length, w = u.shape
    c, s = _dft_tables(length, length, length)
    cs = _mxu_const(np.concatenate([c, -s], axis=0))
    const2 = lambda b: (0, 0)
    return pl.pallas_call(
        _fft_direct_kernel,
        out_shape=jax.ShapeDtypeStruct((bsz, length, w), F32),
        grid=(bsz,),
        in_specs=[pl.BlockSpec((None, length, w), lambda b: (b, 0, 0)),
                  pl.BlockSpec((2 * length, length), const2),
                  pl.BlockSpec((2 * w, w), const2), pl.BlockSpec((w, w), const2)],
        out_specs=pl.BlockSpec((None, length, w), lambda b: (b, 0, 0)),
        compiler_params=_cparams(("arbitrary",)),
        name="fourier_ctx",
    )(u, cs, _channel_dft(w, length), fm_bd)


def _log_sigmoid(x):
    return jnp.minimum(x, 0.0) - jnp.log1p(jnp.exp(-jnp.abs(x)))


def _ret_kernel(*refs, need_ctx):
    (rdl_ref, rdh_ref, qf_ref, kf_ref, vf_ref, qb_ref, kb_ref, vb_ref, qc_ref, kc_ref, vc_ref) = refs[:11]
    if need_ctx:
        of_ref, ob_ref, oc_ref = refs[11:14]
        scr = refs[14:]
    else:
        of_ref, ob_ref = refs[11:13]
        oc_ref = None
        scr = refs[13:]
    sf_ref, sb_ref, din_ref, tab_ref = scr
    c = RET_CHUNK
    w = 4 * HEAD_DIM
    j = pl.program_id(1)
    head_shift = HEAD_DIM.bit_length() - 1
    lane_head = lax.broadcasted_iota(jnp.int32, (c, w), 1) >> head_shift
    blockdiag = ((lax.broadcasted_iota(jnp.int32, (w, w), 0) >> head_shift)
                 == (lax.broadcasted_iota(jnp.int32, (w, w), 1) >> head_shift))

    def decayed_scores(q, k, d):
        qf = q.astype(F32)
        inner = []
        for h in range(4):
            qh = jnp.where(lane_head == h, qf, 0.0).astype(BF16)
            inner.append((_dot_nt(qh, k) * din_ref[d, h]).astype(BF16))
        return inner, (k.astype(F32) * tab_ref[d, 1]).T.astype(BF16)

    def chunk_outputs(scores, v):
        inner, kz = scores
        o = jnp.zeros((c, w), F32)
        for h in range(4):
            o = o + jnp.where(lane_head == h, jnp.dot(inner[h], v, preferred_element_type=F32), 0.0)
        kv = jnp.where(blockdiag, jnp.dot(kz, v, preferred_element_type=F32), 0.0)
        return o, kv

    def chunk(q, k, v, d):
        return chunk_outputs(decayed_scores(q, k, d), v)

    @pl.when(j == 0)
    def _():
        t = lax.broadcasted_iota(jnp.int32, (c, w), 0).astype(F32)
        rr = lax.broadcasted_iota(jnp.int32, (c, c), 0)
        cc = lax.broadcasted_iota(jnp.int32, (c, c), 1)
        for d in range(2):
            lg = _log_sigmoid(rdl_ref[d])
            tab_ref[d, 0] = jnp.exp(lg * ((t + 1.0) if d == 0 else (c - t)))
            tab_ref[d, 1] = jnp.exp(lg * ((c - 1.0 - t) if d == 0 else t))
            tab_ref[d, 2] = jnp.exp(jnp.broadcast_to(lg, (c, w)) * float(c))
            diff = (rr - cc) if d == 0 else (cc - rr)
            dpos = jnp.maximum(diff, 0).astype(F32)
            for h in range(4):
                lgh = _log_sigmoid(rdh_ref[d, h])
                din_ref[d, h] = jnp.where(diff >= 0, jnp.exp(lgh * dpos), 0.0)
        q, k, v = qc_ref[...], kc_ref[...], vc_ref[...]
        o_f, kv_f = chunk(q, k, v, 0)
        o_b, kv_b = chunk(q, k, v, 1)
        sf_ref[...] = kv_f
        sb_ref[...] = kv_b
        if need_ctx:
            oc_ref[...] = o_f + o_b

    @pl.when(j > 0)
    def _():
        n_sub = qf_ref.shape[0] // c
        io = ((qf_ref, kf_ref, vf_ref, of_ref), (qb_ref, kb_ref, vb_ref, ob_ref))
        states = [sf_ref[...], sb_ref[...]]
        work = [(d, t if d == 0 else n_sub - 1 - t) for t in range(n_sub) for d in range(2)]

        def start(d, t):
            rs = slice(t * c, (t + 1) * c)
            q, k, v = io[d][0][rs, :], io[d][1][rs, :], io[d][2][rs, :]
            return q, v, decayed_scores(q, k, d)

        ahead = start(*work[0])
        for idx, (d, t) in enumerate(work):
            q, v, scores = ahead
            if idx + 1 < len(work):
                ahead = start(*work[idx + 1])
            o, kv = chunk_outputs(scores, v)
            cross = jnp.dot(q, states[d].astype(BF16), preferred_element_type=F32) * tab_ref[d, 0]
            io[d][3][t * c:(t + 1) * c, :] = o + cross
            states[d] = states[d] * tab_ref[d, 2, 0:1, :] + kv
        sf_ref[...] = states[0]
        sb_ref[...] = states[1]


def _retention(p_lat, p_ctx, rdl, rdh, need_ctx):
    bsz, ll, _ = p_lat.shape
    lc = p_ctx.shape[1]
    c = RET_CHUNK
    w = 4 * HEAD_DIM
    cs = c * RET_STEP_CHUNKS
    assert lc == c and ll % cs == 0
    n = ll // cs
    fwd = lambda blk: (lambda b, j: (b, jnp.maximum(j - 1, 0), blk))
    bwd = lambda blk: (lambda b, j: (b, n - 1 - jnp.maximum(j - 1, 0), blk))
    ctx = lambda blk: (lambda b, j: (b, 0, blk))
    in_specs = [pl.BlockSpec((2, 1, w), lambda b, j: (0, 0, 0)),
                pl.BlockSpec((2, 4, 1, c), lambda b, j: (0, 0, 0, 0))]
    in_specs += [pl.BlockSpec((None, cs, w), fwd(blk)) for blk in (4, 5, 6)]
    in_specs += [pl.BlockSpec((None, cs, w), bwd(blk)) for blk in (4, 5, 6)]
    in_specs += [pl.BlockSpec((None, c, w), ctx(blk)) for blk in (4, 5, 6)]
    out_shape = [jax.ShapeDtypeStruct((bsz, ll, w), F32), jax.ShapeDtypeStruct((bsz, ll, w), F32)]
    out_specs = [pl.BlockSpec((None, cs, w), fwd(0)), pl.BlockSpec((None, cs, w), bwd(0))]
    if need_ctx:
        out_shape.append(jax.ShapeDtypeStruct((bsz, lc, w), F32))
        out_specs.append(pl.BlockSpec((None, c, w), ctx(0)))
    return pl.pallas_call(
        functools.partial(_ret_kernel, need_ctx=need_ctx),
        out_shape=tuple(out_shape),
        grid=(bsz, n + 1),
        in_specs=in_specs,
        out_specs=tuple(out_specs),
        scratch_shapes=[pltpu.VMEM((w, w), F32), pltpu.VMEM((w, w), F32),
                        pltpu.VMEM((2, 4, c, c), F32), pltpu.VMEM((2, 3, c, w), F32)],
        compiler_params=_cparams(("arbitrary", "arbitrary")),
        name="retention_ctx_out" if need_ctx else "retention",
    )(rdl, rdh, *([p_lat] * 6), *([p_ctx] * 3))


OUT_ROW_SLABS = 4
OUT_TILE = 1024


def _out_kernel(*refs, n_o, alpha):
    a_ref, b_ref, f_ref, g_ref = refs[:4]
    o_refs = refs[4:4 + n_o]
    x_ref, g1_ref, lnw_ref, lnb_ref, w_ref, gnw_ref, gavg_ref, out_ref = refs[4 + n_o:]
    gavg = gavg_ref[...]
    slabs = OUT_ROW_SLABS if x_ref.shape[0] % (16 * OUT_ROW_SLABS) == 0 else 1
    rows = x_ref.shape[0] // slabs
    def mixer_outputs(s):
        rs = slice(s * rows, (s + 1) * rows)
        o = o_refs[0][rs, :]
        for r in o_refs[1:]:
            o = o + r[rs, :]
        dlt = o - _group_mean(o, gavg)
        on = dlt * lax.rsqrt(_group_mean(dlt * dlt, gavg) + NORM_EPS) * gnw_ref[...]
        ret = (_silu(g_ref[rs, :]) * on).astype(BF16)
        return jnp.concatenate([a_ref[rs, :], b_ref[rs, :], f_ref[rs, :].astype(BF16), ret], axis=1)

    ahead = mixer_outputs(0)
    for s in range(slabs):
        rs = slice(s * rows, (s + 1) * rows)
        cat = ahead
        if s + 1 < slabs:
            ahead = mixer_outputs(s + 1)
        y = jnp.dot(cat, w_ref[...], preferred_element_type=F32)
        z = alpha * x_ref[rs, :] + g1_ref[...] * y
        out_ref[rs, :] = _ln(z) * lnw_ref[...] + lnb_ref[...]


def _out_proj(a, b, f, g, o_parts, x, mod, mod_row, lnw, lnb, w, layer, gnw, gavg, alpha, tm):
    bsz, length, d = x.shape
    row = lambda bb, i: (bb, i, 0)
    const2 = lambda bb, i: (0, 0)
    blk256 = pl.BlockSpec((None, tm, 256), row)
    in_specs = [blk256] * (4 + len(o_parts)) + [
        pl.BlockSpec((None, tm, d), row),
        pl.BlockSpec((None, None, 1, d), lambda bb, i: (mod_row(bb), 2, 0, 0)),
        pl.BlockSpec((1, d), const2), pl.BlockSpec((1, d), const2),
        pl.BlockSpec((None, d, d), lambda bb, i: (layer, 0, 0)),
        pl.BlockSpec((1, 256), const2), pl.BlockSpec((256, 256), const2)]
    return pl.pallas_call(
        functools.partial(_out_kernel, n_o=len(o_parts), alpha=alpha),
        out_shape=jax.ShapeDtypeStruct((bsz, length, d), F32),
        grid=(bsz, length // tm),
        in_specs=in_specs,
        out_specs=pl.BlockSpec((None, tm, d), row),
        compiler_params=_cparams(("arbitrary", "arbitrary")),
        name="out_proj",
    )(a, b, f, g, *o_parts, x, mod, lnw, lnb, w, gnw, gavg)


FFN_ROW_SLABS = 4
FFN_TILE = 1024


def _ffn_kernel(x_ref, sh_ref, sc_ref, g2_ref, lnw_ref, lnb_ref, wg_ref, wu_ref, wd_ref, out_ref, *, fc, alpha):
    rows = x_ref.shape[0] // FFN_ROW_SLABS
    n_chunks = wg_ref.shape[1] // fc
    work = [(r, c) for r in range(FFN_ROW_SLABS) for c in range(n_chunks)]
    xs, hs, accs = {}, {}, {}

    def gate_up(r, c):
        if r not in hs:
            xs[r] = x_ref[r * rows:(r + 1) * rows, :]
            hs[r] = (_ln(xs[r]) * (1.0 + sc_ref[...]) + sh_ref[...]).astype(BF16)
        cols = slice(c * fc, (c + 1) * fc)
        return (jnp.dot(hs[r], wg_ref[:, cols], preferred_element_type=F32),
                jnp.dot(hs[r], wu_ref[:, cols], preferred_element_type=F32))

    ahead = gate_up(*work[0])
    for idx, (r, c) in enumerate(work):
        gate, up = ahead
        if idx + 1 < len(work):
            ahead = gate_up(*work[idx + 1])
        act = (_silu(gate) * up).astype(BF16)
        down = jnp.dot(act, wd_ref[c * fc:(c + 1) * fc, :], preferred_element_type=F32)
        accs[r] = down if c == 0 else accs[r] + down
        if c == n_chunks - 1:
            z = alpha * xs[r] + g2_ref[...] * accs[r]
            out_ref[r * rows:(r + 1) * rows, :] = _ln(z) * lnw_ref[...] + lnb_ref[...]


def _ffn(x, mod, mod_row, lnw, lnb, wgu, wd, layer, alpha, tm):
    bsz, length, d = x.shape
    ff = wd.shape[1]
    fc = ff // 2 if (ff // 2) % LANES == 0 else ff
    row = lambda bb, i: (bb, i, 0)
    const2 = lambda bb, i: (0, 0)
    modspec = lambda which: pl.BlockSpec((None, None, 1, d), lambda bb, i: (mod_row(bb), which, 0, 0))
    resident = lambda shape, col: pl.BlockSpec(shape, lambda bb, i: (layer, 0, col), pipeline_mode=pl.Buffered(1))
    return pl.pallas_call(
        functools.partial(_ffn_kernel, fc=fc, alpha=alpha),
        out_shape=jax.ShapeDtypeStruct((bsz, length, d), F32),
        grid=(bsz, length // tm),
        in_specs=[pl.BlockSpec((None, tm, d), row), modspec(3), modspec(4), modspec(5),
                  pl.BlockSpec((1, d), const2), pl.BlockSpec((1, d), const2),
                  resident((None, d, ff), 0), resident((None, d, ff), 1), resident((None, ff, d), 0)],
        out_specs=pl.BlockSpec((None, tm, d), row),
        compiler_params=_cparams(("arbitrary", "arbitrary")),
        name="ffn",
    )(x, mod, mod, mod, lnw, lnb, wgu, wgu, wd)


def _rope_tables(seq):
    t = np.arange(seq)
    f32 = np.float32

    def tab(pos, n_freq):
        inv = f32(ROPE_THETA) ** (-np.arange(n_freq, dtype=f32) / f32(n_freq))
        ang = (pos.astype(f32)[:, None] * inv[None, :]).astype(np.float64)
        return np.cos(ang), np.sin(ang)

    cr, sr = tab(t // GRID_W, HEAD_DIM // 4)
    cc, sc = tab(t % GRID_W, HEAD_DIM // 4)
    ct, st = tab(t, HEAD_DIM // 2)
    tables = (np.concatenate([cr, cr, cc, cc], -1), np.concatenate([-sr, sr, -sc, sc], -1),
              np.concatenate([ct, ct], -1), np.concatenate([-st, st], -1))
    return tuple(jnp.asarray(np.tile(a, (1, 2)), F32) for a in tables)


def kernel(x, c, ctx, c_ctx, w_mod, b_mod, w_in, a_q_norm, a_k_norm, b_sink, f_mix, r_decay, r_gn_w, w_out,
           ln1_w, ln1_b, w_gate_up, w_down, ln2_w, ln2_b):
    bsz, seq, d = x.shape
    depth = w_in.shape[0]
    gw = d // 4
    assert gw == 4 * HEAD_DIM and a_q_norm.shape[-1] == HEAD_DIM and seq % (FFT_L2 * 8) == 0
    alpha = (2.0 * depth) ** 0.25

    tabs = _rope_tables(seq)
    gavg = jnp.asarray(np.kron(np.eye(gw // HEAD_DIM), np.full((HEAD_DIM, HEAD_DIM), 1.0 / HEAD_DIM)), BF16)
    cc = jnp.zeros((8, d), F32).at[:bsz].set(c).at[bsz].set(c_ctx)
    mod_all = _modulation(cc, w_mod, b_mod).reshape(depth, 8, 6, 1, d)
    lat_row = lambda b: b
    ctx_row = lambda b: bsz

    eye_g = jnp.eye(gw // HEAD_DIM, dtype=F32)
    w_in_b, w_out_b = w_in.astype(BF16), w_out.astype(BF16)
    w_gu_b, w_dn_b = w_gate_up.astype(BF16), w_down.astype(BF16)

    for layer in range(depth):
        need_ctx = layer < depth - 1
        mod = mod_all[layer]
        qn = jnp.tile(a_q_norm[layer], 4)[None, :]
        kn = jnp.tile(a_k_norm[layer], 2)[None, :]
        gnw = r_gn_w[layer][None, :]
        lnw1, lnb1 = ln1_w[layer][None, :], ln1_b[layer][None, :]
        lnw2, lnb2 = ln2_w[layer][None, :], ln2_b[layer][None, :]
        fm_bd = jnp.einsum('gh,gce->gche', eye_g, f_mix[layer]).reshape(gw, gw).astype(BF16)
        rd = r_decay[layer]
        rdl = jnp.repeat(rd, HEAD_DIM, axis=1)[:, None, :]
        rdh = jnp.broadcast_to(rd[:, :, None, None], (2, 4, 1, RET_CHUNK))
        sink = b_sink[layer]

        p_l, u_l, g_l = _in_proj(x, mod, lat_row, w_in_b, layer, qn, kn, gavg, tabs, min(IN_TILE, seq))
        lc = ctx.shape[1]
        flat = lambda t: t.reshape(1, bsz * lc, t.shape[-1])
        unflat = lambda t: t.reshape(bsz, lc, t.shape[-1])
        p_c, u_c, g_c = map(unflat, _in_proj(flat(ctx), mod, ctx_row, w_in_b, layer, qn, kn, gavg, None, bsz * lc))

        a_l = _attention(p_l, 0, p_l, p_c, 2, 3, None, tq=ATTN_Q_TILE, tk=min(ATTN_K_CHUNK, seq))
        b_l = _window_attention(p_l, p_c, sink)
        f_l = _fourier_latent(u_l, fm_bd)
        r_out = _retention(p_l, p_c, rdl, rdh, need_ctx)
        x = _out_proj(a_l, b_l, f_l, g_l, r_out[:2], x, mod, lat_row, lnw1, lnb1, w_out_b, layer, gnw, gavg, alpha,
                      min(OUT_TILE, seq))
        x = _ffn(x, mod, lat_row, lnw2, lnb2, w_gu_b, w_dn_b, layer, alpha, min(FFN_TILE, seq))
        if need_ctx:
            a_c = _attention(p_c, 0, None, p_c, 2, 3, None, tq=Q_BLOCK, tk=ATTN_K_CHUNK)
            b_c = _attention(p_c, 2, None, p_c, 6, 7, sink, tq=Q_BLOCK, tk=ATTN_K_CHUNK)
            f_c = _fourier_direct(u_c, fm_bd)
            ctx = _out_proj(flat(a_c), flat(b_c), flat(f_c), flat(g_c), (flat(r_out[2]),), flat(ctx), mod, ctx_row,
                            lnw1, lnb1, w_out_b, layer, gnw, gavg, alpha, bsz * lc)
            ctx = unflat(_ffn(ctx, mod, ctx_row, lnw2, lnb2, w_gu_b, w_dn_b, layer, alpha, bsz * lc))
    return x
```

```python
import functools

import numpy as np
import jax
import jax.numpy as jnp
from jax import lax
from jax.experimental import pallas as pl
from jax.experimental.pallas import tpu as pltpu

F32 = jnp.float32
BF16 = jnp.bfloat16

HEAD_DIM = 64
GRID_W = 64
Q_BLOCK = 128
ROPE_THETA = 10000.0
NORM_EPS = 1e-6
NEG_INF = -1e30
LOG2E = 1.4426950408889634

LANES = 128
MXU_TILE = 256
VMEM_LIMIT_BYTES = 56 * 1024 * 1024

RET_CHUNK = 256
RET_STEP_CHUNKS = 4
FFT_L2 = 128


def _cparams(sem):
    return pltpu.CompilerParams(dimension_semantics=sem, vmem_limit_bytes=VMEM_LIMIT_BYTES)


def _ln(x):
    mu = jnp.mean(x, axis=-1, keepdims=True)
    xc = x - mu
    var = jnp.mean(xc * xc, axis=-1, keepdims=True)
    return xc * lax.rsqrt(var + NORM_EPS)


def _silu(x):
    return x * jax.nn.sigmoid(x)


def _group_mean(t, g):
    hi = t.astype(BF16)
    lo = (t - hi.astype(F32)).astype(BF16)
    return (jnp.dot(hi, g, preferred_element_type=F32) + jnp.dot(lo, g, preferred_element_type=F32))


def _dot_nt(a, b):
    return lax.dot_general(a, b, (((1,), (1,)), ((), ())), preferred_element_type=F32)


def _mod_kernel(c_ref, w_ref, b_ref, o_ref):
    h = _silu(c_ref[...])
    w = w_ref[...]
    h_hi = h.astype(BF16)
    h_lo = (h - h_hi.astype(F32)).astype(BF16)
    w_hi = w.astype(BF16)
    w_lo = (w - w_hi.astype(F32)).astype(BF16)
    rows = h.shape[0]
    both = jnp.dot(jnp.concatenate([h_hi, h_lo], axis=0), w_hi, preferred_element_type=F32)
    o_ref[...] = both[:rows] + both[rows:] + jnp.dot(h_hi, w_lo, preferred_element_type=F32) + b_ref[...]


def _modulation(cc, w_mod, b_mod):
    depth, d, n = w_mod.shape
    tn = 2048
    return pl.pallas_call(
        _mod_kernel,
        out_shape=jax.ShapeDtypeStruct((depth, 8, n), F32),
        grid=(depth, n // tn),
        in_specs=[pl.BlockSpec((8, d), lambda l, j: (0, 0)),
                  pl.BlockSpec((None, d, tn), lambda l, j: (l, 0, j)),
                  pl.BlockSpec((None, 1, tn), lambda l, j: (l, 0, j))],
        out_specs=pl.BlockSpec((None, 8, tn), lambda l, j: (l, 0, j)),
        compiler_params=_cparams(("arbitrary", "arbitrary")),
        name="modulation",
    )(cc, w_mod, b_mod.reshape(depth, 1, n))


P_COLS = 14 * LANES


def _rope_lanes(t, c, ss, half):
    first = (lax.broadcasted_iota(jnp.int32, (t.shape[0], LANES), 1) & half) == 0
    outs = []
    for j in range(t.shape[1] // LANES):
        tj = t[:, j * LANES:(j + 1) * LANES]
        partner = jnp.where(first, pltpu.roll(tj, LANES - half, 1), pltpu.roll(tj, half, 1))
        outs.append(tj * c + partner * ss)
    return outs[0] if len(outs) == 1 else jnp.concatenate(outs, axis=1)


IN_ROW_SLABS = 4
IN_TILE = 1024


def _pair_heads_by_kv(q):
    a, b = q[:, 0:LANES], q[:, LANES:2 * LANES]
    lo = lax.broadcasted_iota(jnp.int32, a.shape, 1) < HEAD_DIM
    return jnp.concatenate([jnp.where(lo, a, pltpu.roll(b, HEAD_DIM, 1)),
                            jnp.where(lo, pltpu.roll(a, HEAD_DIM, 1), b)], axis=1)


def _in_kernel(*refs, rope):
    x_ref, sh_ref, sc_ref, w_ref, qn_ref, kn_ref, gavg_ref = refs[:7]
    if rope:
        c2_ref, ss2_ref, c1_ref, ss1_ref, p_ref, u_ref, g_ref = refs[7:]
    else:
        p_ref, u_ref, g_ref = refs[7:]
    def rms(t, w, g):
        return t * lax.rsqrt(_group_mean(t * t, g) + NORM_EPS) * w

    scale = HEAD_DIM ** -0.5
    qscale = scale * LOG2E
    slabs = IN_ROW_SLABS if x_ref.shape[0] % (8 * IN_ROW_SLABS) == 0 else 1
    rows = x_ref.shape[0] // slabs
    for r in range(slabs):
        rs = slice(r * rows, (r + 1) * rows)

        def rope2(t):
            if not rope:
                return t
            return _rope_lanes(t, c2_ref[rs, :], ss2_ref[rs, :], HEAD_DIM // 4)

        def rope1(t):
            if not rope:
                return t
            return _rope_lanes(t, c1_ref[rs, :], ss1_ref[rs, :], HEAD_DIM // 2)

        h = _ln(x_ref[rs, :]) * (1.0 + sc_ref[...]) + sh_ref[...]
        y = jnp.dot(h.astype(BF16), w_ref[...], preferred_element_type=F32)
        qa = rope2(rms(y[:, 0:256], qn_ref[...], gavg_ref[...])) * qscale
        ka = rope2(rms(y[:, 256:384], kn_ref[...], gavg_ref[0:LANES, 0:LANES]))
        p_ref[rs, 0:256] = _pair_heads_by_kv(qa).astype(BF16)
        p_ref[rs, 256:384] = ka.astype(BF16)
        p_ref[rs, 384:512] = y[:, 384:512].astype(BF16)
        p_ref[rs, 512:768] = _pair_heads_by_kv(rope2(y[:, 512:768]) * qscale).astype(BF16)
        p_ref[rs, 768:896] = rope2(y[:, 768:896]).astype(BF16)
        p_ref[rs, 896:1024] = y[:, 896:1024].astype(BF16)
        u_ref[rs, :] = y[:, 1024:1280].astype(BF16)
        p_ref[rs, 1024:1280] = rope1(y[:, 1280:1536]).astype(BF16)
        p_ref[rs, 1280:1536] = (rope1(y[:, 1536:1792]) * scale).astype(BF16)
        p_ref[rs, 1536:1792] = y[:, 1792:2048].astype(BF16)
        g_ref[rs, :] = y[:, 2048:2304]


def _in_proj(x, mod, mod_row, w, layer, qn, kn, gavg, tabs, tm):
    bsz, length, d = x.shape
    nw = w.shape[2]
    nt = length // tm
    rope = tabs is not None
    row = lambda b, i: (b, i, 0)
    const2 = lambda b, i: (0, 0)
    in_specs = [pl.BlockSpec((None, tm, d), row),
                pl.BlockSpec((None, None, 1, d), lambda b, i: (mod_row(b), 0, 0, 0)),
                pl.BlockSpec((None, None, 1, d), lambda b, i: (mod_row(b), 1, 0, 0)),
                pl.BlockSpec((None, d, nw), lambda b, i: (layer, 0, 0)),
                pl.BlockSpec((1, 256), const2),
                pl.BlockSpec((1, LANES), const2),
                pl.BlockSpec((256, 256), const2)]
    args = [x, mod, mod, w, qn, kn, gavg]
    if rope:
        in_specs += [pl.BlockSpec((tm, LANES), lambda b, i: (i, 0))] * len(tabs)
        args += list(tabs)
    return pl.pallas_call(
        functools.partial(_in_kernel, rope=rope),
        out_shape=(jax.ShapeDtypeStruct((bsz, length, P_COLS), BF16),
                   jax.ShapeDtypeStruct((bsz, length, 256), BF16),
                   jax.ShapeDtypeStruct((bsz, length, 256), F32)),
        grid=(bsz, nt),
        in_specs=in_specs,
        out_specs=(pl.BlockSpec((None, tm, P_COLS), row),
                   pl.BlockSpec((None, tm, 256), row),
                   pl.BlockSpec((None, tm, 256), row)),
        compiler_params=_cparams(("arbitrary", "arbitrary")),
        name="in_proj_rope" if rope else "in_proj_ctx",
    )(*args)


def _stack_heads(q):
    qf = q.astype(F32)
    lo = lax.broadcasted_iota(jnp.int32, (q.shape[0], LANES), 1) < HEAD_DIM
    q0, q1 = qf[:, 0:LANES], qf[:, LANES:2 * LANES]
    z = jnp.zeros_like(q0)
    return jnp.concatenate([jnp.where(lo, q0, z), jnp.where(lo, q1, z),
                            jnp.where(lo, z, q0), jnp.where(lo, z, q1)], axis=0).astype(BF16)


def _aug_values(v):
    vf = v.astype(F32)
    lo = lax.broadcasted_iota(jnp.int32, vf.shape, 1) < HEAD_DIM
    one = jnp.ones_like(vf)
    return jnp.where(lo, vf, one).astype(BF16), jnp.where(lo, one, vf).astype(BF16)


def _finish_heads(acc0, acc1, e, tq):
    l0 = pltpu.roll(acc0, HEAD_DIM, 1)
    l1 = pltpu.roll(acc1, HEAD_DIM, 1)
    if e is not None:
        l0 = l0 + e[:2 * tq]
        l1 = l1 + e[2 * tq:]
    n0 = acc0 / l0
    n1 = acc1 / l1
    lo = lax.broadcasted_iota(jnp.int32, (tq, LANES), 1) < HEAD_DIM
    return jnp.concatenate([jnp.where(lo, n0[:tq], pltpu.roll(n0[tq:], HEAD_DIM, 1)),
                            jnp.where(lo, pltpu.roll(n1[:tq], HEAD_DIM, 1), n1[tq:])], axis=1)


def _sink_column(sink_ref, tq):
    return jnp.concatenate([jnp.full((tq, 1), sink_ref[h] * LOG2E, F32) for h in range(4)], axis=0)


ATTN_Q_TILE = 256
ATTN_K_CHUNK = 2048

def _attn_kernel(*refs, tq, tk, n_lat, has_sink):
    i = 0
    sink_ref = None
    if has_sink:
        sink_ref = refs[0]
        i = 1
    q_ref = refs[i]
    i += 1
    if n_lat:
        kl_ref, vl_ref = refs[i:i + 2]
        i += 2
    kc_ref, vc_ref, o_ref = refs[i:i + 3]
    i += 3
    if n_lat:
        v0l_ref, v1l_ref = refs[i:i + 2]
        i += 2
    v0c_ref, v1c_ref = refs[i:i + 2]

    @pl.when(pl.program_id(1) == 0)
    def _():
        if n_lat:
            a0, a1 = _aug_values(vl_ref[...])
            v0l_ref[...] = a0
            v1l_ref[...] = a1
        a0, a1 = _aug_values(vc_ref[...])
        v0c_ref[...] = a0
        v1c_ref[...] = a1

    qs = _stack_heads(q_ref[...])
    half = 2 * tq

    chunks = [(kl_ref, v0l_ref, v1l_ref, slice(c * tk, (c + 1) * tk)) for c in range(n_lat)]
    chunks.append((kc_ref, v0c_ref, v1c_ref, slice(None)))

    def scores(chunk):
        k_ref, _, _, rows = chunk
        return _dot_nt(qs, k_ref[rows, :])

    m = _sink_column(sink_ref, tq) if has_sink else jnp.full((4 * tq, 1), NEG_INF, F32)
    acc0 = jnp.zeros((half, LANES), F32)
    acc1 = jnp.zeros((half, LANES), F32)
    s_next = scores(chunks[0])
    for idx, (_, v0_ref, v1_ref, rows) in enumerate(chunks):
        s = s_next
        if idx + 1 < len(chunks):
            s_next = scores(chunks[idx + 1])
        m_new = jnp.maximum(m, jnp.max(s, axis=1, keepdims=True))
        alpha = jnp.exp2(m - m_new)
        p = jnp.exp2(s - m_new).astype(BF16)
        acc0 = acc0 * alpha[:half] + jnp.dot(p[:half], v0_ref[rows, :], preferred_element_type=F32)
        acc1 = acc1 * alpha[half:] + jnp.dot(p[half:], v1_ref[rows, :], preferred_element_type=F32)
        m = m_new
    e = jnp.exp2(_sink_column(sink_ref, tq) - m) if has_sink else None
    o_ref[...] = _finish_heads(acc0, acc1, e, tq).astype(BF16)


def _attention(pq, q_blk, p_lat, p_ctx, k_blk, v_blk, sink, tq, tk):
    bsz, lq, _ = pq.shape
    lc = p_ctx.shape[1]
    assert p_lat is None or p_lat.shape[1] % tk == 0
    n_lat = 0 if p_lat is None else p_lat.shape[1] // tk
    has_sink = sink is not None
    in_specs = [pl.BlockSpec((None, tq, 256), lambda b, i, *_: (b, i, q_blk))]
    args = [pq]
    scratch = []
    if n_lat:
        ll = p_lat.shape[1]
        in_specs += [pl.BlockSpec((None, ll, LANES), lambda b, i, *_: (b, 0, k_blk)),
                     pl.BlockSpec((None, ll, LANES), lambda b, i, *_: (b, 0, v_blk))]
        args += [p_lat, p_lat]
        scratch += [pltpu.VMEM((ll, LANES), BF16), pltpu.VMEM((ll, LANES), BF16)]
    in_specs += [pl.BlockSpec((None, lc, LANES), lambda b, i, *_: (b, 0, k_blk)),
                 pl.BlockSpec((None, lc, LANES), lambda b, i, *_: (b, 0, v_blk))]
    args += [p_ctx, p_ctx]
    scratch += [pltpu.VMEM((lc, LANES), BF16), pltpu.VMEM((lc, LANES), BF16)]
    kern = functools.partial(_attn_kernel, tq=tq, tk=tk, n_lat=n_lat, has_sink=has_sink)
    grid_spec = pltpu.PrefetchScalarGridSpec(
        num_scalar_prefetch=1 if has_sink else 0,
        grid=(bsz, lq // tq),
        in_specs=in_specs,
        out_specs=pl.BlockSpec((None, tq, 256), lambda b, i, *_: (b, i, 0)),
        scratch_shapes=scratch)
    call = pl.pallas_call(
        kern, out_shape=jax.ShapeDtypeStruct((bsz, lq, 256), BF16), grid_spec=grid_spec,
        compiler_params=_cparams(("arbitrary", "arbitrary")),
        name="attn_sink" if has_sink else ("attn_global" if n_lat else "attn_ctx"))
    return call(sink, *args) if has_sink else call(*args)


WIN_BLOCKS_PER_STEP = 16


def _win_kernel(sink_ref, q_ref, kl_ref, vl_ref, kc_ref, vc_ref, o_ref, *, nb):
    tq = Q_BLOCK
    r = lax.broadcasted_iota(jnp.int32, (4 * tq, tq), 0) & (tq - 1)
    j = lax.broadcasted_iota(jnp.int32, (4 * tq, tq), 1)
    in_prev = j >= r
    in_next = j <= r
    snk = _sink_column(sink_ref, tq)
    kc, vc = kc_ref[...], vc_ref[...]

    def rows(ref, blk):
        return ref[pl.ds(pl.multiple_of(blk * tq, tq), tq), :]

    def scores(t):
        i = pl.program_id(1) * WIN_BLOCKS_PER_STEP + t
        prev = jnp.maximum(i - 1, 0)
        nxt = jnp.minimum(i + 1, nb - 1)
        k = jnp.concatenate([rows(kl_ref, prev), rows(kl_ref, i), rows(kl_ref, nxt), kc], axis=0)
        qs = _stack_heads(q_ref[t * tq:(t + 1) * tq, :])
        return _dot_nt(qs, k), i, prev, nxt

    ahead = scores(0)
    for t in range(WIN_BLOCKS_PER_STEP):
        s, i, prev, nxt = ahead
        if t + 1 < WIN_BLOCKS_PER_STEP:
            ahead = scores(t + 1)
        v = jnp.concatenate([rows(vl_ref, prev), rows(vl_ref, i), rows(vl_ref, nxt), vc], axis=0)
        off_prev = jnp.where(i > 0, 0.0, NEG_INF)
        off_next = jnp.where(i < nb - 1, 0.0, NEG_INF)
        s = jnp.concatenate([jnp.where(in_prev, s[:, 0:tq] + off_prev, NEG_INF), s[:, tq:2 * tq],
                             jnp.where(in_next, s[:, 2 * tq:3 * tq] + off_next, NEG_INF), s[:, 3 * tq:]], axis=1)
        m = jnp.maximum(jnp.max(s, axis=1, keepdims=True), snk)
        p = jnp.exp2(s - m).astype(BF16)
        v0, v1 = _aug_values(v)
        acc0 = jnp.dot(p[:2 * tq], v0, preferred_element_type=F32)
        acc1 = jnp.dot(p[2 * tq:], v1, preferred_element_type=F32)
        o_ref[t * tq:(t + 1) * tq, :] = _finish_heads(acc0, acc1, jnp.exp2(snk - m), tq).astype(BF16)


def _window_attention(p_lat, p_ctx, sink):
    bsz, ll, _ = p_lat.shape
    lc = p_ctx.shape[1]
    nb = ll // Q_BLOCK
    tqs = WIN_BLOCKS_PER_STEP * Q_BLOCK
    assert ll % tqs == 0
    grid_spec = pltpu.PrefetchScalarGridSpec(
        num_scalar_prefetch=1,
        grid=(bsz, ll // tqs),
        in_specs=[pl.BlockSpec((None, tqs, 256), lambda b, i, *_: (b, i, 2)),
                  pl.BlockSpec((None, ll, LANES), lambda b, i, *_: (b, 0, 6)),
                  pl.BlockSpec((None, ll, LANES), lambda b, i, *_: (b, 0, 7)),
                  pl.BlockSpec((None, lc, LANES), lambda b, i, *_: (b, 0, 6)),
                  pl.BlockSpec((None, lc, LANES), lambda b, i, *_: (b, 0, 7))],
        out_specs=pl.BlockSpec((None, tqs, 256), lambda b, i, *_: (b, i, 0)))
    return pl.pallas_call(
        functools.partial(_win_kernel, nb=nb),
        out_shape=jax.ShapeDtypeStruct((bsz, ll, 256), BF16), grid_spec=grid_spec,
        compiler_params=_cparams(("arbitrary", "arbitrary")),
        name="attn_window",
    )(sink, p_lat, p_lat, p_lat, p_ctx, p_ctx)


FFT_ROWS = 8
FFT1_STEP_ROWS = 16


def _fft1_kernel(u_ref, w_ref, y_ref):
    l1, rows, w = u_ref.shape
    uf = u_ref[...].astype(F32)
    ys = []
    for h in range(rows // FFT_ROWS):
        u = uf[:, h * FFT_ROWS:(h + 1) * FFT_ROWS, :].reshape(l1 * FFT_ROWS, w).astype(BF16)
        ys.append(jnp.dot(w_ref[...], u, preferred_element_type=F32).reshape(2, l1, FFT_ROWS, w))
    y_ref[...] = jnp.concatenate(ys, axis=2).astype(BF16)


def _channel_mix(ab, g_ref, fm_ref):
    z = jnp.dot(ab.astype(BF16), g_ref[...], preferred_element_type=F32)
    return jnp.dot(z.astype(BF16), fm_ref[...], preferred_element_type=F32)


def _fft2_kernel(y_ref, c_ref, s_ref, g_ref, fm_ref, o_ref):
    l2 = y_ref.shape[2]
    ab = []
    for r in range(FFT_ROWS):
        yr, yi = y_ref[0, r], y_ref[1, r]
        cs = jnp.concatenate([c_ref[r], s_ref[r]], axis=1)
        rhs = jnp.concatenate([jnp.concatenate([yr, yi], axis=1),
                               jnp.concatenate([yi, -yr], axis=1)], axis=0)
        ab.append(jnp.dot(cs, rhs, preferred_element_type=F32))
    o = _channel_mix(jnp.concatenate(ab, axis=0), g_ref, fm_ref)
    for r in range(FFT_ROWS):
        o_ref[:, r, :] = o[r * l2:(r + 1) * l2]


def _fft_direct_kernel(u_ref, cs_ref, g_ref, fm_ref, o_ref):
    n = u_ref.shape[0]
    y = jnp.dot(cs_ref[...], u_ref[...].astype(BF16), preferred_element_type=F32)
    o_ref[...] = _channel_mix(jnp.concatenate([y[:n], y[n:]], axis=1), g_ref, fm_ref)


def _mxu_const(a):
    return jnp.asarray(a, F32).astype(BF16)


def _dft_tables(n_rows, n_cols, length, row_stride=1, row_offset=0):
    k = row_offset + row_stride * np.arange(n_rows, dtype=np.int64)
    n = np.arange(n_cols, dtype=np.int64)
    ang = 2.0 * np.pi * ((k[:, None] * n[None, :]) % length).astype(np.float64) / length
    return np.cos(ang), np.sin(ang)


def _channel_dft(width, length):
    c, s = _dft_tables(HEAD_DIM, HEAD_DIM, HEAD_DIM)
    eye = np.eye(width // HEAD_DIM) / np.sqrt(float(length) * HEAD_DIM)
    return _mxu_const(np.concatenate([np.kron(eye, c), np.kron(eye, s)], axis=0))


def _fourier_latent(u, fm_bd):
    bsz, length, w = u.shape
    l2 = FFT_L2
    l1 = length // l2
    rows = FFT_ROWS
    c1, s1 = _dft_tables(l1, l1, l1)
    w1 = _mxu_const(np.kron(np.concatenate([c1, -s1], axis=0), np.eye(rows)))
    y = pl.pallas_call(
        _fft1_kernel,
        out_shape=jax.ShapeDtypeStruct((bsz, 2, l1, l2, w), BF16),
        grid=(bsz, l2 // FFT1_STEP_ROWS),
        in_specs=[pl.BlockSpec((None, l1, FFT1_STEP_ROWS, w), lambda b, j: (b, 0, j, 0)),
                  pl.BlockSpec(w1.shape, lambda b, j: (0, 0))],
        out_specs=pl.BlockSpec((None, 2, l1, FFT1_STEP_ROWS, w), lambda b, j: (b, 0, 0, j, 0)),
        compiler_params=_cparams(("arbitrary", "arbitrary")),
        name="fourier_stage1",
    )(u.reshape(bsz, l1, l2, w), w1)
    tabs = [_dft_tables(l2, l2, length, row_stride=l1, row_offset=k1) for k1 in range(l1)]
    ck = _mxu_const(np.stack([t[0] for t in tabs]))
    sk = _mxu_const(np.stack([t[1] for t in tabs]))
    const2 = lambda b, k: (0, 0)
    out = pl.pallas_call(
        _fft2_kernel,
        out_shape=jax.ShapeDtypeStruct((bsz, l2, l1, w), F32),
        grid=(bsz, l1 // rows),
        in_specs=[pl.BlockSpec((None, 2, rows, l2, w), lambda b, k: (b, 0, k, 0, 0)),
                  pl.BlockSpec((rows, l2, l2), lambda b, k: (k, 0, 0)),
                  pl.BlockSpec((rows, l2, l2), lambda b, k: (k, 0, 0)),
                  pl.BlockSpec((2 * w, w), const2), pl.BlockSpec((w, w), const2)],
        out_specs=pl.BlockSpec((None, l2, rows, w), lambda b, k: (b, 0, k, 0)),
        compiler_params=_cparams(("arbitrary", "arbitrary")),
        name="fourier_stage2",
    )(y, ck, sk, _channel_dft(w, length), fm_bd)
    return out.reshape(bsz, length, w)


def _fourier_direct(u, fm_bd):
    bsz, length, w = u.shape
    c, s = _dft_tables(length, length, length)
    cs = _mxu_const(np.concatenate([c, -s], axis=0))
    const2 = lambda b: (0, 0)
    return pl.pallas_call(
        _fft_direct_kernel,
        out_shape=jax.ShapeDtypeStruct((bsz, length, w), F32),
        grid=(bsz,),
        in_specs=[pl.BlockSpec((None, length, w), lambda b: (b, 0, 0)),
                  pl.BlockSpec((2 * length, length), const2),
                  pl.BlockSpec((2 * w, w), const2), pl.BlockSpec((w, w), const2)],
        out_specs=pl.BlockSpec((None, length, w), lambda b: (b, 0, 0)),
        compiler_params=_cparams(("arbitrary",)),
        name="fourier_ctx",
    )(u, cs, _channel_dft(w, length), fm_bd)


def _log_sigmoid(x):
    return jnp.minimum(x, 0.0) - jnp.log1p(jnp.exp(-jnp.abs(x)))


def _ret_kernel(*refs, need_ctx):
    (rdl_ref, rdh_ref, qf_ref, kf_ref, vf_ref, qb_ref, kb_ref, vb_ref, qc_ref, kc_ref, vc_ref) = refs[:11]
    if need_ctx:
        of_ref, ob_ref, oc_ref = refs[11:14]
        scr = refs[14:]
    else:
        of_ref, ob_ref = refs[11:13]
        oc_ref = None
        scr = refs[13:]
    sf_ref, sb_ref, din_ref, tab_ref = scr
    c = RET_CHUNK
    w = 4 * HEAD_DIM
    j = pl.program_id(1)
    head_shift = HEAD_DIM.bit_length() - 1
    lane_head = lax.broadcasted_iota(jnp.int32, (c, w), 1) >> head_shift
    blockdiag = ((lax.broadcasted_iota(jnp.int32, (w, w), 0) >> head_shift)
                 == (lax.broadcasted_iota(jnp.int32, (w, w), 1) >> head_shift))

    def decayed_scores(q, k, d):
        qf = q.astype(F32)
        inner = []
        for h in range(4):
            qh = jnp.where(lane_head == h, qf, 0.0).astype(BF16)
            inner.append((_dot_nt(qh, k) * din_ref[d, h]).astype(BF16))
        return inner, (k.astype(F32) * tab_ref[d, 1]).T.astype(BF16)

    def chunk_outputs(scores, v):
        inner, kz = scores
        o = jnp.zeros((c, w), F32)
        for h in range(4):
            o = o + jnp.where(lane_head == h, jnp.dot(inner[h], v, preferred_element_type=F32), 0.0)
        kv = jnp.where(blockdiag, jnp.dot(kz, v, preferred_element_type=F32), 0.0)
        return o, kv

    def chunk(q, k, v, d):
        return chunk_outputs(decayed_scores(q, k, d), v)

    @pl.when(j == 0)
    def _():
        t = lax.broadcasted_iota(jnp.int32, (c, w), 0).astype(F32)
        rr = lax.broadcasted_iota(jnp.int32, (c, c), 0)
        cc = lax.broadcasted_iota(jnp.int32, (c, c), 1)
        for d in range(2):
            lg = _log_sigmoid(rdl_ref[d])
            tab_ref[d, 0] = jnp.exp(lg * ((t + 1.0) if d == 0 else (c - t)))
            tab_ref[d, 1] = jnp.exp(lg * ((c - 1.0 - t) if d == 0 else t))
            tab_ref[d, 2] = jnp.exp(jnp.broadcast_to(lg, (c, w)) * float(c))
            diff = (rr - cc) if d == 0 else (cc - rr)
            dpos = jnp.maximum(diff, 0).astype(F32)
            for h in range(4):
                lgh = _log_sigmoid(rdh_ref[d, h])
                din_ref[d, h] = jnp.where(diff >= 0, jnp.exp(lgh * dpos), 0.0)
        q, k, v = qc_ref[...], kc_ref[...], vc_ref[...]
        o_f, kv_f = chunk(q, k, v, 0)
        o_b, kv_b = chunk(q, k, v, 1)
        sf_ref[...] = kv_f
        sb_ref[...] = kv_b
        if need_ctx:
            oc_ref[...] = o_f + o_b

    @pl.when(j > 0)
    def _():
        n_sub = qf_ref.shape[0] // c
        io = ((qf_ref, kf_ref, vf_ref, of_ref), (qb_ref, kb_ref, vb_ref, ob_ref))
        states = [sf_ref[...], sb_ref[...]]
        work = [(d, t if d == 0 else n_sub - 1 - t) for t in range(n_sub) for d in range(2)]

        def start(d, t):
            rs = slice(t * c, (t + 1) * c)
            q, k, v = io[d][0][rs, :], io[d][1][rs, :], io[d][2][rs, :]
            return q, v, decayed_scores(q, k, d)

        ahead = start(*work[0])
        for idx, (d, t) in enumerate(work):
            q, v, scores = ahead
            if idx + 1 < len(work):
                ahead = start(*work[idx + 1])
            o, kv = chunk_outputs(scores, v)
            cross = jnp.dot(q, states[d].astype(BF16), preferred_element_type=F32) * tab_ref[d, 0]
            io[d][3][t * c:(t + 1) * c, :] = o + cross
            states[d] = states[d] * tab_ref[d, 2, 0:1, :] + kv
        sf_ref[...] = states[0]
        sb_ref[...] = states[1]


def _retention(p_lat, p_ctx, rdl, rdh, need_ctx):
    bsz, ll, _ = p_lat.shape
    lc = p_ctx.shape[1]
    c = RET_CHUNK
    w = 4 * HEAD_DIM
    cs = c * RET_STEP_CHUNKS
    assert lc == c and ll % cs == 0
    n = ll // cs
    fwd = lambda blk: (lambda b, j: (b, jnp.maximum(j - 1, 0), blk))
    bwd = lambda blk: (lambda b, j: (b, n - 1 - jnp.maximum(j - 1, 0), blk))
    ctx = lambda blk: (lambda b, j: (b, 0, blk))
    in_specs = [pl.BlockSpec((2, 1, w), lambda b, j: (0, 0, 0)),
                pl.BlockSpec((2, 4, 1, c), lambda b, j: (0, 0, 0, 0))]
    in_specs += [pl.BlockSpec((None, cs, w), fwd(blk)) for blk in (4, 5, 6)]
    in_specs += [pl.BlockSpec((None, cs, w), bwd(blk)) for blk in (4, 5, 6)]
    in_specs += [pl.BlockSpec((None, c, w), ctx(blk)) for blk in (4, 5, 6)]
    out_shape = [jax.ShapeDtypeStruct((bsz, ll, w), F32), jax.ShapeDtypeStruct((bsz, ll, w), F32)]
    out_specs = [pl.BlockSpec((None, cs, w), fwd(0)), pl.BlockSpec((None, cs, w), bwd(0))]
    if need_ctx:
        out_shape.append(jax.ShapeDtypeStruct((bsz, lc, w), F32))
        out_specs.append(pl.BlockSpec((None, c, w), ctx(0)))
    return pl.pallas_call(
        functools.partial(_ret_kernel, need_ctx=need_ctx),
        out_shape=tuple(out_shape),
        grid=(bsz, n + 1),
        in_specs=in_specs,
        out_specs=tuple(out_specs),
        scratch_shapes=[pltpu.VMEM((w, w), F32), pltpu.VMEM((w, w), F32),
                        pltpu.VMEM((2, 4, c, c), F32), pltpu.VMEM((2, 3, c, w), F32)],
        compiler_params=_cparams(("arbitrary", "arbitrary")),
        name="retention_ctx_out" if need_ctx else "retention",
    )(rdl, rdh, *([p_lat] * 6), *([p_ctx] * 3))


OUT_ROW_SLABS = 4
OUT_TILE = 1024


def _out_kernel(*refs, n_o, alpha):
    a_ref, b_ref, f_ref, g_ref = refs[:4]
    o_refs = refs[4:4 + n_o]
    x_ref, g1_ref, lnw_ref, lnb_ref, w_ref, gnw_ref, gavg_ref, out_ref = refs[4 + n_o:]
    gavg = gavg_ref[...]
    slabs = OUT_ROW_SLABS if x_ref.shape[0] % (16 * OUT_ROW_SLABS) == 0 else 1
    rows = x_ref.shape[0] // slabs
    def mixer_outputs(s):
        rs = slice(s * rows, (s + 1) * rows)
        o = o_refs[0][rs, :]
        for r in o_refs[1:]:
            o = o + r[rs, :]
        dlt = o - _group_mean(o, gavg)
        on = dlt * lax.rsqrt(_group_mean(dlt * dlt, gavg) + NORM_EPS) * gnw_ref[...]
        ret = (_silu(g_ref[rs, :]) * on).astype(BF16)
        return jnp.concatenate([a_ref[rs, :], b_ref[rs, :], f_ref[rs, :].astype(BF16), ret], axis=1)

    ahead = mixer_outputs(0)
    for s in range(slabs):
        rs = slice(s * rows, (s + 1) * rows)
        cat = ahead
        if s + 1 < slabs:
            ahead = mixer_outputs(s + 1)
        y = jnp.dot(cat, w_ref[...], preferred_element_type=F32)
        z = alpha * x_ref[rs, :] + g1_ref[...] * y
        out_ref[rs, :] = _ln(z) * lnw_ref[...] + lnb_ref[...]


def _out_proj(a, b, f, g, o_parts, x, mod, mod_row, lnw, lnb, w, layer, gnw, gavg, alpha, tm):
    bsz, length, d = x.shape
    row = lambda bb, i: (bb, i, 0)
    const2 = lambda bb, i: (0, 0)
    blk256 = pl.BlockSpec((None, tm, 256), row)
    in_specs = [blk256] * (4 + len(o_parts)) + [
        pl.BlockSpec((None, tm, d), row),
        pl.BlockSpec((None, None, 1, d), lambda bb, i: (mod_row(bb), 2, 0, 0)),
        pl.BlockSpec((1, d), const2), pl.BlockSpec((1, d), const2),
        pl.BlockSpec((None, d, d), lambda bb, i: (layer, 0, 0)),
        pl.BlockSpec((1, 256), const2), pl.BlockSpec((256, 256), const2)]
    return pl.pallas_call(
        functools.partial(_out_kernel, n_o=len(o_parts), alpha=alpha),
        out_shape=jax.ShapeDtypeStruct((bsz, length, d), F32),
        grid=(bsz, length // tm),
        in_specs=in_specs,
        out_specs=pl.BlockSpec((None, tm, d), row),
        compiler_params=_cparams(("arbitrary", "arbitrary")),
        name="out_proj",
    )(a, b, f, g, *o_parts, x, mod, lnw, lnb, w, gnw, gavg)


FFN_ROW_SLABS = 4
FFN_TILE = 1024


def _ffn_kernel(x_ref, sh_ref, sc_ref, g2_ref, lnw_ref, lnb_ref, wg_ref, wu_ref, wd_ref, out_ref, *, fc, alpha):
    rows = x_ref.shape[0] // FFN_ROW_SLABS
    n_chunks = wg_ref.shape[1] // fc
    work = [(r, c) for r in range(FFN_ROW_SLABS) for c in range(n_chunks)]
    xs, hs, accs = {}, {}, {}

    def gate_up(r, c):
        if r not in hs:
            xs[r] = x_ref[r * rows:(r + 1) * rows, :]
            hs[r] = (_ln(xs[r]) * (1.0 + sc_ref[...]) + sh_ref[...]).astype(BF16)
        cols = slice(c * fc, (c + 1) * fc)
        return (jnp.dot(hs[r], wg_ref[:, cols], preferred_element_type=F32),
                jnp.dot(hs[r], wu_ref[:, cols], preferred_element_type=F32))

    ahead = gate_up(*work[0])
    for idx, (r, c) in enumerate(work):
        gate, up = ahead
        if idx + 1 < len(work):
            ahead = gate_up(*work[idx + 1])
        act = (_silu(gate) * up).astype(BF16)
        down = jnp.dot(act, wd_ref[c * fc:(c + 1) * fc, :], preferred_element_type=F32)
        accs[r] = down if c == 0 else accs[r] + down
        if c == n_chunks - 1:
            z = alpha * xs[r] + g2_ref[...] * accs[r]
            out_ref[r * rows:(r + 1) * rows, :] = _ln(z) * lnw_ref[...] + lnb_ref[...]


def _ffn(x, mod, mod_row, lnw, lnb, wgu, wd, layer, alpha, tm):
    bsz, length, d = x.shape
    ff = wd.shape[1]
    fc = ff // 2 if (ff // 2) % MXU_TILE == 0 else ff
    row = lambda bb, i: (bb, i, 0)
    const2 = lambda bb, i: (0, 0)
    modspec = lambda which: pl.BlockSpec((None, None, 1, d), lambda bb, i: (mod_row(bb), which, 0, 0))
    resident = lambda shape, col: pl.BlockSpec(shape, lambda bb, i: (layer, 0, col), pipeline_mode=pl.Buffered(1))
    return pl.pallas_call(
        functools.partial(_ffn_kernel, fc=fc, alpha=alpha),
        out_shape=jax.ShapeDtypeStruct((bsz, length, d), F32),
        grid=(bsz, length // tm),
        in_specs=[pl.BlockSpec((None, tm, d), row), modspec(3), modspec(4), modspec(5),
                  pl.BlockSpec((1, d), const2), pl.BlockSpec((1, d), const2),
                  resident((None, d, ff), 0), resident((None, d, ff), 1), resident((None, ff, d), 0)],
        out_specs=pl.BlockSpec((None, tm, d), row),
        compiler_params=_cparams(("arbitrary", "arbitrary")),
        name="ffn",
    )(x, mod, mod, mod, lnw, lnb, wgu, wgu, wd)


def _rope_tables(seq):
    t = np.arange(seq)
    f32 = np.float32

    def tab(pos, n_freq):
        inv = f32(ROPE_THETA) ** (-np.arange(n_freq, dtype=f32) / f32(n_freq))
        ang = (pos.astype(f32)[:, None] * inv[None, :]).astype(np.float64)
        return np.cos(ang), np.sin(ang)

    cr, sr = tab(t // GRID_W, HEAD_DIM // 4)
    cc, sc = tab(t % GRID_W, HEAD_DIM // 4)
    ct, st = tab(t, HEAD_DIM // 2)
    tables = (np.concatenate([cr, cr, cc, cc], -1), np.concatenate([-sr, sr, -sc, sc], -1),
              np.concatenate([ct, ct], -1), np.concatenate([-st, st], -1))
    return tuple(jnp.asarray(np.tile(a, (1, 2)), F32) for a in tables)


def kernel(x, c, ctx, c_ctx, w_mod, b_mod, w_in, a_q_norm, a_k_norm, b_sink, f_mix, r_decay, r_gn_w, w_out,
           ln1_w, ln1_b, w_gate_up, w_down, ln2_w, ln2_b):
    bsz, seq, d = x.shape
    depth = w_in.shape[0]
    gw = d // 4
    assert gw == 4 * HEAD_DIM and a_q_norm.shape[-1] == HEAD_DIM and seq % (FFT_L2 * 8) == 0
    alpha = (2.0 * depth) ** 0.25

    tabs = _rope_tables(seq)
    gavg = jnp.asarray(np.kron(np.eye(gw // HEAD_DIM), np.full((HEAD_DIM, HEAD_DIM), 1.0 / HEAD_DIM)), BF16)
    cc = jnp.zeros((8, d), F32).at[:bsz].set(c).at[bsz].set(c_ctx)
    mod_all = _modulation(cc, w_mod, b_mod).reshape(depth, 8, 6, 1, d)
    lat_row = lambda b: b
    ctx_row = lambda b: bsz

    eye_g = jnp.eye(gw // HEAD_DIM, dtype=F32)
    w_in_b, w_out_b = w_in.astype(BF16), w_out.astype(BF16)
    w_gu_b, w_dn_b = w_gate_up.astype(BF16), w_down.astype(BF16)

    for layer in range(depth):
        need_ctx = layer < depth - 1
        mod = mod_all[layer]
        qn = jnp.tile(a_q_norm[layer], 4)[None, :]
        kn = jnp.tile(a_k_norm[layer], 2)[None, :]
        gnw = r_gn_w[layer][None, :]
        lnw1, lnb1 = ln1_w[layer][None, :], ln1_b[layer][None, :]
        lnw2, lnb2 = ln2_w[layer][None, :], ln2_b[layer][None, :]
        fm_bd = jnp.einsum('gh,gce->gche', eye_g, f_mix[layer]).reshape(gw, gw).astype(BF16)
        rd = r_decay[layer]
        rdl = jnp.repeat(rd, HEAD_DIM, axis=1)[:, None, :]
        rdh = jnp.broadcast_to(rd[:, :, None, None], (2, 4, 1, RET_CHUNK))
        sink = b_sink[layer]

        p_l, u_l, g_l = _in_proj(x, mod, lat_row, w_in_b, layer, qn, kn, gavg, tabs, min(IN_TILE, seq))
        lc = ctx.shape[1]
        flat = lambda t: t.reshape(1, bsz * lc, t.shape[-1])
        unflat = lambda t: t.reshape(bsz, lc, t.shape[-1])
        p_c, u_c, g_c = map(unflat, _in_proj(flat(ctx), mod, ctx_row, w_in_b, layer, qn, kn, gavg, None, bsz * lc))

        a_l = _attention(p_l, 0, p_l, p_c, 2, 3, None, tq=ATTN_Q_TILE, tk=min(ATTN_K_CHUNK, seq))
        b_l = _window_attention(p_l, p_c, sink)
        f_l = _fourier_latent(u_l, fm_bd)
        r_out = _retention(p_l, p_c, rdl, rdh, need_ctx)
        x = _out_proj(a_l, b_l, f_l, g_l, r_out[:2], x, mod, lat_row, lnw1, lnb1, w_out_b, layer, gnw, gavg, alpha,
                      min(OUT_TILE, seq))
        x = _ffn(x, mod, lat_row, lnw2, lnb2, w_gu_b, w_dn_b, layer, alpha, min(FFN_TILE, seq))
        if need_ctx:
            a_c = _attention(p_c, 0, None, p_c, 2, 3, None, tq=Q_BLOCK, tk=ATTN_K_CHUNK)
            b_c = _attention(p_c, 2, None, p_c, 6, 7, sink, tq=Q_BLOCK, tk=ATTN_K_CHUNK)
            f_c = _fourier_direct(u_c, fm_bd)
            ctx = _out_proj(flat(a_c), flat(b_c), flat(f_c), flat(g_c), (flat(r_out[2]),), flat(ctx), mod, ctx_row,
                            lnw1, lnb1, w_out_b, layer, gnw, gavg, alpha, bsz * lc)
            ctx = unflat(_ffn(ctx, mod, ctx_row, lnw2, lnb2, w_gu_b, w_dn_b, layer, alpha, bsz * lc))
    return x
```

```python
import functools

import numpy as np
import jax
import jax.numpy as jnp
from jax import lax
from jax.experimental import pallas as pl
from jax.experimental.pallas import tpu as pltpu

F32 = jnp.float32
BF16 = jnp.bfloat16

HEAD_DIM = 64
GRID_W = 64
Q_BLOCK = 128
ROPE_THETA = 10000.0
NORM_EPS = 1e-6
NEG_INF = -1e30
LOG2E = 1.4426950408889634

LANES = 128
MXU_TILE = 256
VMEM_LIMIT_BYTES = 56 * 1024 * 1024

RET_CHUNK = 256
RET_STEP_CHUNKS = 8
FFT_L2 = 128


def _cparams(sem):
    return pltpu.CompilerParams(dimension_semantics=sem, vmem_limit_bytes=VMEM_LIMIT_BYTES)


def _ln(x):
    mu = jnp.mean(x, axis=-1, keepdims=True)
    xc = x - mu
    var = jnp.mean(xc * xc, axis=-1, keepdims=True)
    return xc * lax.rsqrt(var + NORM_EPS)


def _silu(x):
    return x * jax.nn.sigmoid(x)


def _group_mean(t, g):
    hi = t.astype(BF16)
    lo = (t - hi.astype(F32)).astype(BF16)
    return (jnp.dot(hi, g, preferred_element_type=F32) + jnp.dot(lo, g, preferred_element_type=F32))


def _dot_nt(a, b):
    return lax.dot_general(a, b, (((1,), (1,)), ((), ())), preferred_element_type=F32)


def _mod_kernel(c_ref, w_ref, b_ref, o_ref):
    h = _silu(c_ref[...])
    w = w_ref[...]
    h_hi = h.astype(BF16)
    h_lo = (h - h_hi.astype(F32)).astype(BF16)
    w_hi = w.astype(BF16)
    w_lo = (w - w_hi.astype(F32)).astype(BF16)
    rows = h.shape[0]
    both = jnp.dot(jnp.concatenate([h_hi, h_lo], axis=0), w_hi, preferred_element_type=F32)
    o_ref[...] = both[:rows] + both[rows:] + jnp.dot(h_hi, w_lo, preferred_element_type=F32) + b_ref[...]


def _modulation(cc, w_mod, b_mod):
    depth, d, n = w_mod.shape
    tn = 2048
    return pl.pallas_call(
        _mod_kernel,
        out_shape=jax.ShapeDtypeStruct((depth, 8, n), F32),
        grid=(depth, n // tn),
        in_specs=[pl.BlockSpec((8, d), lambda l, j: (0, 0)),
                  pl.BlockSpec((None, d, tn), lambda l, j: (l, 0, j)),
                  pl.BlockSpec((None, 1, tn), lambda l, j: (l, 0, j))],
        out_specs=pl.BlockSpec((None, 8, tn), lambda l, j: (l, 0, j)),
        compiler_params=_cparams(("arbitrary", "arbitrary")),
        name="modulation",
    )(cc, w_mod, b_mod.reshape(depth, 1, n))


P_COLS = 14 * LANES


def _rope_lanes(t, c, ss, half):
    first = (lax.broadcasted_iota(jnp.int32, (t.shape[0], LANES), 1) & half) == 0
    outs = []
    for j in range(t.shape[1] // LANES):
        tj = t[:, j * LANES:(j + 1) * LANES]
        partner = jnp.where(first, pltpu.roll(tj, LANES - half, 1), pltpu.roll(tj, half, 1))
        outs.append(tj * c + partner * ss)
    return outs[0] if len(outs) == 1 else jnp.concatenate(outs, axis=1)


IN_ROW_SLABS = 4
IN_TILE = 1024


def _pair_heads_by_kv(q):
    a, b = q[:, 0:LANES], q[:, LANES:2 * LANES]
    lo = lax.broadcasted_iota(jnp.int32, a.shape, 1) < HEAD_DIM
    return jnp.concatenate([jnp.where(lo, a, pltpu.roll(b, HEAD_DIM, 1)),
                            jnp.where(lo, pltpu.roll(a, HEAD_DIM, 1), b)], axis=1)


def _in_kernel(*refs, rope):
    x_ref, sh_ref, sc_ref, w_ref, qn_ref, kn_ref, gavg_ref = refs[:7]
    if rope:
        c2_ref, ss2_ref, c1_ref, ss1_ref, p_ref, u_ref, g_ref = refs[7:]
    else:
        p_ref, u_ref, g_ref = refs[7:]
    def rms(t, w, g):
        return t * lax.rsqrt(_group_mean(t * t, g) + NORM_EPS) * w

    scale = HEAD_DIM ** -0.5
    qscale = scale * LOG2E
    slabs = IN_ROW_SLABS if x_ref.shape[0] % (8 * IN_ROW_SLABS) == 0 else 1
    rows = x_ref.shape[0] // slabs
    for r in range(slabs):
        rs = slice(r * rows, (r + 1) * rows)

        def rope2(t):
            if not rope:
                return t
            return _rope_lanes(t, c2_ref[rs, :], ss2_ref[rs, :], HEAD_DIM // 4)

        def rope1(t):
            if not rope:
                return t
            return _rope_lanes(t, c1_ref[rs, :], ss1_ref[rs, :], HEAD_DIM // 2)

        h = _ln(x_ref[rs, :]) * (1.0 + sc_ref[...]) + sh_ref[...]
        y = jnp.dot(h.astype(BF16), w_ref[...], preferred_element_type=F32)
        qa = rope2(rms(y[:, 0:256], qn_ref[...], gavg_ref[...])) * qscale
        ka = rope2(rms(y[:, 256:384], kn_ref[...], gavg_ref[0:LANES, 0:LANES]))
        p_ref[rs, 0:256] = _pair_heads_by_kv(qa).astype(BF16)
        p_ref[rs, 256:384] = ka.astype(BF16)
        p_ref[rs, 384:512] = y[:, 384:512].astype(BF16)
        p_ref[rs, 512:768] = _pair_heads_by_kv(rope2(y[:, 512:768]) * qscale).astype(BF16)
        p_ref[rs, 768:896] = rope2(y[:, 768:896]).astype(BF16)
        p_ref[rs, 896:1024] = y[:, 896:1024].astype(BF16)
        u_ref[rs, :] = y[:, 1024:1280].astype(BF16)
        p_ref[rs, 1024:1280] = rope1(y[:, 1280:1536]).astype(BF16)
        p_ref[rs, 1280:1536] = (rope1(y[:, 1536:1792]) * scale).astype(BF16)
        p_ref[rs, 1536:1792] = y[:, 1792:2048].astype(BF16)
        g_ref[rs, :] = y[:, 2048:2304]


def _in_proj(x, mod, mod_row, w, layer, qn, kn, gavg, tabs, tm):
    bsz, length, d = x.shape
    nw = w.shape[2]
    nt = length // tm
    rope = tabs is not None
    row = lambda b, i: (b, i, 0)
    const2 = lambda b, i: (0, 0)
    in_specs = [pl.BlockSpec((None, tm, d), row),
                pl.BlockSpec((None, None, 1, d), lambda b, i: (mod_row(b), 0, 0, 0)),
                pl.BlockSpec((None, None, 1, d), lambda b, i: (mod_row(b), 1, 0, 0)),
                pl.BlockSpec((None, d, nw), lambda b, i: (layer, 0, 0)),
                pl.BlockSpec((1, 256), const2),
                pl.BlockSpec((1, LANES), const2),
                pl.BlockSpec((256, 256), const2)]
    args = [x, mod, mod, w, qn, kn, gavg]
    if rope:
        in_specs += [pl.BlockSpec((tm, LANES), lambda b, i: (i, 0))] * len(tabs)
        args += list(tabs)
    return pl.pallas_call(
        functools.partial(_in_kernel, rope=rope),
        out_shape=(jax.ShapeDtypeStruct((bsz, length, P_COLS), BF16),
                   jax.ShapeDtypeStruct((bsz, length, 256), BF16),
                   jax.ShapeDtypeStruct((bsz, length, 256), F32)),
        grid=(bsz, nt),
        in_specs=in_specs,
        out_specs=(pl.BlockSpec((None, tm, P_COLS), row),
                   pl.BlockSpec((None, tm, 256), row),
                   pl.BlockSpec((None, tm, 256), row)),
        compiler_params=_cparams(("arbitrary", "arbitrary")),
        name="in_proj_rope" if rope else "in_proj_ctx",
    )(*args)


def _stack_heads(q):
    qf = q.astype(F32)
    lo = lax.broadcasted_iota(jnp.int32, (q.shape[0], LANES), 1) < HEAD_DIM
    q0, q1 = qf[:, 0:LANES], qf[:, LANES:2 * LANES]
    z = jnp.zeros_like(q0)
    return jnp.concatenate([jnp.where(lo, q0, z), jnp.where(lo, q1, z),
                            jnp.where(lo, z, q0), jnp.where(lo, z, q1)], axis=0).astype(BF16)


def _aug_values(v):
    vf = v.astype(F32)
    lo = lax.broadcasted_iota(jnp.int32, vf.shape, 1) < HEAD_DIM
    one = jnp.ones_like(vf)
    return jnp.where(lo, vf, one).astype(BF16), jnp.where(lo, one, vf).astype(BF16)


def _finish_heads(acc0, acc1, e, tq):
    l0 = pltpu.roll(acc0, HEAD_DIM, 1)
    l1 = pltpu.roll(acc1, HEAD_DIM, 1)
    if e is not None:
        l0 = l0 + e[:2 * tq]
        l1 = l1 + e[2 * tq:]
    n0 = acc0 / l0
    n1 = acc1 / l1
    lo = lax.broadcasted_iota(jnp.int32, (tq, LANES), 1) < HEAD_DIM
    return jnp.concatenate([jnp.where(lo, n0[:tq], pltpu.roll(n0[tq:], HEAD_DIM, 1)),
                            jnp.where(lo, pltpu.roll(n1[:tq], HEAD_DIM, 1), n1[tq:])], axis=1)


def _sink_column(sink_ref, tq):
    return jnp.concatenate([jnp.full((tq, 1), sink_ref[h] * LOG2E, F32) for h in range(4)], axis=0)


ATTN_Q_TILE = 256
ATTN_K_CHUNK = 2048

def _attn_kernel(*refs, tq, tk, n_lat, has_sink):
    i = 0
    sink_ref = None
    if has_sink:
        sink_ref = refs[0]
        i = 1
    q_ref = refs[i]
    i += 1
    if n_lat:
        kl_ref, vl_ref = refs[i:i + 2]
        i += 2
    kc_ref, vc_ref, o_ref = refs[i:i + 3]
    i += 3
    if n_lat:
        v0l_ref, v1l_ref = refs[i:i + 2]
        i += 2
    v0c_ref, v1c_ref = refs[i:i + 2]

    @pl.when(pl.program_id(1) == 0)
    def _():
        if n_lat:
            a0, a1 = _aug_values(vl_ref[...])
            v0l_ref[...] = a0
            v1l_ref[...] = a1
        a0, a1 = _aug_values(vc_ref[...])
        v0c_ref[...] = a0
        v1c_ref[...] = a1

    qs = _stack_heads(q_ref[...])
    half = 2 * tq

    chunks = [(kl_ref, v0l_ref, v1l_ref, slice(c * tk, (c + 1) * tk)) for c in range(n_lat)]
    chunks.append((kc_ref, v0c_ref, v1c_ref, slice(None)))

    def scores(chunk):
        k_ref, _, _, rows = chunk
        return _dot_nt(qs, k_ref[rows, :])

    m = _sink_column(sink_ref, tq) if has_sink else jnp.full((4 * tq, 1), NEG_INF, F32)
    acc0 = jnp.zeros((half, LANES), F32)
    acc1 = jnp.zeros((half, LANES), F32)
    s_next = scores(chunks[0])
    for idx, (_, v0_ref, v1_ref, rows) in enumerate(chunks):
        s = s_next
        if idx + 1 < len(chunks):
            s_next = scores(chunks[idx + 1])
        m_new = jnp.maximum(m, jnp.max(s, axis=1, keepdims=True))
        alpha = jnp.exp2(m - m_new)
        p = jnp.exp2(s - m_new).astype(BF16)
        pv = jnp.dot(p, jnp.concatenate([v0_ref[rows, :], v1_ref[rows, :]], axis=1), preferred_element_type=F32)
        acc0 = acc0 * alpha[:half] + pv[:half, :LANES]
        acc1 = acc1 * alpha[half:] + pv[half:, LANES:]
        m = m_new
    e = jnp.exp2(_sink_column(sink_ref, tq) - m) if has_sink else None
    o_ref[...] = _finish_heads(acc0, acc1, e, tq).astype(BF16)


def _attention(pq, q_blk, p_lat, p_ctx, k_blk, v_blk, sink, tq, tk):
    bsz, lq, _ = pq.shape
    lc = p_ctx.shape[1]
    assert p_lat is None or p_lat.shape[1] % tk == 0
    n_lat = 0 if p_lat is None else p_lat.shape[1] // tk
    has_sink = sink is not None
    in_specs = [pl.BlockSpec((None, tq, 256), lambda b, i, *_: (b, i, q_blk))]
    args = [pq]
    scratch = []
    if n_lat:
        ll = p_lat.shape[1]
        in_specs += [pl.BlockSpec((None, ll, LANES), lambda b, i, *_: (b, 0, k_blk)),
                     pl.BlockSpec((None, ll, LANES), lambda b, i, *_: (b, 0, v_blk))]
        args += [p_lat, p_lat]
        scratch += [pltpu.VMEM((ll, LANES), BF16), pltpu.VMEM((ll, LANES), BF16)]
    in_specs += [pl.BlockSpec((None, lc, LANES), lambda b, i, *_: (b, 0, k_blk)),
                 pl.BlockSpec((None, lc, LANES), lambda b, i, *_: (b, 0, v_blk))]
    args += [p_ctx, p_ctx]
    scratch += [pltpu.VMEM((lc, LANES), BF16), pltpu.VMEM((lc, LANES), BF16)]
    kern = functools.partial(_attn_kernel, tq=tq, tk=tk, n_lat=n_lat, has_sink=has_sink)
    grid_spec = pltpu.PrefetchScalarGridSpec(
        num_scalar_prefetch=1 if has_sink else 0,
        grid=(bsz, lq // tq),
        in_specs=in_specs,
        out_specs=pl.BlockSpec((None, tq, 256), lambda b, i, *_: (b, i, 0)),
        scratch_shapes=scratch)
    call = pl.pallas_call(
        kern, out_shape=jax.ShapeDtypeStruct((bsz, lq, 256), BF16), grid_spec=grid_spec,
        compiler_params=_cparams(("arbitrary", "arbitrary")),
        name="attn_sink" if has_sink else ("attn_global" if n_lat else "attn_ctx"))
    return call(sink, *args) if has_sink else call(*args)


WIN_BLOCKS_PER_STEP = 16


def _win_kernel(sink_ref, q_ref, kl_ref, vl_ref, kc_ref, vc_ref, o_ref, *, nb):
    tq = Q_BLOCK
    r = lax.broadcasted_iota(jnp.int32, (4 * tq, tq), 0) & (tq - 1)
    j = lax.broadcasted_iota(jnp.int32, (4 * tq, tq), 1)
    in_prev = j >= r
    in_next = j <= r
    snk = _sink_column(sink_ref, tq)
    kc, vc = kc_ref[...], vc_ref[...]

    def rows(ref, blk):
        return ref[pl.ds(pl.multiple_of(blk * tq, tq), tq), :]

    def scores(t):
        i = pl.program_id(1) * WIN_BLOCKS_PER_STEP + t
        prev = jnp.maximum(i - 1, 0)
        nxt = jnp.minimum(i + 1, nb - 1)
        k = jnp.concatenate([rows(kl_ref, prev), rows(kl_ref, i), rows(kl_ref, nxt), kc], axis=0)
        qs = _stack_heads(q_ref[t * tq:(t + 1) * tq, :])
        return _dot_nt(qs, k), i, prev, nxt

    ahead = scores(0)
    for t in range(WIN_BLOCKS_PER_STEP):
        s, i, prev, nxt = ahead
        if t + 1 < WIN_BLOCKS_PER_STEP:
            ahead = scores(t + 1)
        v = jnp.concatenate([rows(vl_ref, prev), rows(vl_ref, i), rows(vl_ref, nxt), vc], axis=0)
        off_prev = jnp.where(i > 0, 0.0, NEG_INF)
        off_next = jnp.where(i < nb - 1, 0.0, NEG_INF)
        s = jnp.concatenate([jnp.where(in_prev, s[:, 0:tq] + off_prev, NEG_INF), s[:, tq:2 * tq],
                             jnp.where(in_next, s[:, 2 * tq:3 * tq] + off_next, NEG_INF), s[:, 3 * tq:]], axis=1)
        m = jnp.maximum(jnp.max(s, axis=1, keepdims=True), snk)
        p = jnp.exp2(s - m).astype(BF16)
        v0, v1 = _aug_values(v)
        acc0 = jnp.dot(p[:2 * tq], v0, preferred_element_type=F32)
        acc1 = jnp.dot(p[2 * tq:], v1, preferred_element_type=F32)
        o_ref[t * tq:(t + 1) * tq, :] = _finish_heads(acc0, acc1, jnp.exp2(snk - m), tq).astype(BF16)


def _window_attention(p_lat, p_ctx, sink):
    bsz, ll, _ = p_lat.shape
    lc = p_ctx.shape[1]
    nb = ll // Q_BLOCK
    tqs = WIN_BLOCKS_PER_STEP * Q_BLOCK
    assert ll % tqs == 0
    grid_spec = pltpu.PrefetchScalarGridSpec(
        num_scalar_prefetch=1,
        grid=(bsz, ll // tqs),
        in_specs=[pl.BlockSpec((None, tqs, 256), lambda b, i, *_: (b, i, 2)),
                  pl.BlockSpec((None, ll, LANES), lambda b, i, *_: (b, 0, 6)),
                  pl.BlockSpec((None, ll, LANES), lambda b, i, *_: (b, 0, 7)),
                  pl.BlockSpec((None, lc, LANES), lambda b, i, *_: (b, 0, 6)),
                  pl.BlockSpec((None, lc, LANES), lambda b, i, *_: (b, 0, 7))],
        out_specs=pl.BlockSpec((None, tqs, 256), lambda b, i, *_: (b, i, 0)))
    return pl.pallas_call(
        functools.partial(_win_kernel, nb=nb),
        out_shape=jax.ShapeDtypeStruct((bsz, ll, 256), BF16), grid_spec=grid_spec,
        compiler_params=_cparams(("arbitrary", "arbitrary")),
        name="attn_window",
    )(sink, p_lat, p_lat, p_lat, p_ctx, p_ctx)


FFT_ROWS = 8
FFT1_STEP_ROWS = 16


def _fft1_kernel(u_ref, w_ref, y_ref):
    l1, rows, w = u_ref.shape
    uf = u_ref[...].astype(F32)
    ys = []
    for h in range(rows // FFT_ROWS):
        u = uf[:, h * FFT_ROWS:(h + 1) * FFT_ROWS, :].reshape(l1 * FFT_ROWS, w).astype(BF16)
        ys.append(jnp.dot(w_ref[...], u, preferred_element_type=F32).reshape(2, l1, FFT_ROWS, w))
    y_ref[...] = jnp.concatenate(ys, axis=2).astype(BF16)


def _channel_mix(ab, g_ref, fm_ref):
    z = jnp.dot(ab.astype(BF16), g_ref[...], preferred_element_type=F32)
    return jnp.dot(z.astype(BF16), fm_ref[...], preferred_element_type=F32)


def _fft2_kernel(y_ref, c_ref, s_ref, g_ref, fm_ref, o_ref):
    l2 = y_ref.shape[2]
    ab = []
    for r in range(FFT_ROWS):
        yr, yi = y_ref[0, r], y_ref[1, r]
        cs = jnp.concatenate([c_ref[r], s_ref[r]], axis=1)
        rhs = jnp.concatenate([jnp.concatenate([yr, yi], axis=1),
                               jnp.concatenate([yi, -yr], axis=1)], axis=0)
        ab.append(jnp.dot(cs, rhs, preferred_element_type=F32))
    o = _channel_mix(jnp.concatenate(ab, axis=0), g_ref, fm_ref)
    for r in range(FFT_ROWS):
        o_ref[:, r, :] = o[r * l2:(r + 1) * l2]


def _fft_direct_kernel(u_ref, cs_ref, g_ref, fm_ref, o_ref):
    n = u_ref.shape[0]
    y = jnp.dot(cs_ref[...], u_ref[...].astype(BF16), preferred_element_type=F32)
    o_ref[...] = _channel_mix(jnp.concatenate([y[:n], y[n:]], axis=1), g_ref, fm_ref)


def _mxu_const(a):
    return jnp.asarray(a, F32).astype(BF16)


def _dft_tables(n_rows, n_cols, length, row_stride=1, row_offset=0):
    k = row_offset + row_stride * np.arange(n_rows, dtype=np.int64)
    n = np.arange(n_cols, dtype=np.int64)
    ang = 2.0 * np.pi * ((k[:, None] * n[None, :]) % length).astype(np.float64) / length
    return np.cos(ang), np.sin(ang)


def _channel_dft(width, length):
    c, s = _dft_tables(HEAD_DIM, HEAD_DIM, HEAD_DIM)
    eye = np.eye(width // HEAD_DIM) / np.sqrt(float(length) * HEAD_DIM)
    return _mxu_const(np.concatenate([np.kron(eye, c), np.kron(eye, s)], axis=0))


def _fourier_latent(u, fm_bd):
    bsz, length, w = u.shape
    l2 = FFT_L2
    l1 = length // l2
    rows = FFT_ROWS
    c1, s1 = _dft_tables(l1, l1, l1)
    w1 = _mxu_const(np.kron(np.concatenate([c1, -s1], axis=0), np.eye(rows)))
    y = pl.pallas_call(
        _fft1_kernel,
        out_shape=jax.ShapeDtypeStruct((bsz, 2, l1, l2, w), BF16),
        grid=(bsz, l2 // FFT1_STEP_ROWS),
        in_specs=[pl.BlockSpec((None, l1, FFT1_STEP_ROWS, w), lambda b, j: (b, 0, j, 0)),
                  pl.BlockSpec(w1.shape, lambda b, j: (0, 0))],
        out_specs=pl.BlockSpec((None, 2, l1, FFT1_STEP_ROWS, w), lambda b, j: (b, 0, 0, j, 0)),
        compiler_params=_cparams(("arbitrary", "arbitrary")),
        name="fourier_stage1",
    )(u.reshape(bsz, l1, l2, w), w1)
    tabs = [_dft_tables(l2, l2, length, row_stride=l1, row_offset=k1) for k1 in range(l1)]
    ck = _mxu_const(np.stack([t[0] for t in tabs]))
    sk = _mxu_const(np.stack([t[1] for t in tabs]))
    const2 = lambda b, k: (0, 0)
    out = pl.pallas_call(
        _fft2_kernel,
        out_shape=jax.ShapeDtypeStruct((bsz, l2, l1, w), F32),
        grid=(bsz, l1 // rows),
        in_specs=[pl.BlockSpec((None, 2, rows, l2, w), lambda b, k: (b, 0, k, 0, 0)),
                  pl.BlockSpec((rows, l2, l2), lambda b, k: (k, 0, 0)),
                  pl.BlockSpec((rows, l2, l2), lambda b, k: (k, 0, 0)),
                  pl.BlockSpec((2 * w, w), const2), pl.BlockSpec((w, w), const2)],
        out_specs=pl.BlockSpec((None, l2, rows, w), lambda b, k: (b, 0, k, 0)),
        compiler_params=_cparams(("arbitrary", "arbitrary")),
        name="fourier_stage2",
    )(y, ck, sk, _channel_dft(w, length), fm_bd)
    return out.reshape(bsz, length, w)


def _fourier_direct(u, fm_bd):
    bsz, length, w = u.shape
    c, s = _dft_tables(length, length, length)
    cs = _mxu_const(np.concatenate([c, -s], axis=0))
    const2 = lambda b: (0, 0)
    return pl.pallas_call(
        _fft_direct_kernel,
        out_shape=jax.ShapeDtypeStruct((bsz, length, w), F32),
        grid=(bsz,),
        in_specs=[pl.BlockSpec((None, length, w), lambda b: (b, 0, 0)),
                  pl.BlockSpec((2 * length, length), const2),
                  pl.BlockSpec((2 * w, w), const2), pl.BlockSpec((w, w), const2)],
        out_specs=pl.BlockSpec((None, length, w), lambda b: (b, 0, 0)),
        compiler_params=_cparams(("arbitrary",)),
        name="fourier_ctx",
    )(u, cs, _channel_dft(w, length), fm_bd)


def _log_sigmoid(x):
    return jnp.minimum(x, 0.0) - jnp.log1p(jnp.exp(-jnp.abs(x)))


def _ret_kernel(*refs, need_ctx):
    (rdl_ref, rdh_ref, qf_ref, kf_ref, vf_ref, qb_ref, kb_ref, vb_ref, qc_ref, kc_ref, vc_ref) = refs[:11]
    if need_ctx:
        of_ref, ob_ref, oc_ref = refs[11:14]
        scr = refs[14:]
    else:
        of_ref, ob_ref = refs[11:13]
        oc_ref = None
        scr = refs[13:]
    sf_ref, sb_ref, din_ref, tab_ref = scr
    c = RET_CHUNK
    w = 4 * HEAD_DIM
    j = pl.program_id(1)
    head_shift = HEAD_DIM.bit_length() - 1
    lane_head = lax.broadcasted_iota(jnp.int32, (c, w), 1) >> head_shift
    blockdiag = ((lax.broadcasted_iota(jnp.int32, (w, w), 0) >> head_shift)
                 == (lax.broadcasted_iota(jnp.int32, (w, w), 1) >> head_shift))

    def decayed_scores(q, k, d):
        qf = q.astype(F32)
        inner = []
        for h in range(4):
            qh = jnp.where(lane_head == h, qf, 0.0).astype(BF16)
            inner.append((_dot_nt(qh, k) * din_ref[d, h]).astype(BF16))
        return inner, (k.astype(F32) * tab_ref[d, 1]).T.astype(BF16)

    def chunk_outputs(scores, v):
        inner, kz = scores
        o = jnp.zeros((c, w), F32)
        for h in range(4):
            o = o + jnp.where(lane_head == h, jnp.dot(inner[h], v, preferred_element_type=F32), 0.0)
        kv = jnp.where(blockdiag, jnp.dot(kz, v, preferred_element_type=F32), 0.0)
        return o, kv

    def chunk(q, k, v, d):
        return chunk_outputs(decayed_scores(q, k, d), v)

    @pl.when(j == 0)
    def _():
        t = lax.broadcasted_iota(jnp.int32, (c, w), 0).astype(F32)
        rr = lax.broadcasted_iota(jnp.int32, (c, c), 0)
        cc = lax.broadcasted_iota(jnp.int32, (c, c), 1)
        for d in range(2):
            lg = _log_sigmoid(rdl_ref[d])
            tab_ref[d, 0] = jnp.exp(lg * ((t + 1.0) if d == 0 else (c - t)))
            tab_ref[d, 1] = jnp.exp(lg * ((c - 1.0 - t) if d == 0 else t))
            tab_ref[d, 2] = jnp.exp(jnp.broadcast_to(lg, (c, w)) * float(c))
            diff = (rr - cc) if d == 0 else (cc - rr)
            dpos = jnp.maximum(diff, 0).astype(F32)
            for h in range(4):
                lgh = _log_sigmoid(rdh_ref[d, h])
                din_ref[d, h] = jnp.where(diff >= 0, jnp.exp(lgh * dpos), 0.0)
        q, k, v = qc_ref[...], kc_ref[...], vc_ref[...]
        o_f, kv_f = chunk(q, k, v, 0)
        o_b, kv_b = chunk(q, k, v, 1)
        sf_ref[...] = kv_f
        sb_ref[...] = kv_b
        if need_ctx:
            oc_ref[...] = o_f + o_b

    @pl.when(j > 0)
    def _():
        n_sub = qf_ref.shape[0] // c
        io = ((qf_ref, kf_ref, vf_ref, of_ref), (qb_ref, kb_ref, vb_ref, ob_ref))
        states = [sf_ref[...], sb_ref[...]]
        work = [(d, t if d == 0 else n_sub - 1 - t) for t in range(n_sub) for d in range(2)]

        def start(d, t):
            rs = slice(t * c, (t + 1) * c)
            q, k, v = io[d][0][rs, :], io[d][1][rs, :], io[d][2][rs, :]
            return q, v, decayed_scores(q, k, d)

        ahead = start(*work[0])
        for idx, (d, t) in enumerate(work):
            q, v, scores = ahead
            if idx + 1 < len(work):
                ahead = start(*work[idx + 1])
            o, kv = chunk_outputs(scores, v)
            cross = jnp.dot(q, states[d].astype(BF16), preferred_element_type=F32) * tab_ref[d, 0]
            io[d][3][t * c:(t + 1) * c, :] = o + cross
            states[d] = states[d] * tab_ref[d, 2, 0:1, :] + kv
        sf_ref[...] = states[0]
        sb_ref[...] = states[1]


def _retention(p_lat, p_ctx, rdl, rdh, need_ctx):
    bsz, ll, _ = p_lat.shape
    lc = p_ctx.shape[1]
    c = RET_CHUNK
    w = 4 * HEAD_DIM
    step_chunks = next(k for k in (RET_STEP_CHUNKS, 4, 2, 1) if ll % (c * k) == 0)
    cs = c * step_chunks
    assert lc == c and ll % cs == 0
    n = ll // cs
    fwd = lambda blk: (lambda b, j: (b, jnp.maximum(j - 1, 0), blk))
    bwd = lambda blk: (lambda b, j: (b, n - 1 - jnp.maximum(j - 1, 0), blk))
    ctx = lambda blk: (lambda b, j: (b, 0, blk))
    in_specs = [pl.BlockSpec((2, 1, w), lambda b, j: (0, 0, 0)),
                pl.BlockSpec((2, 4, 1, c), lambda b, j: (0, 0, 0, 0))]
    in_specs += [pl.BlockSpec((None, cs, w), fwd(blk)) for blk in (4, 5, 6)]
    in_specs += [pl.BlockSpec((None, cs, w), bwd(blk)) for blk in (4, 5, 6)]
    in_specs += [pl.BlockSpec((None, c, w), ctx(blk)) for blk in (4, 5, 6)]
    out_shape = [jax.ShapeDtypeStruct((bsz, ll, w), F32), jax.ShapeDtypeStruct((bsz, ll, w), F32)]
    out_specs = [pl.BlockSpec((None, cs, w), fwd(0)), pl.BlockSpec((None, cs, w), bwd(0))]
    if need_ctx:
        out_shape.append(jax.ShapeDtypeStruct((bsz, lc, w), F32))
        out_specs.append(pl.BlockSpec((None, c, w), ctx(0)))
    return pl.pallas_call(
        functools.partial(_ret_kernel, need_ctx=need_ctx),
        out_shape=tuple(out_shape),
        grid=(bsz, n + 1),
        in_specs=in_specs,
        out_specs=tuple(out_specs),
        scratch_shapes=[pltpu.VMEM((w, w), F32), pltpu.VMEM((w, w), F32),
                        pltpu.VMEM((2, 4, c, c), F32), pltpu.VMEM((2, 3, c, w), F32)],
        compiler_params=_cparams(("arbitrary", "arbitrary")),
        name="retention_ctx_out" if need_ctx else "retention",
    )(rdl, rdh, *([p_lat] * 6), *([p_ctx] * 3))


OUT_ROW_SLABS = 4
OUT_TILE = 1024


def _out_kernel(*refs, n_o, alpha):
    a_ref, b_ref, f_ref, g_ref = refs[:4]
    o_refs = refs[4:4 + n_o]
    x_ref, g1_ref, lnw_ref, lnb_ref, w_ref, gnw_ref, gavg_ref, out_ref = refs[4 + n_o:]
    gavg = gavg_ref[...]
    slabs = OUT_ROW_SLABS if x_ref.shape[0] % (16 * OUT_ROW_SLABS) == 0 else 1
    rows = x_ref.shape[0] // slabs
    def mixer_outputs(s):
        rs = slice(s * rows, (s + 1) * rows)
        o = o_refs[0][rs, :]
        for r in o_refs[1:]:
            o = o + r[rs, :]
        dlt = o - _group_mean(o, gavg)
        on = dlt * lax.rsqrt(_group_mean(dlt * dlt, gavg) + NORM_EPS) * gnw_ref[...]
        ret = (_silu(g_ref[rs, :]) * on).astype(BF16)
        return jnp.concatenate([a_ref[rs, :], b_ref[rs, :], f_ref[rs, :].astype(BF16), ret], axis=1)

    ahead = mixer_outputs(0)
    for s in range(slabs):
        rs = slice(s * rows, (s + 1) * rows)
        cat = ahead
        if s + 1 < slabs:
            ahead = mixer_outputs(s + 1)
        y = jnp.dot(cat, w_ref[...], preferred_element_type=F32)
        z = alpha * x_ref[rs, :] + g1_ref[...] * y
        out_ref[rs, :] = _ln(z) * lnw_ref[...] + lnb_ref[...]


def _out_proj(a, b, f, g, o_parts, x, mod, mod_row, lnw, lnb, w, layer, gnw, gavg, alpha, tm):
    bsz, length, d = x.shape
    row = lambda bb, i: (bb, i, 0)
    const2 = lambda bb, i: (0, 0)
    blk256 = pl.BlockSpec((None, tm, 256), row)
    in_specs = [blk256] * (4 + len(o_parts)) + [
        pl.BlockSpec((None, tm, d), row),
        pl.BlockSpec((None, None, 1, d), lambda bb, i: (mod_row(bb), 2, 0, 0)),
        pl.BlockSpec((1, d), const2), pl.BlockSpec((1, d), const2),
        pl.BlockSpec((None, d, d), lambda bb, i: (layer, 0, 0)),
        pl.BlockSpec((1, 256), const2), pl.BlockSpec((256, 256), const2)]
    return pl.pallas_call(
        functools.partial(_out_kernel, n_o=len(o_parts), alpha=alpha),
        out_shape=jax.ShapeDtypeStruct((bsz, length, d), F32),
        grid=(bsz, length // tm),
        in_specs=in_specs,
        out_specs=pl.BlockSpec((None, tm, d), row),
        compiler_params=_cparams(("arbitrary", "arbitrary")),
        name="out_proj",
    )(a, b, f, g, *o_parts, x, mod, lnw, lnb, w, gnw, gavg)


FFN_ROW_SLABS = 4
FFN_TILE = 1024


def _ffn_kernel(x_ref, sh_ref, sc_ref, g2_ref, lnw_ref, lnb_ref, wg_ref, wu_ref, wd_ref, out_ref, *, fc, alpha):
    rows = x_ref.shape[0] // FFN_ROW_SLABS
    n_chunks = wg_ref.shape[1] // fc
    work = [(r, c) for r in range(FFN_ROW_SLABS) for c in range(n_chunks)]
    xs, hs, accs = {}, {}, {}

    def gate_up(r, c):
        if r not in hs:
            xs[r] = x_ref[r * rows:(r + 1) * rows, :]
            hs[r] = (_ln(xs[r]) * (1.0 + sc_ref[...]) + sh_ref[...]).astype(BF16)
        cols = slice(c * fc, (c + 1) * fc)
        return (jnp.dot(hs[r], wg_ref[:, cols], preferred_element_type=F32),
                jnp.dot(hs[r], wu_ref[:, cols], preferred_element_type=F32))

    ahead = gate_up(*work[0])
    for idx, (r, c) in enumerate(work):
        gate, up = ahead
        if idx + 1 < len(work):
            ahead = gate_up(*work[idx + 1])
        act = (_silu(gate) * up).astype(BF16)
        down = jnp.dot(act, wd_ref[c * fc:(c + 1) * fc, :], preferred_element_type=F32)
        accs[r] = down if c == 0 else accs[r] + down
        if c == n_chunks - 1:
            z = alpha * xs[r] + g2_ref[...] * accs[r]
            out_ref[r * rows:(r + 1) * rows, :] = _ln(z) * lnw_ref[...] + lnb_ref[...]


def _ffn(x, mod, mod_row, lnw, lnb, wgu, wd, layer, alpha, tm):
    bsz, length, d = x.shape
    ff = wd.shape[1]
    fc = ff // 2 if (ff // 2) % MXU_TILE == 0 else ff
    row = lambda bb, i: (bb, i, 0)
    const2 = lambda bb, i: (0, 0)
    modspec = lambda which: pl.BlockSpec((None, None, 1, d), lambda bb, i: (mod_row(bb), which, 0, 0))
    resident = lambda shape, col: pl.BlockSpec(shape, lambda bb, i: (layer, 0, col), pipeline_mode=pl.Buffered(1))
    return pl.pallas_call(
        functools.partial(_ffn_kernel, fc=fc, alpha=alpha),
        out_shape=jax.ShapeDtypeStruct((bsz, length, d), F32),
        grid=(bsz, length // tm),
        in_specs=[pl.BlockSpec((None, tm, d), row), modspec(3), modspec(4), modspec(5),
                  pl.BlockSpec((1, d), const2), pl.BlockSpec((1, d), const2),
                  resident((None, d, ff), 0), resident((None, d, ff), 1), resident((None, ff, d), 0)],
        out_specs=pl.BlockSpec((None, tm, d), row),
        compiler_params=_cparams(("arbitrary", "arbitrary")),
        name="ffn",
    )(x, mod, mod, mod, lnw, lnb, wgu, wgu, wd)


def _rope_tables(seq):
    t = np.arange(seq)
    f32 = np.float32

    def tab(pos, n_freq):
        inv = f32(ROPE_THETA) ** (-np.arange(n_freq, dtype=f32) / f32(n_freq))
        ang = (pos.astype(f32)[:, None] * inv[None, :]).astype(np.float64)
        return np.cos(ang), np.sin(ang)

    cr, sr = tab(t // GRID_W, HEAD_DIM // 4)
    cc, sc = tab(t % GRID_W, HEAD_DIM // 4)
    ct, st = tab(t, HEAD_DIM // 2)
    tables = (np.concatenate([cr, cr, cc, cc], -1), np.concatenate([-sr, sr, -sc, sc], -1),
              np.concatenate([ct, ct], -1), np.concatenate([-st, st], -1))
    return tuple(jnp.asarray(np.tile(a, (1, 2)), F32) for a in tables)


def kernel(x, c, ctx, c_ctx, w_mod, b_mod, w_in, a_q_norm, a_k_norm, b_sink, f_mix, r_decay, r_gn_w, w_out,
           ln1_w, ln1_b, w_gate_up, w_down, ln2_w, ln2_b):
    bsz, seq, d = x.shape
    depth = w_in.shape[0]
    gw = d // 4
    assert gw == 4 * HEAD_DIM and a_q_norm.shape[-1] == HEAD_DIM and seq % (FFT_L2 * 8) == 0
    alpha = (2.0 * depth) ** 0.25

    tabs = _rope_tables(seq)
    gavg = jnp.asarray(np.kron(np.eye(gw // HEAD_DIM), np.full((HEAD_DIM, HEAD_DIM), 1.0 / HEAD_DIM)), BF16)
    cc = jnp.zeros((8, d), F32).at[:bsz].set(c).at[bsz].set(c_ctx)
    mod_all = _modulation(cc, w_mod, b_mod).reshape(depth, 8, 6, 1, d)
    lat_row = lambda b: b
    ctx_row = lambda b: bsz

    eye_g = jnp.eye(gw // HEAD_DIM, dtype=F32)
    w_in_b, w_out_b = w_in.astype(BF16), w_out.astype(BF16)
    w_gu_b, w_dn_b = w_gate_up.astype(BF16), w_down.astype(BF16)

    for layer in range(depth):
        need_ctx = layer < depth - 1
        mod = mod_all[layer]
        qn = jnp.tile(a_q_norm[layer], 4)[None, :]
        kn = jnp.tile(a_k_norm[layer], 2)[None, :]
        gnw = r_gn_w[layer][None, :]
        lnw1, lnb1 = ln1_w[layer][None, :], ln1_b[layer][None, :]
        lnw2, lnb2 = ln2_w[layer][None, :], ln2_b[layer][None, :]
        fm_bd = jnp.einsum('gh,gce->gche', eye_g, f_mix[layer]).reshape(gw, gw).astype(BF16)
        rd = r_decay[layer]
        rdl = jnp.repeat(rd, HEAD_DIM, axis=1)[:, None, :]
        rdh = jnp.broadcast_to(rd[:, :, None, None], (2, 4, 1, RET_CHUNK))
        sink = b_sink[layer]

        p_l, u_l, g_l = _in_proj(x, mod, lat_row, w_in_b, layer, qn, kn, gavg, tabs, min(IN_TILE, seq))
        lc = ctx.shape[1]
        flat = lambda t: t.reshape(1, bsz * lc, t.shape[-1])
        unflat = lambda t: t.reshape(bsz, lc, t.shape[-1])
        p_c, u_c, g_c = map(unflat, _in_proj(flat(ctx), mod, ctx_row, w_in_b, layer, qn, kn, gavg, None, bsz * lc))

        a_l = _attention(p_l, 0, p_l, p_c, 2, 3, None, tq=ATTN_Q_TILE, tk=min(ATTN_K_CHUNK, seq))
        b_l = _window_attention(p_l, p_c, sink)
        f_l = _fourier_latent(u_l, fm_bd)
        r_out = _retention(p_l, p_c, rdl, rdh, need_ctx)
        x = _out_proj(a_l, b_l, f_l, g_l, r_out[:2], x, mod, lat_row, lnw1, lnb1, w_out_b, layer, gnw, gavg, alpha,
                      min(OUT_TILE, seq))
        x = _ffn(x, mod, lat_row, lnw2, lnb2, w_gu_b, w_dn_b, layer, alpha, min(FFN_TILE, seq))
        if need_ctx:
            a_c = _attention(p_c, 0, None, p_c, 2, 3, None, tq=Q_BLOCK, tk=ATTN_K_CHUNK)
            b_c = _attention(p_c, 2, None, p_c, 6, 7, sink, tq=Q_BLOCK, tk=ATTN_K_CHUNK)
            f_c = _fourier_direct(u_c, fm_bd)
            ctx = _out_proj(flat(a_c), flat(b_c), flat(f_c), flat(g_c), (flat(r_out[2]),), flat(ctx), mod, ctx_row,
                            lnw1, lnb1, w_out_b, layer, gnw, gavg, alpha, bsz * lc)
            ctx = unflat(_ffn(ctx, mod, ctx_row, lnw2, lnb2, w_gu_b, w_dn_b, layer, alpha, bsz * lc))
    return x
```

```python
import functools

import numpy as np
import jax
import jax.numpy as jnp
from jax import lax
from jax.experimental import pallas as pl
from jax.experimental.pallas import tpu as pltpu

F32 = jnp.float32
BF16 = jnp.bfloat16

HEAD_DIM = 64
GRID_W = 64
Q_BLOCK = 128
ROPE_THETA = 10000.0
NORM_EPS = 1e-6
NEG_INF = -1e30
LOG2E = 1.4426950408889634

LANES = 128
MXU_TILE = 256
VMEM_LIMIT_BYTES = 56 * 1024 * 1024

RET_CHUNK = 256
RET_STEP_CHUNKS = 8
FFT_L2 = 128


def _cparams(sem):
    return pltpu.CompilerParams(dimension_semantics=sem, vmem_limit_bytes=VMEM_LIMIT_BYTES)


def _ln(x):
    mu = jnp.mean(x, axis=-1, keepdims=True)
    xc = x - mu
    var = jnp.mean(xc * xc, axis=-1, keepdims=True)
    return xc * lax.rsqrt(var + NORM_EPS)


def _silu(x):
    return x * jax.nn.sigmoid(x)


def _group_mean(t, g):
    hi = t.astype(BF16)
    lo = (t - hi.astype(F32)).astype(BF16)
    return (jnp.dot(hi, g, preferred_element_type=F32) + jnp.dot(lo, g, preferred_element_type=F32))


def _row_slabs(rows, wanted):
    slabs = max(1, min(wanted, rows // MXU_TILE))
    return slabs if rows % (16 * slabs) == 0 else 1


def _dot_nt(a, b):
    return lax.dot_general(a, b, (((1,), (1,)), ((), ())), preferred_element_type=F32)


def _mod_kernel(c_ref, w_ref, b_ref, o_ref):
    h = _silu(c_ref[...])
    w = w_ref[...]
    h_hi = h.astype(BF16)
    h_lo = (h - h_hi.astype(F32)).astype(BF16)
    w_hi = w.astype(BF16)
    w_lo = (w - w_hi.astype(F32)).astype(BF16)
    rows = h.shape[0]
    both = jnp.dot(jnp.concatenate([h_hi, h_lo], axis=0), w_hi, preferred_element_type=F32)
    o_ref[...] = both[:rows] + both[rows:] + jnp.dot(h_hi, w_lo, preferred_element_type=F32) + b_ref[...]


def _modulation(cc, w_mod, b_mod):
    depth, d, n = w_mod.shape
    tn = 2048
    return pl.pallas_call(
        _mod_kernel,
        out_shape=jax.ShapeDtypeStruct((depth, 8, n), F32),
        grid=(depth, n // tn),
        in_specs=[pl.BlockSpec((8, d), lambda l, j: (0, 0)),
                  pl.BlockSpec((None, d, tn), lambda l, j: (l, 0, j)),
                  pl.BlockSpec((None, 1, tn), lambda l, j: (l, 0, j))],
        out_specs=pl.BlockSpec((None, 8, tn), lambda l, j: (l, 0, j)),
        compiler_params=_cparams(("arbitrary", "arbitrary")),
        name="modulation",
    )(cc, w_mod, b_mod.reshape(depth, 1, n))


P_COLS = 14 * LANES


def _rope_lanes(t, c, ss, half):
    first = (lax.broadcasted_iota(jnp.int32, (t.shape[0], LANES), 1) & half) == 0
    outs = []
    for j in range(t.shape[1] // LANES):
        tj = t[:, j * LANES:(j + 1) * LANES]
        partner = jnp.where(first, pltpu.roll(tj, LANES - half, 1), pltpu.roll(tj, half, 1))
        outs.append(tj * c + partner * ss)
    return outs[0] if len(outs) == 1 else jnp.concatenate(outs, axis=1)


IN_ROW_SLABS = 4
IN_TILE = 1024


def _pair_heads_by_kv(q):
    a, b = q[:, 0:LANES], q[:, LANES:2 * LANES]
    lo = lax.broadcasted_iota(jnp.int32, a.shape, 1) < HEAD_DIM
    return jnp.concatenate([jnp.where(lo, a, pltpu.roll(b, HEAD_DIM, 1)),
                            jnp.where(lo, pltpu.roll(a, HEAD_DIM, 1), b)], axis=1)


def _in_kernel(*refs, rope):
    x_ref, sh_ref, sc_ref, w_ref, qn_ref, kn_ref, gavg_ref = refs[:7]
    if rope:
        c2_ref, ss2_ref, c1_ref, ss1_ref, p_ref, u_ref, g_ref = refs[7:]
    else:
        p_ref, u_ref, g_ref = refs[7:]
    def rms(t, w, g):
        return t * lax.rsqrt(_group_mean(t * t, g) + NORM_EPS) * w

    scale = HEAD_DIM ** -0.5
    qscale = scale * LOG2E
    slabs = _row_slabs(x_ref.shape[0], IN_ROW_SLABS)
    rows = x_ref.shape[0] // slabs
    for r in range(slabs):
        rs = slice(r * rows, (r + 1) * rows)

        def rope2(t):
            if not rope:
                return t
            return _rope_lanes(t, c2_ref[rs, :], ss2_ref[rs, :], HEAD_DIM // 4)

        def rope1(t):
            if not rope:
                return t
            return _rope_lanes(t, c1_ref[rs, :], ss1_ref[rs, :], HEAD_DIM // 2)

        h = _ln(x_ref[rs, :]) * (1.0 + sc_ref[...]) + sh_ref[...]
        y = jnp.dot(h.astype(BF16), w_ref[...], preferred_element_type=F32)
        qa = rope2(rms(y[:, 0:256], qn_ref[...], gavg_ref[...])) * qscale
        ka = rope2(rms(y[:, 256:384], kn_ref[...], gavg_ref[0:LANES, 0:LANES]))
        p_ref[rs, 0:256] = _pair_heads_by_kv(qa).astype(BF16)
        p_ref[rs, 256:384] = ka.astype(BF16)
        p_ref[rs, 384:512] = y[:, 384:512].astype(BF16)
        p_ref[rs, 512:768] = _pair_heads_by_kv(rope2(y[:, 512:768]) * qscale).astype(BF16)
        p_ref[rs, 768:896] = rope2(y[:, 768:896]).astype(BF16)
        p_ref[rs, 896:1024] = y[:, 896:1024].astype(BF16)
        u_ref[rs, :] = y[:, 1024:1280].astype(BF16)
        p_ref[rs, 1024:1280] = rope1(y[:, 1280:1536]).astype(BF16)
        p_ref[rs, 1280:1536] = (rope1(y[:, 1536:1792]) * scale).astype(BF16)
        p_ref[rs, 1536:1792] = y[:, 1792:2048].astype(BF16)
        g_ref[rs, :] = y[:, 2048:2304]


def _in_proj(x, mod, mod_row, w, layer, qn, kn, gavg, tabs, tm):
    bsz, length, d = x.shape
    nw = w.shape[2]
    nt = length // tm
    rope = tabs is not None
    row = lambda b, i: (b, i, 0)
    const2 = lambda b, i: (0, 0)
    in_specs = [pl.BlockSpec((None, tm, d), row),
                pl.BlockSpec((None, None, 1, d), lambda b, i: (mod_row(b), 0, 0, 0)),
                pl.BlockSpec((None, None, 1, d), lambda b, i: (mod_row(b), 1, 0, 0)),
                pl.BlockSpec((None, d, nw), lambda b, i: (layer, 0, 0)),
                pl.BlockSpec((1, 256), const2),
                pl.BlockSpec((1, LANES), const2),
                pl.BlockSpec((256, 256), const2)]
    args = [x, mod, mod, w, qn, kn, gavg]
    if rope:
        in_specs += [pl.BlockSpec((tm, LANES), lambda b, i: (i, 0))] * len(tabs)
        args += list(tabs)
    return pl.pallas_call(
        functools.partial(_in_kernel, rope=rope),
        out_shape=(jax.ShapeDtypeStruct((bsz, length, P_COLS), BF16),
                   jax.ShapeDtypeStruct((bsz, length, 256), BF16),
                   jax.ShapeDtypeStruct((bsz, length, 256), F32)),
        grid=(bsz, nt),
        in_specs=in_specs,
        out_specs=(pl.BlockSpec((None, tm, P_COLS), row),
                   pl.BlockSpec((None, tm, 256), row),
                   pl.BlockSpec((None, tm, 256), row)),
        compiler_params=_cparams(("arbitrary", "arbitrary")),
        name="in_proj_rope" if rope else "in_proj_ctx",
    )(*args)


def _stack_heads(q):
    qf = q.astype(F32)
    lo = lax.broadcasted_iota(jnp.int32, (q.shape[0], LANES), 1) < HEAD_DIM
    q0, q1 = qf[:, 0:LANES], qf[:, LANES:2 * LANES]
    z = jnp.zeros_like(q0)
    return jnp.concatenate([jnp.where(lo, q0, z), jnp.where(lo, q1, z),
                            jnp.where(lo, z, q0), jnp.where(lo, z, q1)], axis=0).astype(BF16)


def _aug_values(v):
    vf = v.astype(F32)
    lo = lax.broadcasted_iota(jnp.int32, vf.shape, 1) < HEAD_DIM
    one = jnp.ones_like(vf)
    return jnp.where(lo, vf, one).astype(BF16), jnp.where(lo, one, vf).astype(BF16)


def _finish_heads(acc0, acc1, e, tq):
    l0 = pltpu.roll(acc0, HEAD_DIM, 1)
    l1 = pltpu.roll(acc1, HEAD_DIM, 1)
    if e is not None:
        l0 = l0 + e[:2 * tq]
        l1 = l1 + e[2 * tq:]
    n0 = acc0 / l0
    n1 = acc1 / l1
    lo = lax.broadcasted_iota(jnp.int32, (tq, LANES), 1) < HEAD_DIM
    return jnp.concatenate([jnp.where(lo, n0[:tq], pltpu.roll(n0[tq:], HEAD_DIM, 1)),
                            jnp.where(lo, pltpu.roll(n1[:tq], HEAD_DIM, 1), n1[tq:])], axis=1)


def _sink_column(sink_ref, tq):
    return jnp.concatenate([jnp.full((tq, 1), sink_ref[h] * LOG2E, F32) for h in range(4)], axis=0)


ATTN_Q_TILE = 256
ATTN_K_CHUNK = 2048

def _attn_kernel(*refs, tq, tk, n_lat, has_sink):
    i = 0
    sink_ref = None
    if has_sink:
        sink_ref = refs[0]
        i = 1
    q_ref = refs[i]
    i += 1
    if n_lat:
        kl_ref, vl_ref = refs[i:i + 2]
        i += 2
    kc_ref, vc_ref, o_ref = refs[i:i + 3]
    i += 3
    if n_lat:
        v0l_ref, v1l_ref = refs[i:i + 2]
        i += 2
    v0c_ref, v1c_ref = refs[i:i + 2]

    @pl.when(pl.program_id(1) == 0)
    def _():
        if n_lat:
            a0, a1 = _aug_values(vl_ref[...])
            v0l_ref[...] = a0
            v1l_ref[...] = a1
        a0, a1 = _aug_values(vc_ref[...])
        v0c_ref[...] = a0
        v1c_ref[...] = a1

    qs = _stack_heads(q_ref[...])
    half = 2 * tq

    chunks = [(kl_ref, v0l_ref, v1l_ref, slice(c * tk, (c + 1) * tk)) for c in range(n_lat)]
    chunks.append((kc_ref, v0c_ref, v1c_ref, slice(None)))

    def scores(chunk):
        k_ref, _, _, rows = chunk
        return _dot_nt(qs, k_ref[rows, :])

    m = _sink_column(sink_ref, tq) if has_sink else jnp.full((4 * tq, 1), NEG_INF, F32)
    acc0 = jnp.zeros((half, LANES), F32)
    acc1 = jnp.zeros((half, LANES), F32)
    s_next = scores(chunks[0])
    for idx, (_, v0_ref, v1_ref, rows) in enumerate(chunks):
        s = s_next
        if idx + 1 < len(chunks):
            s_next = scores(chunks[idx + 1])
        m_new = jnp.maximum(m, jnp.max(s, axis=1, keepdims=True))
        alpha = jnp.exp2(m - m_new)
        p = jnp.exp2(s - m_new).astype(BF16)
        pv = jnp.dot(p, jnp.concatenate([v0_ref[rows, :], v1_ref[rows, :]], axis=1), preferred_element_type=F32)
        acc0 = acc0 * alpha[:half] + pv[:half, :LANES]
        acc1 = acc1 * alpha[half:] + pv[half:, LANES:]
        m = m_new
    e = jnp.exp2(_sink_column(sink_ref, tq) - m) if has_sink else None
    o_ref[...] = _finish_heads(acc0, acc1, e, tq).astype(BF16)


def _attention(pq, q_blk, p_lat, p_ctx, k_blk, v_blk, sink, tq, tk):
    bsz, lq, _ = pq.shape
    lc = p_ctx.shape[1]
    assert p_lat is None or p_lat.shape[1] % tk == 0
    n_lat = 0 if p_lat is None else p_lat.shape[1] // tk
    has_sink = sink is not None
    in_specs = [pl.BlockSpec((None, tq, 256), lambda b, i, *_: (b, i, q_blk))]
    args = [pq]
    scratch = []
    if n_lat:
        ll = p_lat.shape[1]
        in_specs += [pl.BlockSpec((None, ll, LANES), lambda b, i, *_: (b, 0, k_blk)),
                     pl.BlockSpec((None, ll, LANES), lambda b, i, *_: (b, 0, v_blk))]
        args += [p_lat, p_lat]
        scratch += [pltpu.VMEM((ll, LANES), BF16), pltpu.VMEM((ll, LANES), BF16)]
    in_specs += [pl.BlockSpec((None, lc, LANES), lambda b, i, *_: (b, 0, k_blk)),
                 pl.BlockSpec((None, lc, LANES), lambda b, i, *_: (b, 0, v_blk))]
    args += [p_ctx, p_ctx]
    scratch += [pltpu.VMEM((lc, LANES), BF16), pltpu.VMEM((lc, LANES), BF16)]
    kern = functools.partial(_attn_kernel, tq=tq, tk=tk, n_lat=n_lat, has_sink=has_sink)
    grid_spec = pltpu.PrefetchScalarGridSpec(
        num_scalar_prefetch=1 if has_sink else 0,
        grid=(bsz, lq // tq),
        in_specs=in_specs,
        out_specs=pl.BlockSpec((None, tq, 256), lambda b, i, *_: (b, i, 0)),
        scratch_shapes=scratch)
    call = pl.pallas_call(
        kern, out_shape=jax.ShapeDtypeStruct((bsz, lq, 256), BF16), grid_spec=grid_spec,
        compiler_params=_cparams(("arbitrary", "arbitrary")),
        name="attn_sink" if has_sink else ("attn_global" if n_lat else "attn_ctx"))
    return call(sink, *args) if has_sink else call(*args)


WIN_BLOCKS_PER_STEP = 16


def _win_kernel(sink_ref, q_ref, kl_ref, vl_ref, kc_ref, vc_ref, o_ref, *, nb):
    tq = Q_BLOCK
    r = lax.broadcasted_iota(jnp.int32, (4 * tq, tq), 0) & (tq - 1)
    j = lax.broadcasted_iota(jnp.int32, (4 * tq, tq), 1)
    in_prev = j >= r
    in_next = j <= r
    snk = _sink_column(sink_ref, tq)
    kc, vc = kc_ref[...], vc_ref[...]

    def rows(ref, blk):
        return ref[pl.ds(pl.multiple_of(blk * tq, tq), tq), :]

    def scores(t):
        i = pl.program_id(1) * WIN_BLOCKS_PER_STEP + t
        prev = jnp.maximum(i - 1, 0)
        nxt = jnp.minimum(i + 1, nb - 1)
        k = jnp.concatenate([rows(kl_ref, prev), rows(kl_ref, i), rows(kl_ref, nxt), kc], axis=0)
        qs = _stack_heads(q_ref[t * tq:(t + 1) * tq, :])
        return _dot_nt(qs, k), i, prev, nxt

    ahead = scores(0)
    for t in range(WIN_BLOCKS_PER_STEP):
        s, i, prev, nxt = ahead
        if t + 1 < WIN_BLOCKS_PER_STEP:
            ahead = scores(t + 1)
        v = jnp.concatenate([rows(vl_ref, prev), rows(vl_ref, i), rows(vl_ref, nxt), vc], axis=0)
        off_prev = jnp.where(i > 0, 0.0, NEG_INF)
        off_next = jnp.where(i < nb - 1, 0.0, NEG_INF)
        s = jnp.concatenate([jnp.where(in_prev, s[:, 0:tq] + off_prev, NEG_INF), s[:, tq:2 * tq],
                             jnp.where(in_next, s[:, 2 * tq:3 * tq] + off_next, NEG_INF), s[:, 3 * tq:]], axis=1)
        m = jnp.maximum(jnp.max(s, axis=1, keepdims=True), snk)
        p = jnp.exp2(s - m).astype(BF16)
        v0, v1 = _aug_values(v)
        acc0 = jnp.dot(p[:2 * tq], v0, preferred_element_type=F32)
        acc1 = jnp.dot(p[2 * tq:], v1, preferred_element_type=F32)
        o_ref[t * tq:(t + 1) * tq, :] = _finish_heads(acc0, acc1, jnp.exp2(snk - m), tq).astype(BF16)


def _window_attention(p_lat, p_ctx, sink):
    bsz, ll, _ = p_lat.shape
    lc = p_ctx.shape[1]
    nb = ll // Q_BLOCK
    tqs = WIN_BLOCKS_PER_STEP * Q_BLOCK
    assert ll % tqs == 0
    grid_spec = pltpu.PrefetchScalarGridSpec(
        num_scalar_prefetch=1,
        grid=(bsz, ll // tqs),
        in_specs=[pl.BlockSpec((None, tqs, 256), lambda b, i, *_: (b, i, 2)),
                  pl.BlockSpec((None, ll, LANES), lambda b, i, *_: (b, 0, 6)),
                  pl.BlockSpec((None, ll, LANES), lambda b, i, *_: (b, 0, 7)),
                  pl.BlockSpec((None, lc, LANES), lambda b, i, *_: (b, 0, 6)),
                  pl.BlockSpec((None, lc, LANES), lambda b, i, *_: (b, 0, 7))],
        out_specs=pl.BlockSpec((None, tqs, 256), lambda b, i, *_: (b, i, 0)))
    return pl.pallas_call(
        functools.partial(_win_kernel, nb=nb),
        out_shape=jax.ShapeDtypeStruct((bsz, ll, 256), BF16), grid_spec=grid_spec,
        compiler_params=_cparams(("arbitrary", "arbitrary")),
        name="attn_window",
    )(sink, p_lat, p_lat, p_lat, p_ctx, p_ctx)


FFT_ROWS = 8
FFT1_STEP_ROWS = 16


def _fft1_kernel(u_ref, w_ref, y_ref):
    l1, rows, w = u_ref.shape
    uf = u_ref[...].astype(F32)
    ys = []
    for h in range(rows // FFT_ROWS):
        u = uf[:, h * FFT_ROWS:(h + 1) * FFT_ROWS, :].reshape(l1 * FFT_ROWS, w).astype(BF16)
        ys.append(jnp.dot(w_ref[...], u, preferred_element_type=F32).reshape(2, l1, FFT_ROWS, w))
    y_ref[...] = jnp.concatenate(ys, axis=2).astype(BF16)


def _channel_mix(ab, g_ref, fm_ref):
    z = jnp.dot(ab.astype(BF16), g_ref[...], preferred_element_type=F32)
    return jnp.dot(z.astype(BF16), fm_ref[...], preferred_element_type=F32)


def _fft2_kernel(y_ref, c_ref, s_ref, g_ref, fm_ref, o_ref):
    l2 = y_ref.shape[2]
    ab = []
    for r in range(FFT_ROWS):
        yr, yi = y_ref[0, r], y_ref[1, r]
        cs = jnp.concatenate([c_ref[r], s_ref[r]], axis=1)
        rhs = jnp.concatenate([jnp.concatenate([yr, yi], axis=1),
                               jnp.concatenate([yi, -yr], axis=1)], axis=0)
        ab.append(jnp.dot(cs, rhs, preferred_element_type=F32))
    o = _channel_mix(jnp.concatenate(ab, axis=0), g_ref, fm_ref)
    for r in range(FFT_ROWS):
        o_ref[:, r, :] = o[r * l2:(r + 1) * l2]


def _fft_direct_kernel(u_ref, cs_ref, g_ref, fm_ref, o_ref):
    n = u_ref.shape[0]
    y = jnp.dot(cs_ref[...], u_ref[...].astype(BF16), preferred_element_type=F32)
    o_ref[...] = _channel_mix(jnp.concatenate([y[:n], y[n:]], axis=1), g_ref, fm_ref)


def _mxu_const(a):
    return jnp.asarray(a, F32).astype(BF16)


def _dft_tables(n_rows, n_cols, length, row_stride=1, row_offset=0):
    k = row_offset + row_stride * np.arange(n_rows, dtype=np.int64)
    n = np.arange(n_cols, dtype=np.int64)
    ang = 2.0 * np.pi * ((k[:, None] * n[None, :]) % length).astype(np.float64) / length
    return np.cos(ang), np.sin(ang)


def _channel_dft(width, length):
    c, s = _dft_tables(HEAD_DIM, HEAD_DIM, HEAD_DIM)
    eye = np.eye(width // HEAD_DIM) / np.sqrt(float(length) * HEAD_DIM)
    return _mxu_const(np.concatenate([np.kron(eye, c), np.kron(eye, s)], axis=0))


def _fourier_latent(u, fm_bd):
    bsz, length, w = u.shape
    l2 = FFT_L2
    l1 = length // l2
    rows = FFT_ROWS
    c1, s1 = _dft_tables(l1, l1, l1)
    w1 = _mxu_const(np.kron(np.concatenate([c1, -s1], axis=0), np.eye(rows)))
    y = pl.pallas_call(
        _fft1_kernel,
        out_shape=jax.ShapeDtypeStruct((bsz, 2, l1, l2, w), BF16),
        grid=(bsz, l2 // FFT1_STEP_ROWS),
        in_specs=[pl.BlockSpec((None, l1, FFT1_STEP_ROWS, w), lambda b, j: (b, 0, j, 0)),
                  pl.BlockSpec(w1.shape, lambda b, j: (0, 0))],
        out_specs=pl.BlockSpec((None, 2, l1, FFT1_STEP_ROWS, w), lambda b, j: (b, 0, 0, j, 0)),
        compiler_params=_cparams(("arbitrary", "arbitrary")),
        name="fourier_stage1",
    )(u.reshape(bsz, l1, l2, w), w1)
    tabs = [_dft_tables(l2, l2, length, row_stride=l1, row_offset=k1) for k1 in range(l1)]
    ck = _mxu_const(np.stack([t[0] for t in tabs]))
    sk = _mxu_const(np.stack([t[1] for t in tabs]))
    const2 = lambda b, k: (0, 0)
    out = pl.pallas_call(
        _fft2_kernel,
        out_shape=jax.ShapeDtypeStruct((bsz, l2, l1, w), F32),
        grid=(bsz, l1 // rows),
        in_specs=[pl.BlockSpec((None, 2, rows, l2, w), lambda b, k: (b, 0, k, 0, 0)),
                  pl.BlockSpec((rows, l2, l2), lambda b, k: (k, 0, 0)),
                  pl.BlockSpec((rows, l2, l2), lambda b, k: (k, 0, 0)),
                  pl.BlockSpec((2 * w, w), const2), pl.BlockSpec((w, w), const2)],
        out_specs=pl.BlockSpec((None, l2, rows, w), lambda b, k: (b, 0, k, 0)),
        compiler_params=_cparams(("arbitrary", "arbitrary")),
        name="fourier_stage2",
    )(y, ck, sk, _channel_dft(w, length), fm_bd)
    return out.reshape(bsz, length, w)


def _fourier_direct(u, fm_bd):
    bsz, length, w = u.shape
    c, s = _dft_tables(length, length, length)
    cs = _mxu_const(np.concatenate([c, -s], axis=0))
    const2 = lambda b: (0, 0)
    return pl.pallas_call(
        _fft_direct_kernel,
        out_shape=jax.ShapeDtypeStruct((bsz, length, w), F32),
        grid=(bsz,),
        in_specs=[pl.BlockSpec((None, length, w), lambda b: (b, 0, 0)),
                  pl.BlockSpec((2 * length, length), const2),
                  pl.BlockSpec((2 * w, w), const2), pl.BlockSpec((w, w), const2)],
        out_specs=pl.BlockSpec((None, length, w), lambda b: (b, 0, 0)),
        compiler_params=_cparams(("arbitrary",)),
        name="fourier_ctx",
    )(u, cs, _channel_dft(w, length), fm_bd)


def _log_sigmoid(x):
    return jnp.minimum(x, 0.0) - jnp.log1p(jnp.exp(-jnp.abs(x)))


def _ret_kernel(*refs, need_ctx):
    (rdl_ref, rdh_ref, qf_ref, kf_ref, vf_ref, qb_ref, kb_ref, vb_ref, qc_ref, kc_ref, vc_ref) = refs[:11]
    if need_ctx:
        of_ref, ob_ref, oc_ref = refs[11:14]
        scr = refs[14:]
    else:
        of_ref, ob_ref = refs[11:13]
        oc_ref = None
        scr = refs[13:]
    sf_ref, sb_ref, din_ref, tab_ref = scr
    c = RET_CHUNK
    w = 4 * HEAD_DIM
    j = pl.program_id(1)
    head_shift = HEAD_DIM.bit_length() - 1
    lane_head = lax.broadcasted_iota(jnp.int32, (c, w), 1) >> head_shift
    blockdiag = ((lax.broadcasted_iota(jnp.int32, (w, w), 0) >> head_shift)
                 == (lax.broadcasted_iota(jnp.int32, (w, w), 1) >> head_shift))

    def decayed_scores(q, k, d):
        qf = q.astype(F32)
        inner = []
        for h in range(4):
            qh = jnp.where(lane_head == h, qf, 0.0).astype(BF16)
            inner.append((_dot_nt(qh, k) * din_ref[d, h]).astype(BF16))
        return inner, (k.astype(F32) * tab_ref[d, 1]).T.astype(BF16)

    def chunk_outputs(scores, v):
        inner, kz = scores
        o = jnp.zeros((c, w), F32)
        for h in range(4):
            o = o + jnp.where(lane_head == h, jnp.dot(inner[h], v, preferred_element_type=F32), 0.0)
        kv = jnp.where(blockdiag, jnp.dot(kz, v, preferred_element_type=F32), 0.0)
        return o, kv

    def chunk(q, k, v, d):
        return chunk_outputs(decayed_scores(q, k, d), v)

    @pl.when(j == 0)
    def _():
        t = lax.broadcasted_iota(jnp.int32, (c, w), 0).astype(F32)
        rr = lax.broadcasted_iota(jnp.int32, (c, c), 0)
        cc = lax.broadcasted_iota(jnp.int32, (c, c), 1)
        for d in range(2):
            lg = _log_sigmoid(rdl_ref[d])
            tab_ref[d, 0] = jnp.exp(lg * ((t + 1.0) if d == 0 else (c - t)))
            tab_ref[d, 1] = jnp.exp(lg * ((c - 1.0 - t) if d == 0 else t))
            tab_ref[d, 2] = jnp.exp(jnp.broadcast_to(lg, (c, w)) * float(c))
            diff = (rr - cc) if d == 0 else (cc - rr)
            dpos = jnp.maximum(diff, 0).astype(F32)
            for h in range(4):
                lgh = _log_sigmoid(rdh_ref[d, h])
                din_ref[d, h] = jnp.where(diff >= 0, jnp.exp(lgh * dpos), 0.0)
        q, k, v = qc_ref[...], kc_ref[...], vc_ref[...]
        o_f, kv_f = chunk(q, k, v, 0)
        o_b, kv_b = chunk(q, k, v, 1)
        sf_ref[...] = kv_f
        sb_ref[...] = kv_b
        if need_ctx:
            oc_ref[...] = o_f + o_b

    @pl.when(j > 0)
    def _():
        n_sub = qf_ref.shape[0] // c
        io = ((qf_ref, kf_ref, vf_ref, of_ref), (qb_ref, kb_ref, vb_ref, ob_ref))
        states = [sf_ref[...], sb_ref[...]]
        work = [(d, t if d == 0 else n_sub - 1 - t) for t in range(n_sub) for d in range(2)]

        def start(d, t):
            rs = slice(t * c, (t + 1) * c)
            q, k, v = io[d][0][rs, :], io[d][1][rs, :], io[d][2][rs, :]
            return q, v, decayed_scores(q, k, d)

        ahead = start(*work[0])
        for idx, (d, t) in enumerate(work):
            q, v, scores = ahead
            if idx + 1 < len(work):
                ahead = start(*work[idx + 1])
            o, kv = chunk_outputs(scores, v)
            cross = jnp.dot(q, states[d].astype(BF16), preferred_element_type=F32) * tab_ref[d, 0]
            io[d][3][t * c:(t + 1) * c, :] = o + cross
            states[d] = states[d] * tab_ref[d, 2, 0:1, :] + kv
        sf_ref[...] = states[0]
        sb_ref[...] = states[1]


def _retention(p_lat, p_ctx, rdl, rdh, need_ctx):
    bsz, ll, _ = p_lat.shape
    lc = p_ctx.shape[1]
    c = RET_CHUNK
    w = 4 * HEAD_DIM
    step_chunks = next(k for k in (RET_STEP_CHUNKS, 4, 2, 1) if ll % (c * k) == 0)
    cs = c * step_chunks
    assert lc == c and ll % cs == 0
    n = ll // cs
    fwd = lambda blk: (lambda b, j: (b, jnp.maximum(j - 1, 0), blk))
    bwd = lambda blk: (lambda b, j: (b, n - 1 - jnp.maximum(j - 1, 0), blk))
    ctx = lambda blk: (lambda b, j: (b, 0, blk))
    in_specs = [pl.BlockSpec((2, 1, w), lambda b, j: (0, 0, 0)),
                pl.BlockSpec((2, 4, 1, c), lambda b, j: (0, 0, 0, 0))]
    in_specs += [pl.BlockSpec((None, cs, w), fwd(blk)) for blk in (4, 5, 6)]
    in_specs += [pl.BlockSpec((None, cs, w), bwd(blk)) for blk in (4, 5, 6)]
    in_specs += [pl.BlockSpec((None, c, w), ctx(blk)) for blk in (4, 5, 6)]
    out_shape = [jax.ShapeDtypeStruct((bsz, ll, w), F32), jax.ShapeDtypeStruct((bsz, ll, w), F32)]
    out_specs = [pl.BlockSpec((None, cs, w), fwd(0)), pl.BlockSpec((None, cs, w), bwd(0))]
    if need_ctx:
        out_shape.append(jax.ShapeDtypeStruct((bsz, lc, w), F32))
        out_specs.append(pl.BlockSpec((None, c, w), ctx(0)))
    return pl.pallas_call(
        functools.partial(_ret_kernel, need_ctx=need_ctx),
        out_shape=tuple(out_shape),
        grid=(bsz, n + 1),
        in_specs=in_specs,
        out_specs=tuple(out_specs),
        scratch_shapes=[pltpu.VMEM((w, w), F32), pltpu.VMEM((w, w), F32),
                        pltpu.VMEM((2, 4, c, c), F32), pltpu.VMEM((2, 3, c, w), F32)],
        compiler_params=_cparams(("arbitrary", "arbitrary")),
        name="retention_ctx_out" if need_ctx else "retention",
    )(rdl, rdh, *([p_lat] * 6), *([p_ctx] * 3))


OUT_ROW_SLABS = 4
OUT_TILE = 1024


def _out_kernel(*refs, n_o, alpha):
    a_ref, b_ref, f_ref, g_ref = refs[:4]
    o_refs = refs[4:4 + n_o]
    x_ref, g1_ref, lnw_ref, lnb_ref, w_ref, gnw_ref, gavg_ref, out_ref = refs[4 + n_o:]
    gavg = gavg_ref[...]
    slabs = _row_slabs(x_ref.shape[0], OUT_ROW_SLABS)
    rows = x_ref.shape[0] // slabs
    def mixer_outputs(s):
        rs = slice(s * rows, (s + 1) * rows)
        o = o_refs[0][rs, :]
        for r in o_refs[1:]:
            o = o + r[rs, :]
        dlt = o - _group_mean(o, gavg)
        on = dlt * lax.rsqrt(_group_mean(dlt * dlt, gavg) + NORM_EPS) * gnw_ref[...]
        ret = (_silu(g_ref[rs, :]) * on).astype(BF16)
        return jnp.concatenate([a_ref[rs, :], b_ref[rs, :], f_ref[rs, :].astype(BF16), ret], axis=1)

    ahead = mixer_outputs(0)
    for s in range(slabs):
        rs = slice(s * rows, (s + 1) * rows)
        cat = ahead
        if s + 1 < slabs:
            ahead = mixer_outputs(s + 1)
        y = jnp.dot(cat, w_ref[...], preferred_element_type=F32)
        z = alpha * x_ref[rs, :] + g1_ref[...] * y
        out_ref[rs, :] = _ln(z) * lnw_ref[...] + lnb_ref[...]


def _out_proj(a, b, f, g, o_parts, x, mod, mod_row, lnw, lnb, w, layer, gnw, gavg, alpha, tm):
    bsz, length, d = x.shape
    row = lambda bb, i: (bb, i, 0)
    const2 = lambda bb, i: (0, 0)
    blk256 = pl.BlockSpec((None, tm, 256), row)
    in_specs = [blk256] * (4 + len(o_parts)) + [
        pl.BlockSpec((None, tm, d), row),
        pl.BlockSpec((None, None, 1, d), lambda bb, i: (mod_row(bb), 2, 0, 0)),
        pl.BlockSpec((1, d), const2), pl.BlockSpec((1, d), const2),
        pl.BlockSpec((None, d, d), lambda bb, i: (layer, 0, 0)),
        pl.BlockSpec((1, 256), const2), pl.BlockSpec((256, 256), const2)]
    return pl.pallas_call(
        functools.partial(_out_kernel, n_o=len(o_parts), alpha=alpha),
        out_shape=jax.ShapeDtypeStruct((bsz, length, d), F32),
        grid=(bsz, length // tm),
        in_specs=in_specs,
        out_specs=pl.BlockSpec((None, tm, d), row),
        compiler_params=_cparams(("arbitrary", "arbitrary")),
        name="out_proj",
    )(a, b, f, g, *o_parts, x, mod, lnw, lnb, w, gnw, gavg)


FFN_ROW_SLABS = 4
FFN_TILE = 1024


def _ffn_kernel(x_ref, sh_ref, sc_ref, g2_ref, lnw_ref, lnb_ref, wg_ref, wu_ref, wd_ref, out_ref, *, fc, alpha):
    slabs = _row_slabs(x_ref.shape[0], FFN_ROW_SLABS)
    rows = x_ref.shape[0] // slabs
    n_chunks = wg_ref.shape[1] // fc
    work = [(r, c) for r in range(slabs) for c in range(n_chunks)]
    xs, hs, accs = {}, {}, {}

    def gate_up(r, c):
        if r not in hs:
            xs[r] = x_ref[r * rows:(r + 1) * rows, :]
            hs[r] = (_ln(xs[r]) * (1.0 + sc_ref[...]) + sh_ref[...]).astype(BF16)
        cols = slice(c * fc, (c + 1) * fc)
        return (jnp.dot(hs[r], wg_ref[:, cols], preferred_element_type=F32),
                jnp.dot(hs[r], wu_ref[:, cols], preferred_element_type=F32))

    ahead = gate_up(*work[0])
    for idx, (r, c) in enumerate(work):
        gate, up = ahead
        if idx + 1 < len(work):
            ahead = gate_up(*work[idx + 1])
        act = (_silu(gate) * up).astype(BF16)
        down = jnp.dot(act, wd_ref[c * fc:(c + 1) * fc, :], preferred_element_type=F32)
        accs[r] = down if c == 0 else accs[r] + down
        if c == n_chunks - 1:
            z = alpha * xs[r] + g2_ref[...] * accs[r]
            out_ref[r * rows:(r + 1) * rows, :] = _ln(z) * lnw_ref[...] + lnb_ref[...]


def _ffn(x, mod, mod_row, lnw, lnb, wgu, wd, layer, alpha, tm):
    bsz, length, d = x.shape
    ff = wd.shape[1]
    fc = ff // 2 if (ff // 2) % MXU_TILE == 0 else ff
    row = lambda bb, i: (bb, i, 0)
    const2 = lambda bb, i: (0, 0)
    modspec = lambda which: pl.BlockSpec((None, None, 1, d), lambda bb, i: (mod_row(bb), which, 0, 0))
    resident = lambda shape, col: pl.BlockSpec(shape, lambda bb, i: (layer, 0, col), pipeline_mode=pl.Buffered(1))
    return pl.pallas_call(
        functools.partial(_ffn_kernel, fc=fc, alpha=alpha),
        out_shape=jax.ShapeDtypeStruct((bsz, length, d), F32),
        grid=(bsz, length // tm),
        in_specs=[pl.BlockSpec((None, tm, d), row), modspec(3), modspec(4), modspec(5),
                  pl.BlockSpec((1, d), const2), pl.BlockSpec((1, d), const2),
                  resident((None, d, ff), 0), resident((None, d, ff), 1), resident((None, ff, d), 0)],
        out_specs=pl.BlockSpec((None, tm, d), row),
        compiler_params=_cparams(("arbitrary", "arbitrary")),
        name="ffn",
    )(x, mod, mod, mod, lnw, lnb, wgu, wgu, wd)


def _rope_tables(seq):
    t = np.arange(seq)
    f32 = np.float32

    def tab(pos, n_freq):
        inv = f32(ROPE_THETA) ** (-np.arange(n_freq, dtype=f32) / f32(n_freq))
        ang = (pos.astype(f32)[:, None] * inv[None, :]).astype(np.float64)
        return np.cos(ang), np.sin(ang)

    cr, sr = tab(t // GRID_W, HEAD_DIM // 4)
    cc, sc = tab(t % GRID_W, HEAD_DIM // 4)
    ct, st = tab(t, HEAD_DIM // 2)
    tables = (np.concatenate([cr, cr, cc, cc], -1), np.concatenate([-sr, sr, -sc, sc], -1),
              np.concatenate([ct, ct], -1), np.concatenate([-st, st], -1))
    return tuple(jnp.asarray(np.tile(a, (1, 2)), F32) for a in tables)


def kernel(x, c, ctx, c_ctx, w_mod, b_mod, w_in, a_q_norm, a_k_norm, b_sink, f_mix, r_decay, r_gn_w, w_out,
           ln1_w, ln1_b, w_gate_up, w_down, ln2_w, ln2_b):
    bsz, seq, d = x.shape
    depth = w_in.shape[0]
    gw = d // 4
    assert gw == 4 * HEAD_DIM and a_q_norm.shape[-1] == HEAD_DIM and seq % (FFT_L2 * 8) == 0
    alpha = (2.0 * depth) ** 0.25

    tabs = _rope_tables(seq)
    gavg = jnp.asarray(np.kron(np.eye(gw // HEAD_DIM), np.full((HEAD_DIM, HEAD_DIM), 1.0 / HEAD_DIM)), BF16)
    cc = jnp.zeros((8, d), F32).at[:bsz].set(c).at[bsz].set(c_ctx)
    mod_all = _modulation(cc, w_mod, b_mod).reshape(depth, 8, 6, 1, d)
    lat_row = lambda b: b
    ctx_row = lambda b: bsz

    eye_g = jnp.eye(gw // HEAD_DIM, dtype=F32)
    w_in_b, w_out_b = w_in.astype(BF16), w_out.astype(BF16)
    w_gu_b, w_dn_b = w_gate_up.astype(BF16), w_down.astype(BF16)

    for layer in range(depth):
        need_ctx = layer < depth - 1
        mod = mod_all[layer]
        qn = jnp.tile(a_q_norm[layer], 4)[None, :]
        kn = jnp.tile(a_k_norm[layer], 2)[None, :]
        gnw = r_gn_w[layer][None, :]
        lnw1, lnb1 = ln1_w[layer][None, :], ln1_b[layer][None, :]
        lnw2, lnb2 = ln2_w[layer][None, :], ln2_b[layer][None, :]
        fm_bd = jnp.einsum('gh,gce->gche', eye_g, f_mix[layer]).reshape(gw, gw).astype(BF16)
        rd = r_decay[layer]
        rdl = jnp.repeat(rd, HEAD_DIM, axis=1)[:, None, :]
        rdh = jnp.broadcast_to(rd[:, :, None, None], (2, 4, 1, RET_CHUNK))
        sink = b_sink[layer]

        p_l, u_l, g_l = _in_proj(x, mod, lat_row, w_in_b, layer, qn, kn, gavg, tabs, min(IN_TILE, seq))
        lc = ctx.shape[1]
        flat = lambda t: t.reshape(1, bsz * lc, t.shape[-1])
        unflat = lambda t: t.reshape(bsz, lc, t.shape[-1])
        p_c, u_c, g_c = map(unflat, _in_proj(flat(ctx), mod, ctx_row, w_in_b, layer, qn, kn, gavg, None, bsz * lc))

        a_l = _attention(p_l, 0, p_l, p_c, 2, 3, None, tq=ATTN_Q_TILE, tk=min(ATTN_K_CHUNK, seq))
        b_l = _window_attention(p_l, p_c, sink)
        f_l = _fourier_latent(u_l, fm_bd)
        r_out = _retention(p_l, p_c, rdl, rdh, need_ctx)
        x = _out_proj(a_l, b_l, f_l, g_l, r_out[:2], x, mod, lat_row, lnw1, lnb1, w_out_b, layer, gnw, gavg, alpha,
                      min(OUT_TILE, seq))
        x = _ffn(x, mod, lat_row, lnw2, lnb2, w_gu_b, w_dn_b, layer, alpha, min(FFN_TILE, seq))
        if need_ctx:
            a_c = _attention(p_c, 0, None, p_c, 2, 3, None, tq=Q_BLOCK, tk=ATTN_K_CHUNK)
            b_c = _attention(p_c, 2, None, p_c, 6, 7, sink, tq=Q_BLOCK, tk=ATTN_K_CHUNK)
            f_c = _fourier_direct(u_c, fm_bd)
            ctx = _out_proj(flat(a_c), flat(b_c), flat(f_c), flat(g_c), (flat(r_out[2]),), flat(ctx), mod, ctx_row,
                            lnw1, lnb1, w_out_b, layer, gnw, gavg, alpha, bsz * lc)
            ctx = unflat(_ffn(ctx, mod, ctx_row, lnw2, lnb2, w_gu_b, w_dn_b, layer, alpha, bsz * lc))
    return x
```

```python
import functools

import numpy as np
import jax
import jax.numpy as jnp
from jax import lax
from jax.experimental import pallas as pl
from jax.experimental.pallas import tpu as pltpu

F32 = jnp.float32
BF16 = jnp.bfloat16

HEAD_DIM = 64
GRID_W = 64
Q_BLOCK = 128
ROPE_THETA = 10000.0
NORM_EPS = 1e-6
NEG_INF = -1e30
LOG2E = 1.4426950408889634

LANES = 128
MXU_TILE = 256
VMEM_LIMIT_BYTES = 56 * 1024 * 1024

RET_CHUNK = 256
RET_STEP_CHUNKS = 8
FFT_L2 = 128


def _cparams(sem):
    return pltpu.CompilerParams(dimension_semantics=sem, vmem_limit_bytes=VMEM_LIMIT_BYTES)


def _ln(x):
    mu = jnp.mean(x, axis=-1, keepdims=True)
    xc = x - mu
    var = jnp.mean(xc * xc, axis=-1, keepdims=True)
    return xc * lax.rsqrt(var + NORM_EPS)


def _silu(x):
    return x * jax.nn.sigmoid(x)


def _group_mean(t, g):
    hi = t.astype(BF16)
    lo = (t - hi.astype(F32)).astype(BF16)
    return (jnp.dot(hi, g, preferred_element_type=F32) + jnp.dot(lo, g, preferred_element_type=F32))


def _row_slabs(rows, wanted):
    slabs = max(1, min(wanted, rows // MXU_TILE))
    return slabs if rows % (16 * slabs) == 0 else 1


def _dot_nt(a, b):
    return lax.dot_general(a, b, (((1,), (1,)), ((), ())), preferred_element_type=F32)


def _mod_kernel(c_ref, w_ref, b_ref, o_ref):
    h = _silu(c_ref[...])
    w = w_ref[...]
    h_hi = h.astype(BF16)
    h_lo = (h - h_hi.astype(F32)).astype(BF16)
    w_hi = w.astype(BF16)
    w_lo = (w - w_hi.astype(F32)).astype(BF16)
    rows = h.shape[0]
    both = jnp.dot(jnp.concatenate([h_hi, h_lo], axis=0), w_hi, preferred_element_type=F32)
    o_ref[...] = both[:rows] + both[rows:] + jnp.dot(h_hi, w_lo, preferred_element_type=F32) + b_ref[...]


def _modulation(cc, w_mod, b_mod):
    depth, d, n = w_mod.shape
    tn = 2048
    return pl.pallas_call(
        _mod_kernel,
        out_shape=jax.ShapeDtypeStruct((depth, 8, n), F32),
        grid=(depth, n // tn),
        in_specs=[pl.BlockSpec((8, d), lambda l, j: (0, 0)),
                  pl.BlockSpec((None, d, tn), lambda l, j: (l, 0, j)),
                  pl.BlockSpec((None, 1, tn), lambda l, j: (l, 0, j))],
        out_specs=pl.BlockSpec((None, 8, tn), lambda l, j: (l, 0, j)),
        compiler_params=_cparams(("arbitrary", "arbitrary")),
        name="modulation",
    )(cc, w_mod, b_mod.reshape(depth, 1, n))


P_COLS = 14 * LANES


def _rope_lanes(t, c, ss, half):
    first = (lax.broadcasted_iota(jnp.int32, (t.shape[0], LANES), 1) & half) == 0
    outs = []
    for j in range(t.shape[1] // LANES):
        tj = t[:, j * LANES:(j + 1) * LANES]
        partner = jnp.where(first, pltpu.roll(tj, LANES - half, 1), pltpu.roll(tj, half, 1))
        outs.append(tj * c + partner * ss)
    return outs[0] if len(outs) == 1 else jnp.concatenate(outs, axis=1)


IN_ROW_SLABS = 4
IN_TILE = 1024


def _pair_heads_by_kv(q):
    a, b = q[:, 0:LANES], q[:, LANES:2 * LANES]
    lo = lax.broadcasted_iota(jnp.int32, a.shape, 1) < HEAD_DIM
    return jnp.concatenate([jnp.where(lo, a, pltpu.roll(b, HEAD_DIM, 1)),
                            jnp.where(lo, pltpu.roll(a, HEAD_DIM, 1), b)], axis=1)


def _in_kernel(*refs, rope):
    x_ref, sh_ref, sc_ref, w_ref, qn_ref, kn_ref, gavg_ref = refs[:7]
    if rope:
        c2_ref, ss2_ref, c1_ref, ss1_ref, p_ref, u_ref, g_ref = refs[7:]
    else:
        p_ref, u_ref, g_ref = refs[7:]
    def rms(t, w, g):
        return t * lax.rsqrt(_group_mean(t * t, g) + NORM_EPS) * w

    scale = HEAD_DIM ** -0.5
    qscale = scale * LOG2E
    slabs = _row_slabs(x_ref.shape[0], IN_ROW_SLABS)
    rows = x_ref.shape[0] // slabs
    for r in range(slabs):
        rs = slice(r * rows, (r + 1) * rows)

        def rope2(t):
            if not rope:
                return t
            return _rope_lanes(t, c2_ref[rs, :], ss2_ref[rs, :], HEAD_DIM // 4)

        def rope1(t):
            if not rope:
                return t
            return _rope_lanes(t, c1_ref[rs, :], ss1_ref[rs, :], HEAD_DIM // 2)

        h = _ln(x_ref[rs, :]) * (1.0 + sc_ref[...]) + sh_ref[...]
        y = jnp.dot(h.astype(BF16), w_ref[...], preferred_element_type=F32)
        qa = rope2(rms(y[:, 0:256], qn_ref[...], gavg_ref[...])) * qscale
        ka = rope2(rms(y[:, 256:384], kn_ref[...], gavg_ref[0:LANES, 0:LANES]))
        p_ref[rs, 0:256] = _pair_heads_by_kv(qa).astype(BF16)
        p_ref[rs, 256:384] = ka.astype(BF16)
        p_ref[rs, 384:512] = y[:, 384:512].astype(BF16)
        p_ref[rs, 512:768] = _pair_heads_by_kv(rope2(y[:, 512:768]) * qscale).astype(BF16)
        p_ref[rs, 768:896] = rope2(y[:, 768:896]).astype(BF16)
        p_ref[rs, 896:1024] = y[:, 896:1024].astype(BF16)
        u_ref[rs, :] = y[:, 1024:1280].astype(BF16)
        p_ref[rs, 1024:1280] = rope1(y[:, 1280:1536]).astype(BF16)
        p_ref[rs, 1280:1536] = (rope1(y[:, 1536:1792]) * scale).astype(BF16)
        p_ref[rs, 1536:1792] = y[:, 1792:2048].astype(BF16)
        g_ref[rs, :] = y[:, 2048:2304]


def _in_proj(x, mod, mod_row, w, layer, qn, kn, gavg, tabs, tm):
    bsz, length, d = x.shape
    nw = w.shape[2]
    nt = length // tm
    rope = tabs is not None
    row = lambda b, i: (b, i, 0)
    const2 = lambda b, i: (0, 0)
    in_specs = [pl.BlockSpec((None, tm, d), row),
                pl.BlockSpec((None, None, 1, d), lambda b, i: (mod_row(b), 0, 0, 0)),
                pl.BlockSpec((None, None, 1, d), lambda b, i: (mod_row(b), 1, 0, 0)),
                pl.BlockSpec((None, d, nw), lambda b, i: (layer, 0, 0)),
                pl.BlockSpec((1, 256), const2),
                pl.BlockSpec((1, LANES), const2),
                pl.BlockSpec((256, 256), const2)]
    args = [x, mod, mod, w, qn, kn, gavg]
    if rope:
        in_specs += [pl.BlockSpec((tm, LANES), lambda b, i: (i, 0))] * len(tabs)
        args += list(tabs)
    return pl.pallas_call(
        functools.partial(_in_kernel, rope=rope),
        out_shape=(jax.ShapeDtypeStruct((bsz, length, P_COLS), BF16),
                   jax.ShapeDtypeStruct((bsz, length, 256), BF16),
                   jax.ShapeDtypeStruct((bsz, length, 256), F32)),
        grid=(bsz, nt),
        in_specs=in_specs,
        out_specs=(pl.BlockSpec((None, tm, P_COLS), row),
                   pl.BlockSpec((None, tm, 256), row),
                   pl.BlockSpec((None, tm, 256), row)),
        compiler_params=_cparams(("arbitrary", "arbitrary")),
        name="in_proj_rope" if rope else "in_proj_ctx",
    )(*args)


def _stack_heads(q):
    qf = q.astype(F32)
    lo = lax.broadcasted_iota(jnp.int32, (q.shape[0], LANES), 1) < HEAD_DIM
    q0, q1 = qf[:, 0:LANES], qf[:, LANES:2 * LANES]
    z = jnp.zeros_like(q0)
    return jnp.concatenate([jnp.where(lo, q0, z), jnp.where(lo, q1, z),
                            jnp.where(lo, z, q0), jnp.where(lo, z, q1)], axis=0).astype(BF16)


def _aug_values(v):
    vf = v.astype(F32)
    lo = lax.broadcasted_iota(jnp.int32, vf.shape, 1) < HEAD_DIM
    one = jnp.ones_like(vf)
    return jnp.where(lo, vf, one).astype(BF16), jnp.where(lo, one, vf).astype(BF16)


def _finish_heads(acc0, acc1, e, tq):
    l0 = pltpu.roll(acc0, HEAD_DIM, 1)
    l1 = pltpu.roll(acc1, HEAD_DIM, 1)
    if e is not None:
        l0 = l0 + e[:2 * tq]
        l1 = l1 + e[2 * tq:]
    n0 = acc0 / l0
    n1 = acc1 / l1
    lo = lax.broadcasted_iota(jnp.int32, (tq, LANES), 1) < HEAD_DIM
    return jnp.concatenate([jnp.where(lo, n0[:tq], pltpu.roll(n0[tq:], HEAD_DIM, 1)),
                            jnp.where(lo, pltpu.roll(n1[:tq], HEAD_DIM, 1), n1[tq:])], axis=1)


def _sink_column(sink_ref, tq):
    return jnp.concatenate([jnp.full((tq, 1), sink_ref[h] * LOG2E, F32) for h in range(4)], axis=0)


ATTN_Q_TILE = 256
ATTN_K_CHUNK = 2048

def _attn_kernel(*refs, tq, tk, n_lat, has_sink):
    i = 0
    sink_ref = None
    if has_sink:
        sink_ref = refs[0]
        i = 1
    q_ref = refs[i]
    i += 1
    if n_lat:
        kl_ref, vl_ref = refs[i:i + 2]
        i += 2
    kc_ref, vc_ref, o_ref = refs[i:i + 3]
    i += 3
    if n_lat:
        v0l_ref, v1l_ref = refs[i:i + 2]
        i += 2
    v0c_ref, v1c_ref = refs[i:i + 2]

    @pl.when(pl.program_id(1) == 0)
    def _():
        if n_lat:
            a0, a1 = _aug_values(vl_ref[...])
            v0l_ref[...] = a0
            v1l_ref[...] = a1
        a0, a1 = _aug_values(vc_ref[...])
        v0c_ref[...] = a0
        v1c_ref[...] = a1

    qs = _stack_heads(q_ref[...])
    half = 2 * tq

    chunks = [(kl_ref, v0l_ref, v1l_ref, slice(c * tk, (c + 1) * tk)) for c in range(n_lat)]
    chunks.append((kc_ref, v0c_ref, v1c_ref, slice(None)))

    def scores(chunk):
        k_ref, _, _, rows = chunk
        return _dot_nt(qs, k_ref[rows, :])

    m = _sink_column(sink_ref, tq) if has_sink else jnp.full((4 * tq, 1), NEG_INF, F32)
    acc0 = jnp.zeros((half, LANES), F32)
    acc1 = jnp.zeros((half, LANES), F32)
    s_next = scores(chunks[0])
    for idx, (_, v0_ref, v1_ref, rows) in enumerate(chunks):
        s = s_next
        if idx + 1 < len(chunks):
            s_next = scores(chunks[idx + 1])
        m_new = jnp.maximum(m, jnp.max(s, axis=1, keepdims=True))
        alpha = jnp.exp2(m - m_new)
        p = jnp.exp2(s - m_new).astype(BF16)
        pv = jnp.dot(p, jnp.concatenate([v0_ref[rows, :], v1_ref[rows, :]], axis=1), preferred_element_type=F32)
        acc0 = acc0 * alpha[:half] + pv[:half, :LANES]
        acc1 = acc1 * alpha[half:] + pv[half:, LANES:]
        m = m_new
    e = jnp.exp2(_sink_column(sink_ref, tq) - m) if has_sink else None
    o_ref[...] = _finish_heads(acc0, acc1, e, tq).astype(BF16)


def _attention(pq, q_blk, p_lat, p_ctx, k_blk, v_blk, sink, tq, tk):
    bsz, lq, _ = pq.shape
    lc = p_ctx.shape[1]
    assert p_lat is None or p_lat.shape[1] % tk == 0
    n_lat = 0 if p_lat is None else p_lat.shape[1] // tk
    has_sink = sink is not None
    in_specs = [pl.BlockSpec((None, tq, 256), lambda b, i, *_: (b, i, q_blk))]
    args = [pq]
    scratch = []
    if n_lat:
        ll = p_lat.shape[1]
        in_specs += [pl.BlockSpec((None, ll, LANES), lambda b, i, *_: (b, 0, k_blk)),
                     pl.BlockSpec((None, ll, LANES), lambda b, i, *_: (b, 0, v_blk))]
        args += [p_lat, p_lat]
        scratch += [pltpu.VMEM((ll, LANES), BF16), pltpu.VMEM((ll, LANES), BF16)]
    in_specs += [pl.BlockSpec((None, lc, LANES), lambda b, i, *_: (b, 0, k_blk)),
                 pl.BlockSpec((None, lc, LANES), lambda b, i, *_: (b, 0, v_blk))]
    args += [p_ctx, p_ctx]
    scratch += [pltpu.VMEM((lc, LANES), BF16), pltpu.VMEM((lc, LANES), BF16)]
    kern = functools.partial(_attn_kernel, tq=tq, tk=tk, n_lat=n_lat, has_sink=has_sink)
    grid_spec = pltpu.PrefetchScalarGridSpec(
        num_scalar_prefetch=1 if has_sink else 0,
        grid=(bsz, lq // tq),
        in_specs=in_specs,
        out_specs=pl.BlockSpec((None, tq, 256), lambda b, i, *_: (b, i, 0)),
        scratch_shapes=scratch)
    call = pl.pallas_call(
        kern, out_shape=jax.ShapeDtypeStruct((bsz, lq, 256), BF16), grid_spec=grid_spec,
        compiler_params=_cparams(("arbitrary", "arbitrary")),
        name="attn_sink" if has_sink else ("attn_global" if n_lat else "attn_ctx"))
    return call(sink, *args) if has_sink else call(*args)


WIN_BLOCKS_PER_STEP = 16


def _win_kernel(sink_ref, q_ref, kl_ref, vl_ref, kc_ref, vc_ref, o_ref, *, nb):
    tq = Q_BLOCK
    r = lax.broadcasted_iota(jnp.int32, (4 * tq, tq), 0) & (tq - 1)
    j = lax.broadcasted_iota(jnp.int32, (4 * tq, tq), 1)
    in_prev = j >= r
    in_next = j <= r
    snk = _sink_column(sink_ref, tq)
    kc, vc = kc_ref[...], vc_ref[...]

    def rows(ref, blk):
        return ref[pl.ds(pl.multiple_of(blk * tq, tq), tq), :]

    def scores(t):
        i = pl.program_id(1) * WIN_BLOCKS_PER_STEP + t
        prev = jnp.maximum(i - 1, 0)
        nxt = jnp.minimum(i + 1, nb - 1)
        k = jnp.concatenate([rows(kl_ref, prev), rows(kl_ref, i), rows(kl_ref, nxt), kc], axis=0)
        qs = _stack_heads(q_ref[t * tq:(t + 1) * tq, :])
        return _dot_nt(qs, k), i, prev, nxt

    ahead = scores(0)
    for t in range(WIN_BLOCKS_PER_STEP):
        s, i, prev, nxt = ahead
        if t + 1 < WIN_BLOCKS_PER_STEP:
            ahead = scores(t + 1)
        v = jnp.concatenate([rows(vl_ref, prev), rows(vl_ref, i), rows(vl_ref, nxt), vc], axis=0)
        off_prev = jnp.where(i > 0, 0.0, NEG_INF)
        off_next = jnp.where(i < nb - 1, 0.0, NEG_INF)
        s = jnp.concatenate([jnp.where(in_prev, s[:, 0:tq] + off_prev, NEG_INF), s[:, tq:2 * tq],
                             jnp.where(in_next, s[:, 2 * tq:3 * tq] + off_next, NEG_INF), s[:, 3 * tq:]], axis=1)
        m = jnp.maximum(jnp.max(s, axis=1, keepdims=True), snk)
        p = jnp.exp2(s - m).astype(BF16)
        v0, v1 = _aug_values(v)
        acc0 = jnp.dot(p[:2 * tq], v0, preferred_element_type=F32)
        acc1 = jnp.dot(p[2 * tq:], v1, preferred_element_type=F32)
        o_ref[t * tq:(t + 1) * tq, :] = _finish_heads(acc0, acc1, jnp.exp2(snk - m), tq).astype(BF16)


def _window_attention(p_lat, p_ctx, sink):
    bsz, ll, _ = p_lat.shape
    lc = p_ctx.shape[1]
    nb = ll // Q_BLOCK
    tqs = WIN_BLOCKS_PER_STEP * Q_BLOCK
    assert ll % tqs == 0
    grid_spec = pltpu.PrefetchScalarGridSpec(
        num_scalar_prefetch=1,
        grid=(bsz, ll // tqs),
        in_specs=[pl.BlockSpec((None, tqs, 256), lambda b, i, *_: (b, i, 2)),
                  pl.BlockSpec((None, ll, LANES), lambda b, i, *_: (b, 0, 6)),
                  pl.BlockSpec((None, ll, LANES), lambda b, i, *_: (b, 0, 7)),
                  pl.BlockSpec((None, lc, LANES), lambda b, i, *_: (b, 0, 6)),
                  pl.BlockSpec((None, lc, LANES), lambda b, i, *_: (b, 0, 7))],
        out_specs=pl.BlockSpec((None, tqs, 256), lambda b, i, *_: (b, i, 0)))
    return pl.pallas_call(
        functools.partial(_win_kernel, nb=nb),
        out_shape=jax.ShapeDtypeStruct((bsz, ll, 256), BF16), grid_spec=grid_spec,
        compiler_params=_cparams(("arbitrary", "arbitrary")),
        name="attn_window",
    )(sink, p_lat, p_lat, p_lat, p_ctx, p_ctx)


FFT_ROWS = 8
FFT1_STEP_ROWS = 16


def _fft1_kernel(u_ref, w_ref, y_ref):
    l1, rows, w = u_ref.shape
    uf = u_ref[...].astype(F32)
    ys = []
    for h in range(rows // FFT_ROWS):
        u = uf[:, h * FFT_ROWS:(h + 1) * FFT_ROWS, :].reshape(l1 * FFT_ROWS, w).astype(BF16)
        ys.append(jnp.dot(w_ref[...], u, preferred_element_type=F32).reshape(2, l1, FFT_ROWS, w))
    y_ref[...] = jnp.concatenate(ys, axis=2).astype(BF16)


def _channel_mix(ab, g_ref, fm_ref):
    z = jnp.dot(ab.astype(BF16), g_ref[...], preferred_element_type=F32)
    return jnp.dot(z.astype(BF16), fm_ref[...], preferred_element_type=F32)


def _fft2_kernel(y_ref, c_ref, s_ref, g_ref, fm_ref, o_ref):
    l2 = y_ref.shape[2]
    ab = []
    for r in range(FFT_ROWS):
        yr, yi = y_ref[0, r], y_ref[1, r]
        cs = jnp.concatenate([c_ref[r], s_ref[r]], axis=1)
        rhs = jnp.concatenate([jnp.concatenate([yr, yi], axis=1),
                               jnp.concatenate([yi, -yr], axis=1)], axis=0)
        ab.append(jnp.dot(cs, rhs, preferred_element_type=F32))
    o = _channel_mix(jnp.concatenate(ab, axis=0), g_ref, fm_ref)
    for r in range(FFT_ROWS):
        o_ref[:, r, :] = o[r * l2:(r + 1) * l2]


def _fft_direct_kernel(u_ref, cs_ref, g_ref, fm_ref, o_ref):
    n = u_ref.shape[0]
    y = jnp.dot(cs_ref[...], u_ref[...].astype(BF16), preferred_element_type=F32)
    o_ref[...] = _channel_mix(jnp.concatenate([y[:n], y[n:]], axis=1), g_ref, fm_ref)


def _mxu_const(a):
    return jnp.asarray(a, F32).astype(BF16)


def _dft_tables(n_rows, n_cols, length, row_stride=1, row_offset=0):
    k = row_offset + row_stride * np.arange(n_rows, dtype=np.int64)
    n = np.arange(n_cols, dtype=np.int64)
    ang = 2.0 * np.pi * ((k[:, None] * n[None, :]) % length).astype(np.float64) / length
    return np.cos(ang), np.sin(ang)


def _channel_dft(width, length):
    c, s = _dft_tables(HEAD_DIM, HEAD_DIM, HEAD_DIM)
    eye = np.eye(width // HEAD_DIM) / np.sqrt(float(length) * HEAD_DIM)
    return _mxu_const(np.concatenate([np.kron(eye, c), np.kron(eye, s)], axis=0))


def _fourier_latent(u, fm_bd):
    bsz, length, w = u.shape
    l2 = FFT_L2
    l1 = length // l2
    rows = FFT_ROWS
    c1, s1 = _dft_tables(l1, l1, l1)
    w1 = _mxu_const(np.kron(np.concatenate([c1, -s1], axis=0), np.eye(rows)))
    y = pl.pallas_call(
        _fft1_kernel,
        out_shape=jax.ShapeDtypeStruct((bsz, 2, l1, l2, w), BF16),
        grid=(bsz, l2 // FFT1_STEP_ROWS),
        in_specs=[pl.BlockSpec((None, l1, FFT1_STEP_ROWS, w), lambda b, j: (b, 0, j, 0)),
                  pl.BlockSpec(w1.shape, lambda b, j: (0, 0))],
        out_specs=pl.BlockSpec((None, 2, l1, FFT1_STEP_ROWS, w), lambda b, j: (b, 0, 0, j, 0)),
        compiler_params=_cparams(("arbitrary", "arbitrary")),
        name="fourier_stage1",
    )(u.reshape(bsz, l1, l2, w), w1)
    tabs = [_dft_tables(l2, l2, length, row_stride=l1, row_offset=k1) for k1 in range(l1)]
    ck = _mxu_const(np.stack([t[0] for t in tabs]))
    sk = _mxu_const(np.stack([t[1] for t in tabs]))
    const2 = lambda b, k: (0, 0)
    out = pl.pallas_call(
        _fft2_kernel,
        out_shape=jax.ShapeDtypeStruct((bsz, l2, l1, w), F32),
        grid=(bsz, l1 // rows),
        in_specs=[pl.BlockSpec((None, 2, rows, l2, w), lambda b, k: (b, 0, k, 0, 0)),
                  pl.BlockSpec((rows, l2, l2), lambda b, k: (k, 0, 0)),
                  pl.BlockSpec((rows, l2, l2), lambda b, k: (k, 0, 0)),
                  pl.BlockSpec((2 * w, w), const2), pl.BlockSpec((w, w), const2)],
        out_specs=pl.BlockSpec((None, l2, rows, w), lambda b, k: (b, 0, k, 0)),
        compiler_params=_cparams(("arbitrary", "arbitrary")),
        name="fourier_stage2",
    )(y, ck, sk, _channel_dft(w, length), fm_bd)
    return out.reshape(bsz, length, w)


def _fourier_direct(u, fm_bd):
    bsz, length, w = u.shape
    c, s = _dft_tables(length, length, length)
    cs = _mxu_const(np.concatenate([c, -s], axis=0))
    const2 = lambda b: (0, 0)
    return pl.pallas_call(
        _fft_direct_kernel,
        out_shape=jax.ShapeDtypeStruct((bsz, length, w), F32),
        grid=(bsz,),
        in_specs=[pl.BlockSpec((None, length, w), lambda b: (b, 0, 0)),
                  pl.BlockSpec((2 * length, length), const2),
                  pl.BlockSpec((2 * w, w), const2), pl.BlockSpec((w, w), const2)],
        out_specs=pl.BlockSpec((None, length, w), lambda b: (b, 0, 0)),
        compiler_params=_cparams(("arbitrary",)),
        name="fourier_ctx",
    )(u, cs, _channel_dft(w, length), fm_bd)


def _log_sigmoid(x):
    return jnp.minimum(x, 0.0) - jnp.log1p(jnp.exp(-jnp.abs(x)))


def _ret_kernel(*refs, need_ctx):
    (rdl_ref, rdh_ref, qf_ref, kf_ref, vf_ref, qb_ref, kb_ref, vb_ref, qc_ref, kc_ref, vc_ref) = refs[:11]
    if need_ctx:
        of_ref, ob_ref, oc_ref = refs[11:14]
        scr = refs[14:]
    else:
        of_ref, ob_ref = refs[11:13]
        oc_ref = None
        scr = refs[13:]
    sf_ref, sb_ref, din_ref, tab_ref = scr
    c = RET_CHUNK
    w = 4 * HEAD_DIM
    j = pl.program_id(1)
    head_shift = HEAD_DIM.bit_length() - 1
    lane_head = lax.broadcasted_iota(jnp.int32, (c, w), 1) >> head_shift
    blockdiag = ((lax.broadcasted_iota(jnp.int32, (w, w), 0) >> head_shift)
                 == (lax.broadcasted_iota(jnp.int32, (w, w), 1) >> head_shift))

    def decayed_scores(q, k, d):
        qf = q.astype(F32)
        inner = []
        for h in range(4):
            qh = jnp.where(lane_head == h, qf, 0.0).astype(BF16)
            inner.append((_dot_nt(qh, k) * din_ref[d, h]).astype(BF16))
        return inner, (k.astype(F32) * tab_ref[d, 1]).T.astype(BF16)

    def chunk_outputs(scores, v):
        inner, kz = scores
        o = jnp.zeros((c, w), F32)
        for h in range(4):
            o = o + jnp.where(lane_head == h, jnp.dot(inner[h], v, preferred_element_type=F32), 0.0)
        kv = jnp.where(blockdiag, jnp.dot(kz, v, preferred_element_type=F32), 0.0)
        return o, kv

    def chunk(q, k, v, d):
        return chunk_outputs(decayed_scores(q, k, d), v)

    @pl.when(j == 0)
    def _():
        t = lax.broadcasted_iota(jnp.int32, (c, w), 0).astype(F32)
        rr = lax.broadcasted_iota(jnp.int32, (c, c), 0)
        cc = lax.broadcasted_iota(jnp.int32, (c, c), 1)
        for d in range(2):
            lg = _log_sigmoid(rdl_ref[d])
            tab_ref[d, 0] = jnp.exp(lg * ((t + 1.0) if d == 0 else (c - t)))
            tab_ref[d, 1] = jnp.exp(lg * ((c - 1.0 - t) if d == 0 else t))
            tab_ref[d, 2] = jnp.exp(jnp.broadcast_to(lg, (c, w)) * float(c))
            diff = (rr - cc) if d == 0 else (cc - rr)
            dpos = jnp.maximum(diff, 0).astype(F32)
            for h in range(4):
                lgh = _log_sigmoid(rdh_ref[d, h])
                din_ref[d, h] = jnp.where(diff >= 0, jnp.exp(lgh * dpos), 0.0)
        q, k, v = qc_ref[...], kc_ref[...], vc_ref[...]
        o_f, kv_f = chunk(q, k, v, 0)
        o_b, kv_b = chunk(q, k, v, 1)
        sf_ref[...] = kv_f
        sb_ref[...] = kv_b
        if need_ctx:
            oc_ref[...] = o_f + o_b

    @pl.when(j > 0)
    def _():
        n_sub = qf_ref.shape[0] // c
        io = ((qf_ref, kf_ref, vf_ref, of_ref), (qb_ref, kb_ref, vb_ref, ob_ref))
        states = [sf_ref[...], sb_ref[...]]
        work = [(d, t if d == 0 else n_sub - 1 - t) for t in range(n_sub) for d in range(2)]

        def start(d, t):
            rs = slice(t * c, (t + 1) * c)
            q, k, v = io[d][0][rs, :], io[d][1][rs, :], io[d][2][rs, :]
            return q, v, decayed_scores(q, k, d)

        ahead = start(*work[0])
        for idx, (d, t) in enumerate(work):
            q, v, scores = ahead
            if idx + 1 < len(work):
                ahead = start(*work[idx + 1])
            o, kv = chunk_outputs(scores, v)
            cross = jnp.dot(q, states[d].astype(BF16), preferred_element_type=F32) * tab_ref[d, 0]
            io[d][3][t * c:(t + 1) * c, :] = o + cross
            states[d] = states[d] * tab_ref[d, 2, 0:1, :] + kv
        sf_ref[...] = states[0]
        sb_ref[...] = states[1]


def _retention(p_lat, p_ctx, rdl, rdh, need_ctx):
    bsz, ll, _ = p_lat.shape
    lc = p_ctx.shape[1]
    c = RET_CHUNK
    w = 4 * HEAD_DIM
    step_chunks = next(k for k in (RET_STEP_CHUNKS, 4, 2, 1) if ll % (c * k) == 0)
    cs = c * step_chunks
    assert lc == c and ll % cs == 0
    n = ll // cs
    fwd = lambda blk: (lambda b, j: (b, jnp.maximum(j - 1, 0), blk))
    bwd = lambda blk: (lambda b, j: (b, n - 1 - jnp.maximum(j - 1, 0), blk))
    ctx = lambda blk: (lambda b, j: (b, 0, blk))
    in_specs = [pl.BlockSpec((2, 1, w), lambda b, j: (0, 0, 0)),
                pl.BlockSpec((2, 4, 1, c), lambda b, j: (0, 0, 0, 0))]
    in_specs += [pl.BlockSpec((None, cs, w), fwd(blk)) for blk in (4, 5, 6)]
    in_specs += [pl.BlockSpec((None, cs, w), bwd(blk)) for blk in (4, 5, 6)]
    in_specs += [pl.BlockSpec((None, c, w), ctx(blk)) for blk in (4, 5, 6)]
    out_shape = [jax.ShapeDtypeStruct((bsz, ll, w), F32), jax.ShapeDtypeStruct((bsz, ll, w), F32)]
    out_specs = [pl.BlockSpec((None, cs, w), fwd(0)), pl.BlockSpec((None, cs, w), bwd(0))]
    if need_ctx:
        out_shape.append(jax.ShapeDtypeStruct((bsz, lc, w), F32))
        out_specs.append(pl.BlockSpec((None, c, w), ctx(0)))
    return pl.pallas_call(
        functools.partial(_ret_kernel, need_ctx=need_ctx),
        out_shape=tuple(out_shape),
        grid=(bsz, n + 1),
        in_specs=in_specs,
        out_specs=tuple(out_specs),
        scratch_shapes=[pltpu.VMEM((w, w), F32), pltpu.VMEM((w, w), F32),
                        pltpu.VMEM((2, 4, c, c), F32), pltpu.VMEM((2, 3, c, w), F32)],
        compiler_params=_cparams(("arbitrary", "arbitrary")),
        name="retention_ctx_out" if need_ctx else "retention",
    )(rdl, rdh, *([p_lat] * 6), *([p_ctx] * 3))


OUT_ROW_SLABS = 4
OUT_TILE = 1024


def _out_kernel(*refs, n_o, alpha):
    a_ref, b_ref, f_ref, g_ref = refs[:4]
    o_refs = refs[4:4 + n_o]
    x_ref, g1_ref, lnw_ref, lnb_ref, w_ref, gnw_ref, gavg_ref, out_ref = refs[4 + n_o:]
    gavg = gavg_ref[...]
    slabs = _row_slabs(x_ref.shape[0], OUT_ROW_SLABS)
    rows = x_ref.shape[0] // slabs
    def mixer_outputs(s):
        rs = slice(s * rows, (s + 1) * rows)
        o = o_refs[0][rs, :]
        for r in o_refs[1:]:
            o = o + r[rs, :]
        dlt = o - _group_mean(o, gavg)
        on = dlt * lax.rsqrt(_group_mean(dlt * dlt, gavg) + NORM_EPS) * gnw_ref[...]
        ret = (_silu(g_ref[rs, :]) * on).astype(BF16)
        return jnp.concatenate([a_ref[rs, :], b_ref[rs, :], f_ref[rs, :].astype(BF16), ret], axis=1)

    ahead = mixer_outputs(0)
    for s in range(slabs):
        rs = slice(s * rows, (s + 1) * rows)
        cat = ahead
        if s + 1 < slabs:
            ahead = mixer_outputs(s + 1)
        y = jnp.dot(cat, w_ref[...], preferred_element_type=F32)
        z = alpha * x_ref[rs, :] + g1_ref[...] * y
        out_ref[rs, :] = _ln(z) * lnw_ref[...] + lnb_ref[...]


def _out_proj(a, b, f, g, o_parts, x, mod, mod_row, lnw, lnb, w, layer, gnw, gavg, alpha, tm):
    bsz, length, d = x.shape
    row = lambda bb, i: (bb, i, 0)
    const2 = lambda bb, i: (0, 0)
    blk256 = pl.BlockSpec((None, tm, 256), row)
    in_specs = [blk256] * (4 + len(o_parts)) + [
        pl.BlockSpec((None, tm, d), row),
        pl.BlockSpec((None, None, 1, d), lambda bb, i: (mod_row(bb), 2, 0, 0)),
        pl.BlockSpec((1, d), const2), pl.BlockSpec((1, d), const2),
        pl.BlockSpec((None, d, d), lambda bb, i: (layer, 0, 0)),
        pl.BlockSpec((1, 256), const2), pl.BlockSpec((256, 256), const2)]
    return pl.pallas_call(
        functools.partial(_out_kernel, n_o=len(o_parts), alpha=alpha),
        out_shape=jax.ShapeDtypeStruct((bsz, length, d), F32),
        grid=(bsz, length // tm),
        in_specs=in_specs,
        out_specs=pl.BlockSpec((None, tm, d), row),
        compiler_params=_cparams(("arbitrary", "arbitrary")),
        name="out_proj",
    )(a, b, f, g, *o_parts, x, mod, lnw, lnb, w, gnw, gavg)


FFN_ROW_SLABS = 4
FFN_TILE = 1024


def _ffn_kernel(x_ref, sh_ref, sc_ref, g2_ref, lnw_ref, lnb_ref, wg_ref, wu_ref, wd_ref, out_ref, *, fc, alpha):
    slabs = _row_slabs(x_ref.shape[0], FFN_ROW_SLABS)
    rows = x_ref.shape[0] // slabs
    n_chunks = wg_ref.shape[1] // fc
    work = [(r, c) for r in range(slabs) for c in range(n_chunks)]
    xs, hs, accs = {}, {}, {}

    def gate_up(r, c):
        if r not in hs:
            xs[r] = x_ref[r * rows:(r + 1) * rows, :]
            hs[r] = (_ln(xs[r]) * (1.0 + sc_ref[...]) + sh_ref[...]).astype(BF16)
        cols = slice(c * fc, (c + 1) * fc)
        return (jnp.dot(hs[r], wg_ref[:, cols], preferred_element_type=F32),
                jnp.dot(hs[r], wu_ref[:, cols], preferred_element_type=F32))

    ahead = gate_up(*work[0])
    for idx, (r, c) in enumerate(work):
        gate, up = ahead
        if idx + 1 < len(work):
            ahead = gate_up(*work[idx + 1])
        act = (_silu(gate) * up).astype(BF16)
        down = jnp.dot(act, wd_ref[c * fc:(c + 1) * fc, :], preferred_element_type=F32)
        accs[r] = down if c == 0 else accs[r] + down
        if c == n_chunks - 1:
            z = alpha * xs[r] + g2_ref[...] * accs[r]
            out_ref[r * rows:(r + 1) * rows, :] = _ln(z) * lnw_ref[...] + lnb_ref[...]


def _ffn(x, mod, mod_row, lnw, lnb, wgu, wd, layer, alpha, tm):
    bsz, length, d = x.shape
    ff = wd.shape[1]
    fc = ff // 2 if (ff // 2) % MXU_TILE == 0 else ff
    row = lambda bb, i: (bb, i, 0)
    const2 = lambda bb, i: (0, 0)
    modspec = lambda which: pl.BlockSpec((None, None, 1, d), lambda bb, i: (mod_row(bb), which, 0, 0))
    resident = lambda shape, col: pl.BlockSpec(shape, lambda bb, i: (layer, 0, col), pipeline_mode=pl.Buffered(1))
    return pl.pallas_call(
        functools.partial(_ffn_kernel, fc=fc, alpha=alpha),
        out_shape=jax.ShapeDtypeStruct((bsz, length, d), F32),
        grid=(bsz, length // tm),
        in_specs=[pl.BlockSpec((None, tm, d), row), modspec(3), modspec(4), modspec(5),
                  pl.BlockSpec((1, d), const2), pl.BlockSpec((1, d), const2),
                  resident((None, d, ff), 0), resident((None, d, ff), 1), resident((None, ff, d), 0)],
        out_specs=pl.BlockSpec((None, tm, d), row),
        compiler_params=_cparams(("arbitrary", "arbitrary")),
        name="ffn",
    )(x, mod, mod, mod, lnw, lnb, wgu, wgu, wd)


TAIL_ROW_SLABS = 2
TAIL_TILE = 512


def _tail_kernel(*refs, n_o, fc, alpha):
    a_ref, b_ref, f_ref, g_ref = refs[:4]
    o_refs = refs[4:4 + n_o]
    (x_ref, g1_ref, sh2_ref, sc2_ref, g2_ref, ln1w_ref, ln1b_ref, ln2w_ref, ln2b_ref, gnw_ref, gavg_ref,
     wo_ref, wg_ref, wu_ref, wd_ref, out_ref) = refs[4 + n_o:]
    slabs = _row_slabs(x_ref.shape[0], TAIL_ROW_SLABS)
    rows = x_ref.shape[0] // slabs
    n_chunks = wg_ref.shape[1] // fc
    gavg = gavg_ref[...]

    def mixed(r):
        rs = slice(r * rows, (r + 1) * rows)
        o = o_refs[0][rs, :]
        for ref in o_refs[1:]:
            o = o + ref[rs, :]
        dlt = o - _group_mean(o, gavg)
        on = dlt * lax.rsqrt(_group_mean(dlt * dlt, gavg) + NORM_EPS) * gnw_ref[...]
        ret = (_silu(g_ref[rs, :]) * on).astype(BF16)
        cat = jnp.concatenate([a_ref[rs, :], b_ref[rs, :], f_ref[rs, :].astype(BF16), ret], axis=1)
        y = jnp.dot(cat, wo_ref[...], preferred_element_type=F32)
        z = alpha * x_ref[rs, :] + g1_ref[...] * y
        return _ln(z) * ln1w_ref[...] + ln1b_ref[...]

    work = [(r, c) for r in range(slabs) for c in range(n_chunks)]
    xs, hs, accs = {}, {}, {}

    def gate_up(r, c):
        if r not in hs:
            xs[r] = mixed(r)
            hs[r] = (_ln(xs[r]) * (1.0 + sc2_ref[...]) + sh2_ref[...]).astype(BF16)
        cols = slice(c * fc, (c + 1) * fc)
        return (jnp.dot(hs[r], wg_ref[:, cols], preferred_element_type=F32),
                jnp.dot(hs[r], wu_ref[:, cols], preferred_element_type=F32))

    ahead = gate_up(*work[0])
    for idx, (r, c) in enumerate(work):
        gate, up = ahead
        if idx + 1 < len(work):
            ahead = gate_up(*work[idx + 1])
        act = (_silu(gate) * up).astype(BF16)
        down = jnp.dot(act, wd_ref[c * fc:(c + 1) * fc, :], preferred_element_type=F32)
        accs[r] = down if c == 0 else accs[r] + down
        if c == n_chunks - 1:
            z = alpha * xs[r] + g2_ref[...] * accs[r]
            out_ref[r * rows:(r + 1) * rows, :] = _ln(z) * ln2w_ref[...] + ln2b_ref[...]


def _layer_tail(a, b, f, g, o_parts, x, mod, mod_row, ln1w, ln1b, ln2w, ln2b, w_out, w_gu, w_dn, layer, gnw, gavg,
                alpha, tm):
    bsz, length, d = x.shape
    ff = w_dn.shape[1]
    fc = ff // 2 if (ff // 2) % MXU_TILE == 0 else ff
    row = lambda bb, i: (bb, i, 0)
    const2 = lambda bb, i: (0, 0)
    blk256 = pl.BlockSpec((None, tm, 256), row)
    modspec = lambda which: pl.BlockSpec((None, None, 1, d), lambda bb, i: (mod_row(bb), which, 0, 0))
    vec = pl.BlockSpec((1, d), const2)
    resident = lambda shape, col: pl.BlockSpec(shape, lambda bb, i: (layer, 0, col), pipeline_mode=pl.Buffered(1))
    in_specs = [blk256] * (4 + len(o_parts)) + [
        pl.BlockSpec((None, tm, d), row), modspec(2), modspec(3), modspec(4), modspec(5), vec, vec, vec, vec,
        pl.BlockSpec((1, 256), const2), pl.BlockSpec((256, 256), const2),
        resident((None, d, d), 0), resident((None, d, ff), 0), resident((None, d, ff), 1), resident((None, ff, d), 0)]
    return pl.pallas_call(
        functools.partial(_tail_kernel, n_o=len(o_parts), fc=fc, alpha=alpha),
        out_shape=jax.ShapeDtypeStruct((bsz, length, d), F32),
        grid=(bsz, length // tm),
        in_specs=in_specs,
        out_specs=pl.BlockSpec((None, tm, d), row),
        compiler_params=_cparams(("arbitrary", "arbitrary")),
        name="layer_tail",
    )(a, b, f, g, *o_parts, x, mod, mod, mod, mod, ln1w, ln1b, ln2w, ln2b, gnw, gavg, w_out, w_gu, w_gu, w_dn)


def _rope_tables(seq):
    t = np.arange(seq)
    f32 = np.float32

    def tab(pos, n_freq):
        inv = f32(ROPE_THETA) ** (-np.arange(n_freq, dtype=f32) / f32(n_freq))
        ang = (pos.astype(f32)[:, None] * inv[None, :]).astype(np.float64)
        return np.cos(ang), np.sin(ang)

    cr, sr = tab(t // GRID_W, HEAD_DIM // 4)
    cc, sc = tab(t % GRID_W, HEAD_DIM // 4)
    ct, st = tab(t, HEAD_DIM // 2)
    tables = (np.concatenate([cr, cr, cc, cc], -1), np.concatenate([-sr, sr, -sc, sc], -1),
              np.concatenate([ct, ct], -1), np.concatenate([-st, st], -1))
    return tuple(jnp.asarray(np.tile(a, (1, 2)), F32) for a in tables)


def kernel(x, c, ctx, c_ctx, w_mod, b_mod, w_in, a_q_norm, a_k_norm, b_sink, f_mix, r_decay, r_gn_w, w_out,
           ln1_w, ln1_b, w_gate_up, w_down, ln2_w, ln2_b):
    bsz, seq, d = x.shape
    depth = w_in.shape[0]
    gw = d // 4
    assert gw == 4 * HEAD_DIM and a_q_norm.shape[-1] == HEAD_DIM and seq % (FFT_L2 * 8) == 0
    alpha = (2.0 * depth) ** 0.25

    tabs = _rope_tables(seq)
    gavg = jnp.asarray(np.kron(np.eye(gw // HEAD_DIM), np.full((HEAD_DIM, HEAD_DIM), 1.0 / HEAD_DIM)), BF16)
    cc = jnp.zeros((8, d), F32).at[:bsz].set(c).at[bsz].set(c_ctx)
    mod_all = _modulation(cc, w_mod, b_mod).reshape(depth, 8, 6, 1, d)
    lat_row = lambda b: b
    ctx_row = lambda b: bsz

    eye_g = jnp.eye(gw // HEAD_DIM, dtype=F32)
    w_in_b, w_out_b = w_in.astype(BF16), w_out.astype(BF16)
    w_gu_b, w_dn_b = w_gate_up.astype(BF16), w_down.astype(BF16)

    for layer in range(depth):
        need_ctx = layer < depth - 1
        mod = mod_all[layer]
        qn = jnp.tile(a_q_norm[layer], 4)[None, :]
        kn = jnp.tile(a_k_norm[layer], 2)[None, :]
        gnw = r_gn_w[layer][None, :]
        lnw1, lnb1 = ln1_w[layer][None, :], ln1_b[layer][None, :]
        lnw2, lnb2 = ln2_w[layer][None, :], ln2_b[layer][None, :]
        fm_bd = jnp.einsum('gh,gce->gche', eye_g, f_mix[layer]).reshape(gw, gw).astype(BF16)
        rd = r_decay[layer]
        rdl = jnp.repeat(rd, HEAD_DIM, axis=1)[:, None, :]
        rdh = jnp.broadcast_to(rd[:, :, None, None], (2, 4, 1, RET_CHUNK))
        sink = b_sink[layer]

        p_l, u_l, g_l = _in_proj(x, mod, lat_row, w_in_b, layer, qn, kn, gavg, tabs, min(IN_TILE, seq))
        lc = ctx.shape[1]
        flat = lambda t: t.reshape(1, bsz * lc, t.shape[-1])
        unflat = lambda t: t.reshape(bsz, lc, t.shape[-1])
        p_c, u_c, g_c = map(unflat, _in_proj(flat(ctx), mod, ctx_row, w_in_b, layer, qn, kn, gavg, None, bsz * lc))

        a_l = _attention(p_l, 0, p_l, p_c, 2, 3, None, tq=ATTN_Q_TILE, tk=min(ATTN_K_CHUNK, seq))
        b_l = _window_attention(p_l, p_c, sink)
        f_l = _fourier_latent(u_l, fm_bd)
        r_out = _retention(p_l, p_c, rdl, rdh, need_ctx)
        x = _layer_tail(a_l, b_l, f_l, g_l, r_out[:2], x, mod, lat_row, lnw1, lnb1, lnw2, lnb2, w_out_b, w_gu_b,
                        w_dn_b, layer, gnw, gavg, alpha, min(TAIL_TILE, seq))
        if need_ctx:
            a_c = _attention(p_c, 0, None, p_c, 2, 3, None, tq=Q_BLOCK, tk=ATTN_K_CHUNK)
            b_c = _attention(p_c, 2, None, p_c, 6, 7, sink, tq=Q_BLOCK, tk=ATTN_K_CHUNK)
            f_c = _fourier_direct(u_c, fm_bd)
            ctx = _out_proj(flat(a_c), flat(b_c), flat(f_c), flat(g_c), (flat(r_out[2]),), flat(ctx), mod, ctx_row,
                            lnw1, lnb1, w_out_b, layer, gnw, gavg, alpha, bsz * lc)
            ctx = unflat(_ffn(ctx, mod, ctx_row, lnw2, lnb2, w_gu_b, w_dn_b, layer, alpha, bsz * lc))
    return x
```

```python
import functools

import numpy as np
import jax
import jax.numpy as jnp
from jax import lax
from jax.experimental import pallas as pl
from jax.experimental.pallas import tpu as pltpu

F32 = jnp.float32
BF16 = jnp.bfloat16

HEAD_DIM = 64
GRID_W = 64
Q_BLOCK = 128
ROPE_THETA = 10000.0
NORM_EPS = 1e-6
NEG_INF = -1e30
LOG2E = 1.4426950408889634

LANES = 128
MXU_TILE = 256
VMEM_LIMIT_BYTES = 56 * 1024 * 1024

RET_CHUNK = 256
RET_STEP_CHUNKS = 8
FFT_L2 = 128


def _cparams(sem):
    return pltpu.CompilerParams(dimension_semantics=sem, vmem_limit_bytes=VMEM_LIMIT_BYTES)


def _ln(x):
    mu = jnp.mean(x, axis=-1, keepdims=True)
    xc = x - mu
    var = jnp.mean(xc * xc, axis=-1, keepdims=True)
    return xc * lax.rsqrt(var + NORM_EPS)


def _silu(x):
    return x * jax.nn.sigmoid(x)


def _group_mean(t, g):
    hi = t.astype(BF16)
    lo = (t - hi.astype(F32)).astype(BF16)
    return (jnp.dot(hi, g, preferred_element_type=F32) + jnp.dot(lo, g, preferred_element_type=F32))


def _row_slabs(rows, wanted):
    slabs = max(1, min(wanted, rows // MXU_TILE))
    return slabs if rows % (16 * slabs) == 0 else 1


def _dot_nt(a, b):
    return lax.dot_general(a, b, (((1,), (1,)), ((), ())), preferred_element_type=F32)


def _mod_kernel(c_ref, w_ref, b_ref, o_ref):
    h = _silu(c_ref[...])
    w = w_ref[...]
    h_hi = h.astype(BF16)
    h_lo = (h - h_hi.astype(F32)).astype(BF16)
    w_hi = w.astype(BF16)
    w_lo = (w - w_hi.astype(F32)).astype(BF16)
    rows = h.shape[0]
    both = jnp.dot(jnp.concatenate([h_hi, h_lo], axis=0), w_hi, preferred_element_type=F32)
    o_ref[...] = both[:rows] + both[rows:] + jnp.dot(h_hi, w_lo, preferred_element_type=F32) + b_ref[...]


def _modulation(cc, w_mod, b_mod):
    depth, d, n = w_mod.shape
    tn = 2048
    return pl.pallas_call(
        _mod_kernel,
        out_shape=jax.ShapeDtypeStruct((depth, 8, n), F32),
        grid=(depth, n // tn),
        in_specs=[pl.BlockSpec((8, d), lambda l, j: (0, 0)),
                  pl.BlockSpec((None, d, tn), lambda l, j: (l, 0, j)),
                  pl.BlockSpec((None, 1, tn), lambda l, j: (l, 0, j))],
        out_specs=pl.BlockSpec((None, 8, tn), lambda l, j: (l, 0, j)),
        compiler_params=_cparams(("arbitrary", "arbitrary")),
        name="modulation",
    )(cc, w_mod, b_mod.reshape(depth, 1, n))


P_COLS = 14 * LANES


def _rope_lanes(t, c, ss, half):
    first = (lax.broadcasted_iota(jnp.int32, (t.shape[0], LANES), 1) & half) == 0
    outs = []
    for j in range(t.shape[1] // LANES):
        tj = t[:, j * LANES:(j + 1) * LANES]
        partner = jnp.where(first, pltpu.roll(tj, LANES - half, 1), pltpu.roll(tj, half, 1))
        outs.append(tj * c + partner * ss)
    return outs[0] if len(outs) == 1 else jnp.concatenate(outs, axis=1)


IN_ROW_SLABS = 4
IN_TILE = 1024


def _pair_heads_by_kv(q):
    a, b = q[:, 0:LANES], q[:, LANES:2 * LANES]
    lo = lax.broadcasted_iota(jnp.int32, a.shape, 1) < HEAD_DIM
    return jnp.concatenate([jnp.where(lo, a, pltpu.roll(b, HEAD_DIM, 1)),
                            jnp.where(lo, pltpu.roll(a, HEAD_DIM, 1), b)], axis=1)


def _in_kernel(*refs, rope):
    x_ref, sh_ref, sc_ref, w_ref, qn_ref, kn_ref, gavg_ref = refs[:7]
    if rope:
        c2_ref, ss2_ref, c1_ref, ss1_ref, p_ref, u_ref, g_ref = refs[7:]
    else:
        p_ref, u_ref, g_ref = refs[7:]
    def rms(t, w, g):
        return t * lax.rsqrt(_group_mean(t * t, g) + NORM_EPS) * w

    scale = HEAD_DIM ** -0.5
    qscale = scale * LOG2E
    slabs = _row_slabs(x_ref.shape[0], IN_ROW_SLABS)
    rows = x_ref.shape[0] // slabs
    for r in range(slabs):
        rs = slice(r * rows, (r + 1) * rows)

        def rope2(t):
            if not rope:
                return t
            return _rope_lanes(t, c2_ref[rs, :], ss2_ref[rs, :], HEAD_DIM // 4)

        def rope1(t):
            if not rope:
                return t
            return _rope_lanes(t, c1_ref[rs, :], ss1_ref[rs, :], HEAD_DIM // 2)

        h = _ln(x_ref[rs, :]) * (1.0 + sc_ref[...]) + sh_ref[...]
        y = jnp.dot(h.astype(BF16), w_ref[...], preferred_element_type=F32)
        qa = rope2(rms(y[:, 0:256], qn_ref[...], gavg_ref[...])) * qscale
        ka = rope2(rms(y[:, 256:384], kn_ref[...], gavg_ref[0:LANES, 0:LANES]))
        p_ref[rs, 0:256] = _pair_heads_by_kv(qa).astype(BF16)
        p_ref[rs, 256:384] = ka.astype(BF16)
        p_ref[rs, 384:512] = y[:, 384:512].astype(BF16)
        p_ref[rs, 512:768] = _pair_heads_by_kv(rope2(y[:, 512:768]) * qscale).astype(BF16)
        p_ref[rs, 768:896] = rope2(y[:, 768:896]).astype(BF16)
        p_ref[rs, 896:1024] = y[:, 896:1024].astype(BF16)
        u_ref[rs, :] = y[:, 1024:1280].astype(BF16)
        p_ref[rs, 1024:1280] = rope1(y[:, 1280:1536]).astype(BF16)
        p_ref[rs, 1280:1536] = (rope1(y[:, 1536:1792]) * scale).astype(BF16)
        p_ref[rs, 1536:1792] = y[:, 1792:2048].astype(BF16)
        g_ref[rs, :] = y[:, 2048:2304]


def _in_proj(x, mod, mod_row, w, layer, qn, kn, gavg, tabs, tm):
    bsz, length, d = x.shape
    nw = w.shape[2]
    nt = length // tm
    rope = tabs is not None
    row = lambda b, i: (b, i, 0)
    const2 = lambda b, i: (0, 0)
    in_specs = [pl.BlockSpec((None, tm, d), row),
                pl.BlockSpec((None, None, 1, d), lambda b, i: (mod_row(b), 0, 0, 0)),
                pl.BlockSpec((None, None, 1, d), lambda b, i: (mod_row(b), 1, 0, 0)),
                pl.BlockSpec((None, d, nw), lambda b, i: (layer, 0, 0)),
                pl.BlockSpec((1, 256), const2),
                pl.BlockSpec((1, LANES), const2),
                pl.BlockSpec((256, 256), const2)]
    args = [x, mod, mod, w, qn, kn, gavg]
    if rope:
        in_specs += [pl.BlockSpec((tm, LANES), lambda b, i: (i, 0))] * len(tabs)
        args += list(tabs)
    return pl.pallas_call(
        functools.partial(_in_kernel, rope=rope),
        out_shape=(jax.ShapeDtypeStruct((bsz, length, P_COLS), BF16),
                   jax.ShapeDtypeStruct((bsz, length, 256), BF16),
                   jax.ShapeDtypeStruct((bsz, length, 256), F32)),
        grid=(bsz, nt),
        in_specs=in_specs,
        out_specs=(pl.BlockSpec((None, tm, P_COLS), row),
                   pl.BlockSpec((None, tm, 256), row),
                   pl.BlockSpec((None, tm, 256), row)),
        compiler_params=_cparams(("arbitrary", "arbitrary")),
        name="in_proj_rope" if rope else "in_proj_ctx",
    )(*args)


def _stack_heads(q):
    qf = q.astype(F32)
    lo = lax.broadcasted_iota(jnp.int32, (q.shape[0], LANES), 1) < HEAD_DIM
    q0, q1 = qf[:, 0:LANES], qf[:, LANES:2 * LANES]
    z = jnp.zeros_like(q0)
    return jnp.concatenate([jnp.where(lo, q0, z), jnp.where(lo, q1, z),
                            jnp.where(lo, z, q0), jnp.where(lo, z, q1)], axis=0).astype(BF16)


def _aug_values(v):
    vf = v.astype(F32)
    lo = lax.broadcasted_iota(jnp.int32, vf.shape, 1) < HEAD_DIM
    one = jnp.ones_like(vf)
    return jnp.where(lo, vf, one).astype(BF16), jnp.where(lo, one, vf).astype(BF16)


def _finish_heads(acc0, acc1, e, tq):
    l0 = pltpu.roll(acc0, HEAD_DIM, 1)
    l1 = pltpu.roll(acc1, HEAD_DIM, 1)
    if e is not None:
        l0 = l0 + e[:2 * tq]
        l1 = l1 + e[2 * tq:]
    n0 = acc0 / l0
    n1 = acc1 / l1
    lo = lax.broadcasted_iota(jnp.int32, (tq, LANES), 1) < HEAD_DIM
    return jnp.concatenate([jnp.where(lo, n0[:tq], pltpu.roll(n0[tq:], HEAD_DIM, 1)),
                            jnp.where(lo, pltpu.roll(n1[:tq], HEAD_DIM, 1), n1[tq:])], axis=1)


def _sink_column(sink_ref, tq):
    return jnp.concatenate([jnp.full((tq, 1), sink_ref[h] * LOG2E, F32) for h in range(4)], axis=0)


ATTN_Q_TILE = 256
ATTN_K_CHUNK = 2048

def _attn_kernel(*refs, tq, tk, n_lat, has_sink):
    i = 0
    sink_ref = None
    if has_sink:
        sink_ref = refs[0]
        i = 1
    q_ref = refs[i]
    i += 1
    if n_lat:
        kl_ref, vl_ref = refs[i:i + 2]
        i += 2
    kc_ref, vc_ref, o_ref = refs[i:i + 3]
    i += 3
    if n_lat:
        v0l_ref, v1l_ref = refs[i:i + 2]
        i += 2
    v0c_ref, v1c_ref = refs[i:i + 2]

    @pl.when(pl.program_id(1) == 0)
    def _():
        if n_lat:
            a0, a1 = _aug_values(vl_ref[...])
            v0l_ref[...] = a0
            v1l_ref[...] = a1
        a0, a1 = _aug_values(vc_ref[...])
        v0c_ref[...] = a0
        v1c_ref[...] = a1

    qs = _stack_heads(q_ref[...])
    half = 2 * tq

    chunks = [(kl_ref, v0l_ref, v1l_ref, slice(c * tk, (c + 1) * tk)) for c in range(n_lat)]
    chunks.append((kc_ref, v0c_ref, v1c_ref, slice(None)))

    def scores(chunk):
        k_ref, _, _, rows = chunk
        return _dot_nt(qs, k_ref[rows, :])

    m = _sink_column(sink_ref, tq) if has_sink else jnp.full((4 * tq, 1), NEG_INF, F32)
    acc0 = jnp.zeros((half, LANES), F32)
    acc1 = jnp.zeros((half, LANES), F32)
    s_next = scores(chunks[0])
    for idx, (_, v0_ref, v1_ref, rows) in enumerate(chunks):
        s = s_next
        if idx + 1 < len(chunks):
            s_next = scores(chunks[idx + 1])
        m_new = jnp.maximum(m, jnp.max(s, axis=1, keepdims=True))
        alpha = jnp.exp2(m - m_new)
        p = jnp.exp2(s - m_new).astype(BF16)
        pv = jnp.dot(p, jnp.concatenate([v0_ref[rows, :], v1_ref[rows, :]], axis=1), preferred_element_type=F32)
        acc0 = acc0 * alpha[:half] + pv[:half, :LANES]
        acc1 = acc1 * alpha[half:] + pv[half:, LANES:]
        m = m_new
    e = jnp.exp2(_sink_column(sink_ref, tq) - m) if has_sink else None
    o_ref[...] = _finish_heads(acc0, acc1, e, tq).astype(BF16)


def _attention(pq, q_blk, p_lat, p_ctx, k_blk, v_blk, sink, tq, tk):
    bsz, lq, _ = pq.shape
    lc = p_ctx.shape[1]
    assert p_lat is None or p_lat.shape[1] % tk == 0
    n_lat = 0 if p_lat is None else p_lat.shape[1] // tk
    has_sink = sink is not None
    in_specs = [pl.BlockSpec((None, tq, 256), lambda b, i, *_: (b, i, q_blk))]
    args = [pq]
    scratch = []
    if n_lat:
        ll = p_lat.shape[1]
        in_specs += [pl.BlockSpec((None, ll, LANES), lambda b, i, *_: (b, 0, k_blk)),
                     pl.BlockSpec((None, ll, LANES), lambda b, i, *_: (b, 0, v_blk))]
        args += [p_lat, p_lat]
        scratch += [pltpu.VMEM((ll, LANES), BF16), pltpu.VMEM((ll, LANES), BF16)]
    in_specs += [pl.BlockSpec((None, lc, LANES), lambda b, i, *_: (b, 0, k_blk)),
                 pl.BlockSpec((None, lc, LANES), lambda b, i, *_: (b, 0, v_blk))]
    args += [p_ctx, p_ctx]
    scratch += [pltpu.VMEM((lc, LANES), BF16), pltpu.VMEM((lc, LANES), BF16)]
    kern = functools.partial(_attn_kernel, tq=tq, tk=tk, n_lat=n_lat, has_sink=has_sink)
    grid_spec = pltpu.PrefetchScalarGridSpec(
        num_scalar_prefetch=1 if has_sink else 0,
        grid=(bsz, lq // tq),
        in_specs=in_specs,
        out_specs=pl.BlockSpec((None, tq, 256), lambda b, i, *_: (b, i, 0)),
        scratch_shapes=scratch)
    call = pl.pallas_call(
        kern, out_shape=jax.ShapeDtypeStruct((bsz, lq, 256), BF16), grid_spec=grid_spec,
        compiler_params=_cparams(("arbitrary", "arbitrary")),
        name="attn_sink" if has_sink else ("attn_global" if n_lat else "attn_ctx"))
    return call(sink, *args) if has_sink else call(*args)


WIN_LOOKAHEAD = 2
WIN_BLOCKS_PER_STEP = 16


def _win_kernel(sink_ref, q_ref, kl_ref, vl_ref, kc_ref, vc_ref, o_ref, *, nb):
    tq = Q_BLOCK
    r = lax.broadcasted_iota(jnp.int32, (4 * tq, tq), 0) & (tq - 1)
    j = lax.broadcasted_iota(jnp.int32, (4 * tq, tq), 1)
    in_prev = j >= r
    in_next = j <= r
    snk = _sink_column(sink_ref, tq)
    kc, vc = kc_ref[...], vc_ref[...]

    def rows(ref, blk):
        return ref[pl.ds(pl.multiple_of(blk * tq, tq), tq), :]

    def scores(t):
        i = pl.program_id(1) * WIN_BLOCKS_PER_STEP + t
        prev = jnp.maximum(i - 1, 0)
        nxt = jnp.minimum(i + 1, nb - 1)
        k = jnp.concatenate([rows(kl_ref, prev), rows(kl_ref, i), rows(kl_ref, nxt), kc], axis=0)
        qs = _stack_heads(q_ref[t * tq:(t + 1) * tq, :])
        return _dot_nt(qs, k), i, prev, nxt

    pending = [scores(t) for t in range(WIN_LOOKAHEAD)]
    for t in range(WIN_BLOCKS_PER_STEP):
        s, i, prev, nxt = pending.pop(0)
        if t + WIN_LOOKAHEAD < WIN_BLOCKS_PER_STEP:
            pending.append(scores(t + WIN_LOOKAHEAD))
        v = jnp.concatenate([rows(vl_ref, prev), rows(vl_ref, i), rows(vl_ref, nxt), vc], axis=0)
        off_prev = jnp.where(i > 0, 0.0, NEG_INF)
        off_next = jnp.where(i < nb - 1, 0.0, NEG_INF)
        s = jnp.concatenate([jnp.where(in_prev, s[:, 0:tq] + off_prev, NEG_INF), s[:, tq:2 * tq],
                             jnp.where(in_next, s[:, 2 * tq:3 * tq] + off_next, NEG_INF), s[:, 3 * tq:]], axis=1)
        m = jnp.maximum(jnp.max(s, axis=1, keepdims=True), snk)
        p = jnp.exp2(s - m).astype(BF16)
        v0, v1 = _aug_values(v)
        acc0 = jnp.dot(p[:2 * tq], v0, preferred_element_type=F32)
        acc1 = jnp.dot(p[2 * tq:], v1, preferred_element_type=F32)
        o_ref[t * tq:(t + 1) * tq, :] = _finish_heads(acc0, acc1, jnp.exp2(snk - m), tq).astype(BF16)


def _window_attention(p_lat, p_ctx, sink):
    bsz, ll, _ = p_lat.shape
    lc = p_ctx.shape[1]
    nb = ll // Q_BLOCK
    tqs = WIN_BLOCKS_PER_STEP * Q_BLOCK
    assert ll % tqs == 0
    grid_spec = pltpu.PrefetchScalarGridSpec(
        num_scalar_prefetch=1,
        grid=(bsz, ll // tqs),
        in_specs=[pl.BlockSpec((None, tqs, 256), lambda b, i, *_: (b, i, 2)),
                  pl.BlockSpec((None, ll, LANES), lambda b, i, *_: (b, 0, 6)),
                  pl.BlockSpec((None, ll, LANES), lambda b, i, *_: (b, 0, 7)),
                  pl.BlockSpec((None, lc, LANES), lambda b, i, *_: (b, 0, 6)),
                  pl.BlockSpec((None, lc, LANES), lambda b, i, *_: (b, 0, 7))],
        out_specs=pl.BlockSpec((None, tqs, 256), lambda b, i, *_: (b, i, 0)))
    return pl.pallas_call(
        functools.partial(_win_kernel, nb=nb),
        out_shape=jax.ShapeDtypeStruct((bsz, ll, 256), BF16), grid_spec=grid_spec,
        compiler_params=_cparams(("arbitrary", "arbitrary")),
        name="attn_window",
    )(sink, p_lat, p_lat, p_lat, p_ctx, p_ctx)


FFT_ROWS = 8
FFT1_STEP_ROWS = 16


def _fft1_kernel(u_ref, w_ref, y_ref):
    l1, rows, w = u_ref.shape
    uf = u_ref[...].astype(F32)
    ys = []
    for h in range(rows // FFT_ROWS):
        u = uf[:, h * FFT_ROWS:(h + 1) * FFT_ROWS, :].reshape(l1 * FFT_ROWS, w).astype(BF16)
        ys.append(jnp.dot(w_ref[...], u, preferred_element_type=F32).reshape(2, l1, FFT_ROWS, w))
    y_ref[...] = jnp.concatenate(ys, axis=2).astype(BF16)


def _channel_mix(ab, g_ref, fm_ref):
    z = jnp.dot(ab.astype(BF16), g_ref[...], preferred_element_type=F32)
    return jnp.dot(z.astype(BF16), fm_ref[...], preferred_element_type=F32)


def _fft2_kernel(y_ref, c_ref, s_ref, g_ref, fm_ref, o_ref):
    l2 = y_ref.shape[2]
    ab = []
    for r in range(FFT_ROWS):
        yr, yi = y_ref[0, r], y_ref[1, r]
        cs = jnp.concatenate([c_ref[r], s_ref[r]], axis=1)
        rhs = jnp.concatenate([jnp.concatenate([yr, yi], axis=1),
                               jnp.concatenate([yi, -yr], axis=1)], axis=0)
        ab.append(jnp.dot(cs, rhs, preferred_element_type=F32))
    o = _channel_mix(jnp.concatenate(ab, axis=0), g_ref, fm_ref)
    for r in range(FFT_ROWS):
        o_ref[:, r, :] = o[r * l2:(r + 1) * l2]


def _fft_direct_kernel(u_ref, cs_ref, g_ref, fm_ref, o_ref):
    n = u_ref.shape[0]
    y = jnp.dot(cs_ref[...], u_ref[...].astype(BF16), preferred_element_type=F32)
    o_ref[...] = _channel_mix(jnp.concatenate([y[:n], y[n:]], axis=1), g_ref, fm_ref)


def _mxu_const(a):
    return jnp.asarray(a, F32).astype(BF16)


def _dft_tables(n_rows, n_cols, length, row_stride=1, row_offset=0):
    k = row_offset + row_stride * np.arange(n_rows, dtype=np.int64)
    n = np.arange(n_cols, dtype=np.int64)
    ang = 2.0 * np.pi * ((k[:, None] * n[None, :]) % length).astype(np.float64) / length
    return np.cos(ang), np.sin(ang)


def _channel_dft(width, length):
    c, s = _dft_tables(HEAD_DIM, HEAD_DIM, HEAD_DIM)
    eye = np.eye(width // HEAD_DIM) / np.sqrt(float(length) * HEAD_DIM)
    return _mxu_const(np.concatenate([np.kron(eye, c), np.kron(eye, s)], axis=0))


def _fourier_latent(u, fm_bd):
    bsz, length, w = u.shape
    l2 = FFT_L2
    l1 = length // l2
    rows = FFT_ROWS
    c1, s1 = _dft_tables(l1, l1, l1)
    w1 = _mxu_const(np.kron(np.concatenate([c1, -s1], axis=0), np.eye(rows)))
    y = pl.pallas_call(
        _fft1_kernel,
        out_shape=jax.ShapeDtypeStruct((bsz, 2, l1, l2, w), BF16),
        grid=(bsz, l2 // FFT1_STEP_ROWS),
        in_specs=[pl.BlockSpec((None, l1, FFT1_STEP_ROWS, w), lambda b, j: (b, 0, j, 0)),
                  pl.BlockSpec(w1.shape, lambda b, j: (0, 0))],
        out_specs=pl.BlockSpec((None, 2, l1, FFT1_STEP_ROWS, w), lambda b, j: (b, 0, 0, j, 0)),
        compiler_params=_cparams(("arbitrary", "arbitrary")),
        name="fourier_stage1",
    )(u.reshape(bsz, l1, l2, w), w1)
    tabs = [_dft_tables(l2, l2, length, row_stride=l1, row_offset=k1) for k1 in range(l1)]
    ck = _mxu_const(np.stack([t[0] for t in tabs]))
    sk = _mxu_const(np.stack([t[1] for t in tabs]))
    const2 = lambda b, k: (0, 0)
    out = pl.pallas_call(
        _fft2_kernel,
        out_shape=jax.ShapeDtypeStruct((bsz, l2, l1, w), F32),
        grid=(bsz, l1 // rows),
        in_specs=[pl.BlockSpec((None, 2, rows, l2, w), lambda b, k: (b, 0, k, 0, 0)),
                  pl.BlockSpec((rows, l2, l2), lambda b, k: (k, 0, 0)),
                  pl.BlockSpec((rows, l2, l2), lambda b, k: (k, 0, 0)),
                  pl.BlockSpec((2 * w, w), const2), pl.BlockSpec((w, w), const2)],
        out_specs=pl.BlockSpec((None, l2, rows, w), lambda b, k: (b, 0, k, 0)),
        compiler_params=_cparams(("arbitrary", "arbitrary")),
        name="fourier_stage2",
    )(y, ck, sk, _channel_dft(w, length), fm_bd)
    return out.reshape(bsz, length, w)


def _fourier_direct(u, fm_bd):
    bsz, length, w = u.shape
    c, s = _dft_tables(length, length, length)
    cs = _mxu_const(np.concatenate([c, -s], axis=0))
    const2 = lambda b: (0, 0)
    return pl.pallas_call(
        _fft_direct_kernel,
        out_shape=jax.ShapeDtypeStruct((bsz, length, w), F32),
        grid=(bsz,),
        in_specs=[pl.BlockSpec((None, length, w), lambda b: (b, 0, 0)),
                  pl.BlockSpec((2 * length, length), const2),
                  pl.BlockSpec((2 * w, w), const2), pl.BlockSpec((w, w), const2)],
        out_specs=pl.BlockSpec((None, length, w), lambda b: (b, 0, 0)),
        compiler_params=_cparams(("arbitrary",)),
        name="fourier_ctx",
    )(u, cs, _channel_dft(w, length), fm_bd)


def _log_sigmoid(x):
    return jnp.minimum(x, 0.0) - jnp.log1p(jnp.exp(-jnp.abs(x)))


def _ret_kernel(*refs, need_ctx):
    (rdl_ref, rdh_ref, qf_ref, kf_ref, vf_ref, qb_ref, kb_ref, vb_ref, qc_ref, kc_ref, vc_ref) = refs[:11]
    if need_ctx:
        of_ref, ob_ref, oc_ref = refs[11:14]
        scr = refs[14:]
    else:
        of_ref, ob_ref = refs[11:13]
        oc_ref = None
        scr = refs[13:]
    sf_ref, sb_ref, din_ref, tab_ref = scr
    c = RET_CHUNK
    w = 4 * HEAD_DIM
    j = pl.program_id(1)
    head_shift = HEAD_DIM.bit_length() - 1
    lane_head = lax.broadcasted_iota(jnp.int32, (c, w), 1) >> head_shift
    blockdiag = ((lax.broadcasted_iota(jnp.int32, (w, w), 0) >> head_shift)
                 == (lax.broadcasted_iota(jnp.int32, (w, w), 1) >> head_shift))

    def decayed_scores(q, k, d):
        qf = q.astype(F32)
        inner = []
        for h in range(4):
            qh = jnp.where(lane_head == h, qf, 0.0).astype(BF16)
            inner.append((_dot_nt(qh, k) * din_ref[d, h]).astype(BF16))
        return inner, (k.astype(F32) * tab_ref[d, 1]).T.astype(BF16)

    def chunk_outputs(scores, v):
        inner, kz = scores
        o = jnp.zeros((c, w), F32)
        for h in range(4):
            o = o + jnp.where(lane_head == h, jnp.dot(inner[h], v, preferred_element_type=F32), 0.0)
        kv = jnp.where(blockdiag, jnp.dot(kz, v, preferred_element_type=F32), 0.0)
        return o, kv

    def chunk(q, k, v, d):
        return chunk_outputs(decayed_scores(q, k, d), v)

    @pl.when(j == 0)
    def _():
        t = lax.broadcasted_iota(jnp.int32, (c, w), 0).astype(F32)
        rr = lax.broadcasted_iota(jnp.int32, (c, c), 0)
        cc = lax.broadcasted_iota(jnp.int32, (c, c), 1)
        for d in range(2):
            lg = _log_sigmoid(rdl_ref[d])
            tab_ref[d, 0] = jnp.exp(lg * ((t + 1.0) if d == 0 else (c - t)))
            tab_ref[d, 1] = jnp.exp(lg * ((c - 1.0 - t) if d == 0 else t))
            tab_ref[d, 2] = jnp.exp(jnp.broadcast_to(lg, (c, w)) * float(c))
            diff = (rr - cc) if d == 0 else (cc - rr)
            dpos = jnp.maximum(diff, 0).astype(F32)
            for h in range(4):
                lgh = _log_sigmoid(rdh_ref[d, h])
                din_ref[d, h] = jnp.where(diff >= 0, jnp.exp(lgh * dpos), 0.0)
        q, k, v = qc_ref[...], kc_ref[...], vc_ref[...]
        o_f, kv_f = chunk(q, k, v, 0)
        o_b, kv_b = chunk(q, k, v, 1)
        sf_ref[...] = kv_f
        sb_ref[...] = kv_b
        if need_ctx:
            oc_ref[...] = o_f + o_b

    @pl.when(j > 0)
    def _():
        n_sub = qf_ref.shape[0] // c
        io = ((qf_ref, kf_ref, vf_ref, of_ref), (qb_ref, kb_ref, vb_ref, ob_ref))
        states = [sf_ref[...], sb_ref[...]]
        work = [(d, t if d == 0 else n_sub - 1 - t) for t in range(n_sub) for d in range(2)]

        def start(d, t):
            rs = slice(t * c, (t + 1) * c)
            q, k, v = io[d][0][rs, :], io[d][1][rs, :], io[d][2][rs, :]
            return q, v, decayed_scores(q, k, d)

        ahead = start(*work[0])
        for idx, (d, t) in enumerate(work):
            q, v, scores = ahead
            if idx + 1 < len(work):
                ahead = start(*work[idx + 1])
            o, kv = chunk_outputs(scores, v)
            cross = jnp.dot(q, states[d].astype(BF16), preferred_element_type=F32) * tab_ref[d, 0]
            io[d][3][t * c:(t + 1) * c, :] = o + cross
            states[d] = states[d] * tab_ref[d, 2, 0:1, :] + kv
        sf_ref[...] = states[0]
        sb_ref[...] = states[1]


def _retention(p_lat, p_ctx, rdl, rdh, need_ctx):
    bsz, ll, _ = p_lat.shape
    lc = p_ctx.shape[1]
    c = RET_CHUNK
    w = 4 * HEAD_DIM
    step_chunks = next(k for k in (RET_STEP_CHUNKS, 4, 2, 1) if ll % (c * k) == 0)
    cs = c * step_chunks
    assert lc == c and ll % cs == 0
    n = ll // cs
    fwd = lambda blk: (lambda b, j: (b, jnp.maximum(j - 1, 0), blk))
    bwd = lambda blk: (lambda b, j: (b, n - 1 - jnp.maximum(j - 1, 0), blk))
    ctx = lambda blk: (lambda b, j: (b, 0, blk))
    in_specs = [pl.BlockSpec((2, 1, w), lambda b, j: (0, 0, 0)),
                pl.BlockSpec((2, 4, 1, c), lambda b, j: (0, 0, 0, 0))]
    in_specs += [pl.BlockSpec((None, cs, w), fwd(blk)) for blk in (4, 5, 6)]
    in_specs += [pl.BlockSpec((None, cs, w), bwd(blk)) for blk in (4, 5, 6)]
    in_specs += [pl.BlockSpec((None, c, w), ctx(blk)) for blk in (4, 5, 6)]
    out_shape = [jax.ShapeDtypeStruct((bsz, ll, w), F32), jax.ShapeDtypeStruct((bsz, ll, w), F32)]
    out_specs = [pl.BlockSpec((None, cs, w), fwd(0)), pl.BlockSpec((None, cs, w), bwd(0))]
    if need_ctx:
        out_shape.append(jax.ShapeDtypeStruct((bsz, lc, w), F32))
        out_specs.append(pl.BlockSpec((None, c, w), ctx(0)))
    return pl.pallas_call(
        functools.partial(_ret_kernel, need_ctx=need_ctx),
        out_shape=tuple(out_shape),
        grid=(bsz, n + 1),
        in_specs=in_specs,
        out_specs=tuple(out_specs),
        scratch_shapes=[pltpu.VMEM((w, w), F32), pltpu.VMEM((w, w), F32),
                        pltpu.VMEM((2, 4, c, c), F32), pltpu.VMEM((2, 3, c, w), F32)],
        compiler_params=_cparams(("arbitrary", "arbitrary")),
        name="retention_ctx_out" if need_ctx else "retention",
    )(rdl, rdh, *([p_lat] * 6), *([p_ctx] * 3))


OUT_ROW_SLABS = 4
OUT_TILE = 1024


def _out_kernel(*refs, n_o, alpha):
    a_ref, b_ref, f_ref, g_ref = refs[:4]
    o_refs = refs[4:4 + n_o]
    x_ref, g1_ref, lnw_ref, lnb_ref, w_ref, gnw_ref, gavg_ref, out_ref = refs[4 + n_o:]
    gavg = gavg_ref[...]
    slabs = _row_slabs(x_ref.shape[0], OUT_ROW_SLABS)
    rows = x_ref.shape[0] // slabs
    def mixer_outputs(s):
        rs = slice(s * rows, (s + 1) * rows)
        o = o_refs[0][rs, :]
        for r in o_refs[1:]:
            o = o + r[rs, :]
        dlt = o - _group_mean(o, gavg)
        on = dlt * lax.rsqrt(_group_mean(dlt * dlt, gavg) + NORM_EPS) * gnw_ref[...]
        ret = (_silu(g_ref[rs, :]) * on).astype(BF16)
        return jnp.concatenate([a_ref[rs, :], b_ref[rs, :], f_ref[rs, :].astype(BF16), ret], axis=1)

    ahead = mixer_outputs(0)
    for s in range(slabs):
        rs = slice(s * rows, (s + 1) * rows)
        cat = ahead
        if s + 1 < slabs:
            ahead = mixer_outputs(s + 1)
        y = jnp.dot(cat, w_ref[...], preferred_element_type=F32)
        z = alpha * x_ref[rs, :] + g1_ref[...] * y
        out_ref[rs, :] = _ln(z) * lnw_ref[...] + lnb_ref[...]


def _out_proj(a, b, f, g, o_parts, x, mod, mod_row, lnw, lnb, w, layer, gnw, gavg, alpha, tm):
    bsz, length, d = x.shape
    row = lambda bb, i: (bb, i, 0)
    const2 = lambda bb, i: (0, 0)
    blk256 = pl.BlockSpec((None, tm, 256), row)
    in_specs = [blk256] * (4 + len(o_parts)) + [
        pl.BlockSpec((None, tm, d), row),
        pl.BlockSpec((None, None, 1, d), lambda bb, i: (mod_row(bb), 2, 0, 0)),
        pl.BlockSpec((1, d), const2), pl.BlockSpec((1, d), const2),
        pl.BlockSpec((None, d, d), lambda bb, i: (layer, 0, 0)),
        pl.BlockSpec((1, 256), const2), pl.BlockSpec((256, 256), const2)]
    return pl.pallas_call(
        functools.partial(_out_kernel, n_o=len(o_parts), alpha=alpha),
        out_shape=jax.ShapeDtypeStruct((bsz, length, d), F32),
        grid=(bsz, length // tm),
        in_specs=in_specs,
        out_specs=pl.BlockSpec((None, tm, d), row),
        compiler_params=_cparams(("arbitrary", "arbitrary")),
        name="out_proj",
    )(a, b, f, g, *o_parts, x, mod, lnw, lnb, w, gnw, gavg)


FFN_ROW_SLABS = 4
FFN_TILE = 1024


def _ffn_kernel(x_ref, sh_ref, sc_ref, g2_ref, lnw_ref, lnb_ref, wg_ref, wu_ref, wd_ref, out_ref, *, fc, alpha):
    slabs = _row_slabs(x_ref.shape[0], FFN_ROW_SLABS)
    rows = x_ref.shape[0] // slabs
    n_chunks = wg_ref.shape[1] // fc
    work = [(r, c) for r in range(slabs) for c in range(n_chunks)]
    xs, hs, accs = {}, {}, {}

    def gate_up(r, c):
        if r not in hs:
            xs[r] = x_ref[r * rows:(r + 1) * rows, :]
            hs[r] = (_ln(xs[r]) * (1.0 + sc_ref[...]) + sh_ref[...]).astype(BF16)
        cols = slice(c * fc, (c + 1) * fc)
        return (jnp.dot(hs[r], wg_ref[:, cols], preferred_element_type=F32),
                jnp.dot(hs[r], wu_ref[:, cols], preferred_element_type=F32))

    ahead = gate_up(*work[0])
    for idx, (r, c) in enumerate(work):
        gate, up = ahead
        if idx + 1 < len(work):
            ahead = gate_up(*work[idx + 1])
        act = (_silu(gate) * up).astype(BF16)
        down = jnp.dot(act, wd_ref[c * fc:(c + 1) * fc, :], preferred_element_type=F32)
        accs[r] = down if c == 0 else accs[r] + down
        if c == n_chunks - 1:
            z = alpha * xs[r] + g2_ref[...] * accs[r]
            out_ref[r * rows:(r + 1) * rows, :] = _ln(z) * lnw_ref[...] + lnb_ref[...]


def _ffn(x, mod, mod_row, lnw, lnb, wgu, wd, layer, alpha, tm):
    bsz, length, d = x.shape
    ff = wd.shape[1]
    fc = ff // 2 if (ff // 2) % MXU_TILE == 0 else ff
    row = lambda bb, i: (bb, i, 0)
    const2 = lambda bb, i: (0, 0)
    modspec = lambda which: pl.BlockSpec((None, None, 1, d), lambda bb, i: (mod_row(bb), which, 0, 0))
    resident = lambda shape, col: pl.BlockSpec(shape, lambda bb, i: (layer, 0, col), pipeline_mode=pl.Buffered(1))
    return pl.pallas_call(
        functools.partial(_ffn_kernel, fc=fc, alpha=alpha),
        out_shape=jax.ShapeDtypeStruct((bsz, length, d), F32),
        grid=(bsz, length // tm),
        in_specs=[pl.BlockSpec((None, tm, d), row), modspec(3), modspec(4), modspec(5),
                  pl.BlockSpec((1, d), const2), pl.BlockSpec((1, d), const2),
                  resident((None, d, ff), 0), resident((None, d, ff), 1), resident((None, ff, d), 0)],
        out_specs=pl.BlockSpec((None, tm, d), row),
        compiler_params=_cparams(("arbitrary", "arbitrary")),
        name="ffn",
    )(x, mod, mod, mod, lnw, lnb, wgu, wgu, wd)


def _rope_tables(seq):
    t = np.arange(seq)
    f32 = np.float32

    def tab(pos, n_freq):
        inv = f32(ROPE_THETA) ** (-np.arange(n_freq, dtype=f32) / f32(n_freq))
        ang = (pos.astype(f32)[:, None] * inv[None, :]).astype(np.float64)
        return np.cos(ang), np.sin(ang)

    cr, sr = tab(t // GRID_W, HEAD_DIM // 4)
    cc, sc = tab(t % GRID_W, HEAD_DIM // 4)
    ct, st = tab(t, HEAD_DIM // 2)
    tables = (np.concatenate([cr, cr, cc, cc], -1), np.concatenate([-sr, sr, -sc, sc], -1),
              np.concatenate([ct, ct], -1), np.concatenate([-st, st], -1))
    return tuple(jnp.asarray(np.tile(a, (1, 2)), F32) for a in tables)


def kernel(x, c, ctx, c_ctx, w_mod, b_mod, w_in, a_q_norm, a_k_norm, b_sink, f_mix, r_decay, r_gn_w, w_out,
           ln1_w, ln1_b, w_gate_up, w_down, ln2_w, ln2_b):
    bsz, seq, d = x.shape
    depth = w_in.shape[0]
    gw = d // 4
    assert gw == 4 * HEAD_DIM and a_q_norm.shape[-1] == HEAD_DIM and seq % (FFT_L2 * 8) == 0
    alpha = (2.0 * depth) ** 0.25

    tabs = _rope_tables(seq)
    gavg = jnp.asarray(np.kron(np.eye(gw // HEAD_DIM), np.full((HEAD_DIM, HEAD_DIM), 1.0 / HEAD_DIM)), BF16)
    cc = jnp.zeros((8, d), F32).at[:bsz].set(c).at[bsz].set(c_ctx)
    mod_all = _modulation(cc, w_mod, b_mod).reshape(depth, 8, 6, 1, d)
    lat_row = lambda b: b
    ctx_row = lambda b: bsz

    eye_g = jnp.eye(gw // HEAD_DIM, dtype=F32)
    w_in_b, w_out_b = w_in.astype(BF16), w_out.astype(BF16)
    w_gu_b, w_dn_b = w_gate_up.astype(BF16), w_down.astype(BF16)

    for layer in range(depth):
        need_ctx = layer < depth - 1
        mod = mod_all[layer]
        qn = jnp.tile(a_q_norm[layer], 4)[None, :]
        kn = jnp.tile(a_k_norm[layer], 2)[None, :]
        gnw = r_gn_w[layer][None, :]
        lnw1, lnb1 = ln1_w[layer][None, :], ln1_b[layer][None, :]
        lnw2, lnb2 = ln2_w[layer][None, :], ln2_b[layer][None, :]
        fm_bd = jnp.einsum('gh,gce->gche', eye_g, f_mix[layer]).reshape(gw, gw).astype(BF16)
        rd = r_decay[layer]
        rdl = jnp.repeat(rd, HEAD_DIM, axis=1)[:, None, :]
        rdh = jnp.broadcast_to(rd[:, :, None, None], (2, 4, 1, RET_CHUNK))
        sink = b_sink[layer]

        p_l, u_l, g_l = _in_proj(x, mod, lat_row, w_in_b, layer, qn, kn, gavg, tabs, min(IN_TILE, seq))
        lc = ctx.shape[1]
        flat = lambda t: t.reshape(1, bsz * lc, t.shape[-1])
        unflat = lambda t: t.reshape(bsz, lc, t.shape[-1])
        p_c, u_c, g_c = map(unflat, _in_proj(flat(ctx), mod, ctx_row, w_in_b, layer, qn, kn, gavg, None, bsz * lc))

        a_l = _attention(p_l, 0, p_l, p_c, 2, 3, None, tq=ATTN_Q_TILE, tk=min(ATTN_K_CHUNK, seq))
        b_l = _window_attention(p_l, p_c, sink)
        f_l = _fourier_latent(u_l, fm_bd)
        r_out = _retention(p_l, p_c, rdl, rdh, need_ctx)
        x = _out_proj(a_l, b_l, f_l, g_l, r_out[:2], x, mod, lat_row, lnw1, lnb1, w_out_b, layer, gnw, gavg, alpha,
                      min(OUT_TILE, seq))
        x = _ffn(x, mod, lat_row, lnw2, lnb2, w_gu_b, w_dn_b, layer, alpha, min(FFN_TILE, seq))
        if need_ctx:
            a_c = _attention(p_c, 0, None, p_c, 2, 3, None, tq=Q_BLOCK, tk=ATTN_K_CHUNK)
            b_c = _attention(p_c, 2, None, p_c, 6, 7, sink, tq=Q_BLOCK, tk=ATTN_K_CHUNK)
            f_c = _fourier_direct(u_c, fm_bd)
            ctx = _out_proj(flat(a_c), flat(b_c), flat(f_c), flat(g_c), (flat(r_out[2]),), flat(ctx), mod, ctx_row,
                            lnw1, lnb1, w_out_b, layer, gnw, gavg, alpha, bsz * lc)
            ctx = unflat(_ffn(ctx, mod, ctx_row, lnw2, lnb2, w_gu_b, w_dn_b, layer, alpha, bsz * lc))
    return x
```
